```python
import jax, jax.numpy as jnp
from jax import lax
import numpy as np

D_MODEL = 1024
BATCH = 16
SEQ = 2048
DEPTH = 1
DEC_BATCH = 128
DEC_SEQ = 4
PAST_LEN = 8192
PAGE_SIZE = 128

MIX_WIDTH = D_MODEL
NSA_WIDTH = MIX_WIDTH // 2
POOL_WIDTH = MIX_WIDTH - NSA_WIDTH
HEAD_DIM = 64
N_HEADS = NSA_WIDTH // HEAD_DIM
N_KV_HEADS = 2
GROUP = N_HEADS // N_KV_HEADS
ROPE_DIM = HEAD_DIM // 4
ROPE_THETA = 500000.0
CMP_BLOCK = 32
CMP_STRIDE = 16
CMP_HIDDEN = 256
SLC_BLOCK = 64
TOPK_BLOCKS = 16
WINDOW = 512
Q_BLOCK = 128
POOL_WINDOWS = (2, 4, 8, 16)
POOL_GROUP_WIDTH = POOL_WIDTH // len(POOL_WINDOWS)
POOL_STATE = max(POOL_WINDOWS) - 1
D_FF = 2816
RMS_EPS = 1e-6
FORCE_SCORE = 1e4
NEG_INF = -1e30
ATTN_SCALE = HEAD_DIM ** -0.5

KV_WIDTH = 2 * N_KV_HEADS * HEAD_DIM
OFF_KV_CMP = NSA_WIDTH
OFF_KV_SLC = OFF_KV_CMP + KV_WIDTH
OFF_KV_WIN = OFF_KV_SLC + KV_WIDTH
OFF_GATE = OFF_KV_WIN + KV_WIDTH
OFF_POOL = OFF_GATE + 3 * N_HEADS
IN_WIDTH = OFF_POOL + POOL_WIDTH

kernel_name = 'nsa_pool_macaron_hybrid_step'


def rmsnorm(x, g):
    x32 = x.astype(jnp.float32)
    y = x32 * lax.rsqrt(jnp.mean(x32 * x32, axis=-1, keepdims=True) + RMS_EPS)
    return (y * g.astype(jnp.float32)).astype(x.dtype)


def swiglu_ffn(x, g, w_gate, w_up, w_down):
    h = rmsnorm(x, g)
    return (jax.nn.silu(h @ w_gate) * (h @ w_up)) @ w_down


def rope(x, pos):
    half = ROPE_DIM // 2
    inv = jnp.power(ROPE_THETA, -jnp.arange(half, dtype=jnp.float32) / half)
    ang = pos.astype(jnp.float32)[:, None] * inv[None, :]
    cos = jnp.cos(ang)[:, None, :].astype(x.dtype)
    sin = jnp.sin(ang)[:, None, :].astype(x.dtype)
    x1 = x[..., :half]
    x2 = x[..., half:ROPE_DIM]
    return jnp.concatenate([x1 * cos - x2 * sin, x1 * sin + x2 * cos, x[..., ROPE_DIM:]], axis=-1)


def masked_softmax(s, mask):
    s = jnp.where(mask, s, NEG_INF)
    m = jnp.max(s, axis=-1, keepdims=True)
    e = jnp.where(mask, jnp.exp(s - m), 0.0)
    return e / jnp.maximum(jnp.sum(e, axis=-1, keepdims=True), 1e-30)


def pad_rows(x, mult):
    pad = (-x.shape[1]) % mult
    return jnp.pad(x, [(0, 0), (0, pad)] + [(0, 0)] * (x.ndim - 2))


def in_project(h, w_in, pos):
    B, T, _ = h.shape
    z = h @ w_in
    q = rope(z[..., :OFF_KV_CMP].reshape(B, T, N_HEADS, HEAD_DIM), pos)

    def kv(off):
        r = z[..., off:off + KV_WIDTH].reshape(B, T, 2, N_KV_HEADS, HEAD_DIM)
        return jnp.stack([rope(r[:, :, 0], pos), r[:, :, 1]], axis=2)

    gates = jax.nn.sigmoid(z[..., OFF_GATE:OFF_POOL].astype(jnp.float32)).astype(h.dtype)
    gates = gates.reshape(B, T, N_HEADS, 3)
    u = z[..., OFF_POOL:]
    return q, kv(OFF_KV_CMP), kv(OFF_KV_SLC), kv(OFF_KV_WIN), gates, u


def cmp_chunk_proj(k, w1):
    B, T, G, D = k.shape
    chunks = k.reshape(B, T // CMP_STRIDE, CMP_STRIDE, G, D)
    w = w1.reshape(CMP_BLOCK // CMP_STRIDE, CMP_STRIDE, D, CMP_HIDDEN)
    return jnp.einsum('bcsgd,hsdf->bcghf', chunks, w)


def cmp_blocks(proj, w1, w2, pe):
    n_slots = CMP_BLOCK // CMP_STRIDE
    n_blk = proj.shape[1] - n_slots + 1
    acc = sum(proj[:, h:h + n_blk, :, h] for h in range(n_slots))
    bias = pe.reshape(-1) @ w1
    return jax.nn.gelu(acc + bias) @ w2


def compress_kv(row_list, cmpw):
    w_k1, w_k2, pe_k, w_v1, w_v2, pe_v = cmpw

    def one(i, w1, w2, pe):
        proj = jnp.concatenate([cmp_chunk_proj(r[:, :, i], w1) for r in row_list], axis=1)
        return cmp_blocks(proj, w1, w2, pe)

    return one(0, w_k1, w_k2, pe_k), one(1, w_v1, w_v2, pe_v)


def compressed_attention(q, kc, vc, q_pos):
    B, T = q.shape[:2]
    n_blk = kc.shape[1]
    qg = q.reshape(B, T, N_KV_HEADS, GROUP, HEAD_DIM)
    s = jnp.einsum('btgrd,bngd->bgrtn', qg, kc).astype(jnp.float32) * ATTN_SCALE
    blk_end = jnp.arange(n_blk) * CMP_STRIDE + CMP_BLOCK - 1
    p = masked_softmax(s, blk_end[None, :] <= q_pos[:, None])
    o = jnp.einsum('bgrtn,bngd->btgrd', p.astype(vc.dtype), vc).reshape(B, T, N_HEADS, HEAD_DIM)
    return o, jnp.sum(p, axis=2)


def select_blocks(p_grp, q_pos, n_slc):
    n_cmp = p_grp.shape[-1]
    c0 = jnp.arange(n_cmp)[:, None] * CMP_STRIDE
    s0 = jnp.arange(n_slc)[None, :] * SLC_BLOCK
    overlap = jnp.clip(jnp.minimum(c0 + CMP_BLOCK, s0 + SLC_BLOCK) - jnp.maximum(c0, s0), 0, None)
    agg = (overlap / CMP_BLOCK).astype(jnp.float32)
    p_slc = jnp.einsum('bgtn,nj->bgtj', p_grp, agg)
    cur = (q_pos // SLC_BLOCK)[:, None]
    j = jnp.arange(n_slc)[None, :]
    forced = (j == 0) | (j == cur) | (j == cur - 1)
    score = jnp.where(forced, FORCE_SCORE, jnp.where(j <= cur, p_slc, NEG_INF))
    _, idx = lax.top_k(score, min(TOPK_BLOCKS, n_slc))
    return idx.astype(jnp.int32)


def selected_core(q, kg, vg, idx, q_pos):
    Q = q.shape[0]
    s = jnp.einsum('qgrd,gqkpd->gqrkp', q, kg).astype(jnp.float32) * ATTN_SCALE
    G_, _, R_, K_, P_ = s.shape
    kpos = idx[..., None] * SLC_BLOCK + jnp.arange(SLC_BLOCK)
    mask = (kpos <= q_pos[None, :, None, None]).reshape(G_, Q, 1, K_ * P_)
    p = masked_softmax(s.reshape(G_, Q, R_, K_ * P_), mask).reshape(s.shape)
    o = jnp.einsum('gqrkp,gqkpd->qgrd', p.astype(vg.dtype), vg)
    return o.reshape(Q, N_HEADS, HEAD_DIM)


def selected_prompt(q, kv_slc, idx, pos):
    B, S = q.shape[:2]
    qb_len = min(Q_BLOCK, S)
    nqb = S // qb_len
    kvb = kv_slc.reshape(B, S // SLC_BLOCK, SLC_BLOCK, 2, N_KV_HEADS, HEAD_DIM)
    qs = q.reshape(B * nqb, qb_len, N_KV_HEADS, GROUP, HEAD_DIM)
    ib = idx.reshape(B, N_KV_HEADS, nqb, qb_len, -1).transpose(0, 2, 1, 3, 4)
    ib = ib.reshape(B * nqb, N_KV_HEADS, qb_len, -1)
    pb = jnp.broadcast_to(pos.reshape(1, nqb, qb_len), (B, nqb, qb_len)).reshape(B * nqb, qb_len)
    bi = jnp.repeat(jnp.arange(B, dtype=jnp.int32), nqb)
    gi = jnp.arange(N_KV_HEADS)[:, None, None]

    def body(args):
        qb, ibk, pq, b = args
        blk = kvb[b, ibk, :, :, gi]
        return selected_core(qb, blk[..., 0, :], blk[..., 1, :], ibk, pq)

    o = lax.map(body, (qs, ib, pb, bi))
    return o.reshape(B, S, N_HEADS, HEAD_DIM)


def selected_sample(q, cache_kv_slc, kv_new, page_table, idx, q_pos):
    DB, DS = q.shape[:2]
    past_len = page_table.shape[1] * PAGE_SIZE
    npb = past_len // SLC_BLOCK
    bpp = PAGE_SIZE // SLC_BLOCK
    pool = cache_kv_slc.reshape(-1, bpp, SLC_BLOCK, 2, N_KV_HEADS, HEAD_DIM)
    newb = pad_rows(kv_new, SLC_BLOCK).reshape(DB, -1, SLC_BLOCK, 2, N_KV_HEADS, HEAD_DIM)
    nnb = newb.shape[1]
    bi = jnp.arange(DB)[:, None, None, None]
    gi = jnp.arange(N_KV_HEADS)[None, :, None, None]
    jp = jnp.minimum(idx, npb - 1)
    phys = page_table[bi, jp // bpp]
    past_blk = pool[phys, jp % bpp, :, :, gi]
    jn = jnp.clip(idx - npb, 0, nnb - 1)
    new_blk = newb[bi, jn, :, :, gi]
    blk = jnp.where((idx >= npb)[..., None, None, None], new_blk, past_blk)
    qs = q.reshape(DB, DS, N_KV_HEADS, GROUP, HEAD_DIM)
    return jax.vmap(selected_core, in_axes=(0, 0, 0, 0, None))(
        qs, blk[..., 0, :], blk[..., 1, :], idx, q_pos)


def window_core(q, kv, q_pos, k_pos):
    B, Q = q.shape[:2]
    qg = q.reshape(B, Q, N_KV_HEADS, GROUP, HEAD_DIM)
    s = jnp.einsum('bqgrd,bkgd->bgrqk', qg, kv[:, :, 0]).astype(jnp.float32) * ATTN_SCALE
    diff = q_pos[:, None] - k_pos[None, :]
    mask = (diff >= 0) & (diff <= WINDOW) & (k_pos[None, :] >= 0)
    p = masked_softmax(s, mask)
    o = jnp.einsum('bgrqk,bkgd->bqgrd', p.astype(kv.dtype), kv[:, :, 1])
    return o.reshape(B, Q, N_HEADS, HEAD_DIM)


def window_prompt(q, kv_win):
    B, S = q.shape[:2]
    qb_len = min(Q_BLOCK, S)
    nqb = S // qb_len
    kvp = jnp.pad(kv_win, ((0, 0), (WINDOW, 0), (0, 0), (0, 0), (0, 0)))

    def body(n):
        start = n * qb_len
        qb = lax.dynamic_slice_in_dim(q, start, qb_len, axis=1)
        kb = lax.dynamic_slice_in_dim(kvp, start, WINDOW + qb_len, axis=1)
        q_pos = start + jnp.arange(qb_len)
        k_pos = start - WINDOW + jnp.arange(WINDOW + qb_len)
        return window_core(qb, kb, q_pos, k_pos)

    o = lax.map(body, jnp.arange(nqb))
    return o.transpose(1, 0, 2, 3, 4).reshape(B, S, N_HEADS, HEAD_DIM)


def pool_mixer(u_ext, pos_ext, n_out, w_pool, pool_scale):
    T = u_ext.shape[1]
    u32 = u_ext.astype(jnp.float32)
    cs = jnp.pad(jnp.cumsum(u32, axis=1), ((0, 0), (1, 0), (0, 0)))
    t = jnp.arange(T - n_out, T)
    pos = pos_ext[T - n_out:]
    outs = []
    for g, w in enumerate(POOL_WINDOWS):
        c0, c1 = g * POOL_GROUP_WIDTH, (g + 1) * POOL_GROUP_WIDTH
        lo = jnp.maximum(t + 1 - w, 0)
        win_sum = cs[:, t + 1, c0:c1] - cs[:, lo, c0:c1]
        cnt = jnp.minimum(w, pos + 1).astype(jnp.float32)[None, :, None]
        d = (win_sum / cnt - u32[:, T - n_out:, c0:c1]).astype(u_ext.dtype)
        outs.append(d @ w_pool[g])
    return jnp.concatenate(outs, axis=-1) * pool_scale


def merge_heads(o_c, o_s, o_w, gates, pool_out, w_out):
    B, T = o_c.shape[:2]
    o = gates[..., 0:1] * o_c + gates[..., 1:2] * o_s + gates[..., 2:3] * o_w
    return jnp.concatenate([o.reshape(B, T, NSA_WIDTH), pool_out], axis=-1) @ w_out


def prompt_mixer(h, pos, w_in, cmpw, w_pool, pool_scale, w_out):
    S = h.shape[1]
    q, kv_c, kv_s, kv_w, gates, u = in_project(h, w_in, pos)
    kc, vc = compress_kv([pad_rows(kv_c, CMP_STRIDE)], cmpw)
    o_c, p_grp = compressed_attention(q, kc, vc, pos)
    idx = select_blocks(p_grp, pos, -(-S // SLC_BLOCK))
    o_s = selected_prompt(q, kv_s, idx, pos)
    o_w = window_prompt(q, kv_w)
    pool_out = pool_mixer(u, pos, S, w_pool, pool_scale)
    out = merge_heads(o_c, o_s, o_w, gates, pool_out, w_out)
    new = (kv_c, kv_s, kv_w[:, S - min(WINDOW, S):], u[:, S - POOL_STATE:])
    return out, new


def sample_mixer(h, pos, cache_kv_cmp, cache_kv_slc, page_table, st_win, st_pool,
                 w_in, cmpw, w_pool, pool_scale, w_out):
    DB, DS = h.shape[:2]
    past_len = page_table.shape[1] * PAGE_SIZE
    q, kv_c, kv_s, kv_w, gates, u = in_project(h, w_in, pos)
    past_c = cache_kv_cmp[page_table].reshape(DB, past_len, 2, N_KV_HEADS, HEAD_DIM)
    kc, vc = compress_kv([past_c, pad_rows(kv_c, CMP_STRIDE)], cmpw)
    o_c, p_grp = compressed_attention(q, kc, vc, pos)
    idx = select_blocks(p_grp, pos, -(-(past_len + DS) // SLC_BLOCK))
    o_s = selected_sample(q, cache_kv_slc, kv_s, page_table, idx, pos)
    wb = st_win.shape[1]
    win_ext = jnp.concatenate([st_win, kv_w], axis=1)
    o_w = window_core(q, win_ext, pos, past_len - wb + jnp.arange(wb + DS))
    ps = st_pool.shape[1]
    pool_ext = jnp.concatenate([st_pool, u], axis=1)
    pool_out = pool_mixer(pool_ext, past_len - ps + jnp.arange(ps + DS), DS, w_pool, pool_scale)
    out = merge_heads(o_c, o_s, o_w, gates, pool_out, w_out)
    new = (kv_c, kv_s, win_ext[:, DS:], pool_ext[:, DS:])
    return out, new


def setup_inputs(seed: int = 0) -> dict:
    key = jax.random.key(seed)
    ks = jax.random.split(key, 32)
    n_pages = PAST_LEN // PAGE_SIZE
    n_used = DEC_BATCH * n_pages
    n_phys = n_used + (n_used + 3) // 4

    def nrm(k, shape, scale):
        return jax.random.normal(k, shape, jnp.float32) * scale

    def gain(k, shape):
        return 1.0 + 0.05 * jax.random.normal(k, shape, jnp.float32)

    kv_shape = (DEPTH, n_phys, PAGE_SIZE, 2, N_KV_HEADS, HEAD_DIM)
    page_table = jax.random.permutation(ks[4], n_phys)[:n_used].reshape(DEC_BATCH, n_pages).astype(jnp.int32)
    return {
        'x_prompt': nrm(ks[0], (BATCH, SEQ, D_MODEL), 1.0),
        'x_sample': nrm(ks[1], (DEC_BATCH, DEC_SEQ, D_MODEL), 1.0),
        'cache_kv_cmp': nrm(ks[2], kv_shape, 1.0),
        'cache_kv_slc': nrm(ks[3], kv_shape, 1.0),
        'page_table': page_table,
        'state_kv_win': nrm(ks[5], (DEPTH, DEC_BATCH, min(WINDOW, PAST_LEN), 2, N_KV_HEADS, HEAD_DIM), 1.0),
        'state_pool': nrm(ks[6], (DEPTH, DEC_BATCH, POOL_STATE, POOL_WIDTH), 1.0),
        'n_ffn1': gain(ks[7], (DEPTH, D_MODEL)),
        'w_ffn1_gate': nrm(ks[8], (DEPTH, D_MODEL, D_FF), D_MODEL ** -0.5),
        'w_ffn1_up': nrm(ks[9], (DEPTH, D_MODEL, D_FF), D_MODEL ** -0.5),
        'w_ffn1_down': nrm(ks[10], (DEPTH, D_FF, D_MODEL), D_FF ** -0.5),
        'n_mix': gain(ks[11], (DEPTH, D_MODEL)),
        'w_in': nrm(ks[12], (DEPTH, D_MODEL, IN_WIDTH), D_MODEL ** -0.5),
        'w_cmp_k1': nrm(ks[13], (DEPTH, CMP_BLOCK * HEAD_DIM, CMP_HIDDEN), (CMP_BLOCK * HEAD_DIM) ** -0.5),
        'w_cmp_k2': nrm(ks[14], (DEPTH, CMP_HIDDEN, HEAD_DIM), CMP_HIDDEN ** -0.5),
        'pe_cmp_k': nrm(ks[15], (DEPTH, CMP_BLOCK, HEAD_DIM), 0.1),
        'w_cmp_v1': nrm(ks[16], (DEPTH, CMP_BLOCK * HEAD_DIM, CMP_HIDDEN), (CMP_BLOCK * HEAD_DIM) ** -0.5),
        'w_cmp_v2': nrm(ks[17], (DEPTH, CMP_HIDDEN, HEAD_DIM), CMP_HIDDEN ** -0.5),
        'pe_cmp_v': nrm(ks[18], (DEPTH, CMP_BLOCK, HEAD_DIM), 0.1),
        'w_pool': nrm(ks[19], (DEPTH, len(POOL_WINDOWS), POOL_GROUP_WIDTH, POOL_GROUP_WIDTH), POOL_GROUP_WIDTH ** -0.5),
        'pool_scale': gain(ks[20], (DEPTH, POOL_WIDTH)),
        'w_out': nrm(ks[21], (DEPTH, MIX_WIDTH, D_MODEL), MIX_WIDTH ** -0.5),
        'n_ffn2': gain(ks[22], (DEPTH, D_MODEL)),
        'w_ffn2_gate': nrm(ks[23], (DEPTH, D_MODEL, D_FF), D_MODEL ** -0.5),
        'w_ffn2_up': nrm(ks[24], (DEPTH, D_MODEL, D_FF), D_MODEL ** -0.5),
        'w_ffn2_down': nrm(ks[25], (DEPTH, D_FF, D_MODEL), D_FF ** -0.5),
        'n_final': gain(ks[26], (D_MODEL,)),
    }


def reference(x_prompt, x_sample, cache_kv_cmp, cache_kv_slc, page_table, state_kv_win, state_pool,
              n_ffn1, w_ffn1_gate, w_ffn1_up, w_ffn1_down, n_mix, w_in,
              w_cmp_k1, w_cmp_k2, pe_cmp_k, w_cmp_v1, w_cmp_v2, pe_cmp_v,
              w_pool, pool_scale, w_out, n_ffn2, w_ffn2_gate, w_ffn2_up, w_ffn2_down, n_final):
    seq = x_prompt.shape[1]
    dec_seq = x_sample.shape[1]
    past_len = page_table.shape[1] * PAGE_SIZE
    pos_p = jnp.arange(seq, dtype=jnp.int32)
    pos_s = past_len + jnp.arange(dec_seq, dtype=jnp.int32)
    xp, xs = x_prompt, x_sample
    st_p = ([], [], [], [])
    st_s = ([], [], [], [])
    for l in range(DEPTH):
        ffn1 = (n_ffn1[l], w_ffn1_gate[l], w_ffn1_up[l], w_ffn1_down[l])
        ffn2 = (n_ffn2[l], w_ffn2_gate[l], w_ffn2_up[l], w_ffn2_down[l])
        cmpw = (w_cmp_k1[l], w_cmp_k2[l], pe_cmp_k[l], w_cmp_v1[l], w_cmp_v2[l], pe_cmp_v[l])
        xp = xp + 0.5 * swiglu_ffn(xp, *ffn1)
        xs = xs + 0.5 * swiglu_ffn(xs, *ffn1)
        mix_p, new_p = prompt_mixer(rmsnorm(xp, n_mix[l]), pos_p, w_in[l], cmpw,
                                    w_pool[l], pool_scale[l], w_out[l])
        mix_s, new_s = sample_mixer(rmsnorm(xs, n_mix[l]), pos_s, cache_kv_cmp[l], cache_kv_slc[l],
                                    page_table, state_kv_win[l], state_pool[l], w_in[l], cmpw,
                                    w_pool[l], pool_scale[l], w_out[l])
        xp = xp + mix_p
        xs = xs + mix_s
        xp = xp + 0.5 * swiglu_ffn(xp, *ffn2)
        xs = xs + 0.5 * swiglu_ffn(xs, *ffn2)
        for lst, a in zip(st_p, new_p):
            lst.append(a)
        for lst, a in zip(st_s, new_s):
            lst.append(a)
    y_prompt = rmsnorm(xp, n_final)
    y_sample = rmsnorm(xs, n_final)
    new_kv_cmp_prompt = jnp.stack(st_p[0])
    new_kv_slc_prompt = jnp.stack(st_p[1])
    new_kv_win_prompt = jnp.stack(st_p[2])
    new_pool_prompt = jnp.stack(st_p[3])
    new_kv_cmp_sample = jnp.stack(st_s[0])
    new_kv_slc_sample = jnp.stack(st_s[1])
    new_kv_win_sample = jnp.stack(st_s[2])
    new_pool_sample = jnp.stack(st_s[3])
    return (y_prompt, y_sample, new_kv_cmp_prompt, new_kv_slc_prompt, new_kv_win_prompt, new_pool_prompt,
            new_kv_cmp_sample, new_kv_slc_sample, new_kv_win_sample, new_pool_sample)
```

```python
import functools

import numpy as np
import jax
import jax.numpy as jnp
from jax import lax
from jax.experimental import pallas as pl
from jax.experimental.pallas import tpu as pltpu

F32 = jnp.float32
BF16 = jnp.bfloat16

HEAD_DIM = 64
N_HEADS = 8
N_KV_HEADS = 2
GROUP = N_HEADS // N_KV_HEADS
NSA_WIDTH = N_HEADS * HEAD_DIM
KV_WIDTH = 2 * N_KV_HEADS * HEAD_DIM
ROPE_DIM = HEAD_DIM // 4
ROPE_THETA = 500000.0
CMP_BLOCK = 32
CMP_STRIDE = 16
CMP_HIDDEN = 256
SLC_BLOCK = 64
TOPK_BLOCKS = 16
WINDOW = 512
PAGE_SIZE = 128
POOL_WINDOWS = (2, 4, 8, 16)
POOL_GROUP_WIDTH = 128
POOL_WIDTH = POOL_GROUP_WIDTH * len(POOL_WINDOWS)
POOL_STATE = max(POOL_WINDOWS) - 1
POOL_HALO = 16
RMS_EPS = 1e-6
FORCE_SCORE = 1e4
NEG_INF = -1e30
ATTN_SCALE = HEAD_DIM ** -0.5
N_GATES = 3 * N_HEADS
CHUNK_WIDTH = CMP_STRIDE * KV_WIDTH

LANES = 128
VMEM_LIMIT_BYTES = 56 * 1024 * 1024

_NT = (((1,), (1,)), ((), ()))


def _params(*sem):
    return pltpu.CompilerParams(dimension_semantics=sem, vmem_limit_bytes=VMEM_LIMIT_BYTES)


def _rmsnorm(x, g):
    return x * lax.rsqrt(jnp.mean(x * x, axis=-1, keepdims=True) + RMS_EPS) * g


def _const_spec(shape):
    n = len(shape)
    return pl.BlockSpec(shape, lambda *_: (0,) * n)


FFN_TM = 512
FFN_CHUNK = 256


def _ffn_kernel(x_ref, g_ref, wg_ref, wu_ref, wd_ref, gf_ref, o_ref, act_ref, *, final_norm):
    x = x_ref[...]
    h = _rmsnorm(x, g_ref[...]).astype(BF16)
    d_ff = wg_ref.shape[1]
    for c in range(d_ff // FFN_CHUNK):
        sl = slice(c * FFN_CHUNK, (c + 1) * FFN_CHUNK)
        a = jnp.dot(h, wg_ref[:, sl], preferred_element_type=F32)
        u = jnp.dot(h, wu_ref[:, sl], preferred_element_type=F32)
        act_ref[:, sl] = (a * jax.nn.sigmoid(a) * u).astype(BF16)
    y = x + 0.5 * jnp.dot(act_ref[...], wd_ref[...], preferred_element_type=F32)
    if final_norm:
        y = _rmsnorm(y, gf_ref[...])
    o_ref[...] = y


def _ffn(x, g, wg, wu, wd, g_final=None):
    m, d = x.shape
    d_ff = wg.shape[1]
    tm = min(FFN_TM, m)
    final_norm = g_final is not None
    gf = g_final if final_norm else g
    return pl.pallas_call(
        functools.partial(_ffn_kernel, final_norm=final_norm),
        grid=(m // tm,),
        in_specs=[
            pl.BlockSpec((tm, d), lambda i: (i, 0)),
            _const_spec((1, d)),
            pl.BlockSpec((d, d_ff), lambda i: (0, 0), pipeline_mode=pl.Buffered(1)),
            pl.BlockSpec((d, d_ff), lambda i: (0, 0), pipeline_mode=pl.Buffered(1)),
            pl.BlockSpec((d_ff, d), lambda i: (0, 0), pipeline_mode=pl.Buffered(1)),
            _const_spec((1, d)),
        ],
        out_specs=pl.BlockSpec((tm, d), lambda i: (i, 0)),
        out_shape=jax.ShapeDtypeStruct((m, d), F32),
        scratch_shapes=[pltpu.VMEM((tm, d_ff), BF16)],
        compiler_params=_params("arbitrary"),
        name="ffn",
    )(x, g.reshape(1, d), wg, wu, wd, gf.reshape(1, d))


INPROJ_TM = 512


def _rope_tables(pos):
    half = ROPE_DIM // 2
    inv = jnp.power(ROPE_THETA, -jnp.arange(half, dtype=F32) / half)
    ang = pos.astype(F32)[:, None] * inv[None, :]
    cos, sin = jnp.cos(ang), jnp.sin(ang)
    t = pos.shape[0]
    rest = HEAD_DIM - ROPE_DIM
    z_half = jnp.zeros((t, half), F32)
    z_rest = jnp.zeros((t, rest), F32)
    cos_t = jnp.concatenate([cos, cos, jnp.ones((t, rest), F32)], axis=1)
    sin_a = jnp.concatenate([-sin, z_half, z_rest], axis=1)
    sin_b = jnp.concatenate([z_half, sin, z_rest], axis=1)
    rep = LANES // HEAD_DIM
    return tuple(jnp.tile(a, (1, rep)) for a in (cos_t, sin_a, sin_b))


def _rope(z, cos_t, sin_a, sin_b):
    half = ROPE_DIM // 2
    outs = []
    for c in range(z.shape[1] // LANES):
        zc = z[:, c * LANES:(c + 1) * LANES]
        outs.append(zc * cos_t + pltpu.roll(zc, LANES - half, axis=1) * sin_a
                    + pltpu.roll(zc, half, axis=1) * sin_b)
    return outs[0] if len(outs) == 1 else jnp.concatenate(outs, axis=1)


def _inproj_kernel(x_ref, g_ref, wq_ref, wkv_ref, wgt_ref, wu_ref, cos_ref, sa_ref, sb_ref,
                   q_ref, kvc_ref, kvs_ref, kvw_ref, kvs_hm_ref, kvw_hm_ref, gate_ref, u_ref):
    h = _rmsnorm(x_ref[...], g_ref[...]).astype(BF16)
    cos_t, sin_a, sin_b = cos_ref[...], sa_ref[...], sb_ref[...]
    q = _rope(jnp.dot(h, wq_ref[...], preferred_element_type=F32), cos_t, sin_a, sin_b) * ATTN_SCALE
    for hd in range(N_HEADS):
        q_ref[hd] = q[:, hd * HEAD_DIM:(hd + 1) * HEAD_DIM].astype(q_ref.dtype)
    kv = jnp.dot(h, wkv_ref[...], preferred_element_type=F32)
    k_width = N_KV_HEADS * HEAD_DIM
    for i, (full_ref, hm_ref) in enumerate(((kvc_ref, None), (kvs_ref, kvs_hm_ref), (kvw_ref, kvw_hm_ref))):
        blk = kv[:, i * KV_WIDTH:(i + 1) * KV_WIDTH]
        full = jnp.concatenate([_rope(blk[:, :k_width], cos_t, sin_a, sin_b), blk[:, k_width:]], axis=1)
        full_ref[...] = full
        if hm_ref is not None:
            for j in range(2 * N_KV_HEADS):
                hm_ref[j] = full[:, j * HEAD_DIM:(j + 1) * HEAD_DIM].astype(BF16)
    gate_ref[...] = jax.nn.sigmoid(jnp.dot(h, wgt_ref[...], preferred_element_type=F32))
    u_ref[...] = jnp.dot(h, wu_ref[...], preferred_element_type=F32)


def _inproj(x, g, wq, wkv, wgt, wu, tables, q_dtype):
    m, d = x.shape
    seq = tables[0].shape[0]
    tm = min(INPROJ_TM, seq)
    n_tab = seq // tm
    row = lambda i: (i, 0)
    tab = lambda i: (i % n_tab, 0)
    hm = lambda i: (i // n_tab, 0, i % n_tab, 0)
    return pl.pallas_call(
        _inproj_kernel,
        grid=(m // tm,),
        in_specs=[
            pl.BlockSpec((tm, d), row), _const_spec((1, d)),
            _const_spec(wq.shape), _const_spec(wkv.shape), _const_spec(wgt.shape), _const_spec(wu.shape),
            pl.BlockSpec((tm, LANES), tab), pl.BlockSpec((tm, LANES), tab), pl.BlockSpec((tm, LANES), tab),
        ],
        out_specs=[
            pl.BlockSpec((None, N_HEADS, tm, HEAD_DIM), hm),
            pl.BlockSpec((tm, KV_WIDTH), row), pl.BlockSpec((tm, KV_WIDTH), row), pl.BlockSpec((tm, KV_WIDTH), row),
            pl.BlockSpec((None, 2 * N_KV_HEADS, tm, HEAD_DIM), hm),
            pl.BlockSpec((None, 2 * N_KV_HEADS, tm, HEAD_DIM), hm),
            pl.BlockSpec((tm, LANES), row), pl.BlockSpec((tm, POOL_WIDTH), row),
        ],
        out_shape=[
            jax.ShapeDtypeStruct((m // seq, N_HEADS, seq, HEAD_DIM), q_dtype),
            jax.ShapeDtypeStruct((m, KV_WIDTH), F32), jax.ShapeDtypeStruct((m, KV_WIDTH), F32),
            jax.ShapeDtypeStruct((m, KV_WIDTH), F32),
            jax.ShapeDtypeStruct((m // seq, 2 * N_KV_HEADS, seq, HEAD_DIM), BF16),
            jax.ShapeDtypeStruct((m // seq, 2 * N_KV_HEADS, seq, HEAD_DIM), BF16),
            jax.ShapeDtypeStruct((m, LANES), F32), jax.ShapeDtypeStruct((m, POOL_WIDTH), F32),
        ],
        compiler_params=_params("arbitrary"),
        name="inproj",
    )(x, g.reshape(1, d), wq, wkv, wgt, wu, *tables)


CMP_ROWS = 128


def _gelu_tanh(x):
    return 0.5 * x * (1.0 + jnp.tanh(np.sqrt(2.0 / np.pi).astype(np.float32) * (x + 0.044715 * (x * x * x))))


def _split_heads(x, kv):
    rows = x.shape[0]
    low = lax.broadcasted_iota(jnp.int32, (rows, LANES), 1) < HEAD_DIM
    g0, g1 = [], []
    for j in range(CMP_STRIDE // 2):
        c0 = (2 * j) * KV_WIDTH + kv * LANES
        c1 = (2 * j + 1) * KV_WIDTH + kv * LANES
        a0, a1 = x[:, c0:c0 + LANES], x[:, c1:c1 + LANES]
        g0.append(jnp.where(low, a0, pltpu.roll(a1, HEAD_DIM, axis=1)))
        g1.append(jnp.where(low, pltpu.roll(a0, HEAD_DIM, axis=1), a1))
    return jnp.concatenate(g0, axis=1), jnp.concatenate(g1, axis=1)


def _compress_compute(load_rows, n, xnew, pe_ref, w1_refs, w2_refs, out_refs, y_ref, has_new):
    row8 = lax.broadcasted_iota(jnp.int32, (8, 1), 0)

    def pick(blk, r):
        return jnp.sum(jnp.where(row8 == r, blk, 0.0), axis=0, keepdims=True)

    for kv in range(2):
        def fill(r, carry):
            r0 = pl.multiple_of(r * CMP_ROWS, CMP_ROWS)
            y0, y1 = _split_heads(load_rows(r0, CMP_ROWS), kv)
            y_ref[pl.ds(r0, CMP_ROWS), :] = y0.astype(BF16)
            y_ref[pl.ds(n + r0, CMP_ROWS), :] = y1.astype(BF16)
            return carry
        lax.fori_loop(0, n // CMP_ROWS, fill, 0)
        if has_new:
            n0, n1 = _split_heads(xnew, kv)
            new8 = jnp.where(row8 == 0, n0, jnp.where(row8 == 1, n1, 0.0))
        else:
            new8 = jnp.zeros((8, CMP_STRIDE * HEAD_DIM), F32)
        y_ref[pl.ds(2 * n, 16), :] = jnp.concatenate([new8, pe_ref[kv]], axis=0).astype(BF16)
        p = jnp.dot(y_ref[...], w1_refs[kv][...], preferred_element_type=F32)
        p_new, p_pe = p[2 * n:2 * n + 8], p[2 * n + 8:2 * n + 16]
        bias = pick(p_pe[:, :CMP_HIDDEN], 0) + pick(p_pe[:, CMP_HIDDEN:], 1)
        last = lax.broadcasted_iota(jnp.int32, (n, CMP_HIDDEN), 0) == n - 1
        for g in range(N_KV_HEADS):
            slot0 = p[g * n:(g + 1) * n, :CMP_HIDDEN]
            slot1 = pltpu.roll(p[g * n:(g + 1) * n, CMP_HIDDEN:], n - 1, axis=0)
            if has_new:
                slot1 = jnp.where(last, pick(p_new[:, CMP_HIDDEN:], g), slot1)
            hid = _gelu_tanh(slot0 + slot1 + bias).astype(BF16)
            out_refs[kv][g] = jnp.dot(hid, w2_refs[kv][...], preferred_element_type=F32).astype(BF16)


def _compress_prompt_kernel(x_ref, pe_ref, wk1_ref, wv1_ref, wk2_ref, wv2_ref, kc_ref, vc_ref, y_ref):
    n = x_ref.shape[0]
    _compress_compute(lambda r0, rows: x_ref[pl.ds(r0, rows), :], n, None, pe_ref,
                      (wk1_ref, wv1_ref), (wk2_ref, wv2_ref), (kc_ref, vc_ref), y_ref, False)


def _compress_prompt(x, pe, wk1, wv1, wk2, wv2):
    b, n, cw = x.shape
    out = jax.ShapeDtypeStruct((b, N_KV_HEADS, n, HEAD_DIM), BF16)
    out_spec = pl.BlockSpec((None, N_KV_HEADS, n, HEAD_DIM), lambda i: (i, 0, 0, 0))
    return pl.pallas_call(
        _compress_prompt_kernel,
        grid=(b,),
        in_specs=[pl.BlockSpec((None, n, cw), lambda i: (i, 0, 0)), _const_spec(pe.shape),
                  _const_spec(wk1.shape), _const_spec(wv1.shape), _const_spec(wk2.shape), _const_spec(wv2.shape)],
        out_specs=[out_spec, out_spec],
        out_shape=[out, out],
        scratch_shapes=[pltpu.VMEM((2 * n + 16, CMP_STRIDE * HEAD_DIM), BF16)],
        compiler_params=_params("arbitrary"),
        name="compress_prompt",
    )(x, pe, wk1, wv1, wk2, wv2)


def _page_copy(cache_hbm, buf, sem, page, slot, p, rows):
    return pltpu.make_async_copy(cache_hbm.at[page], buf.at[slot, pl.ds(p * rows, rows)], sem.at[slot])


def _fetch_pages(pt_ref, cache_hbm, buf, sem, seq, slot, n_pages, rows):
    def body(p, carry):
        _page_copy(cache_hbm, buf, sem, pt_ref[seq, p], slot, p, rows).start()
        return carry
    lax.fori_loop(0, n_pages, body, 0)


def _wait_pages(cache_hbm, buf, sem, slot, n_pages, rows):
    def body(p, carry):
        _page_copy(cache_hbm, buf, sem, 0, slot, p, rows).wait()
        return carry
    lax.fori_loop(0, n_pages, body, 0)


def _paged_prologue(pt_ref, cache_hbm, buf, sem, n_pages, rows):
    b = pl.program_id(0)
    slot = lax.rem(b, 2)

    @pl.when(b == 0)
    def _():
        _fetch_pages(pt_ref, cache_hbm, buf, sem, 0, 0, n_pages, rows)

    @pl.when(b + 1 < pl.num_programs(0))
    def _():
        _fetch_pages(pt_ref, cache_hbm, buf, sem, b + 1, 1 - slot, n_pages, rows)

    _wait_pages(cache_hbm, buf, sem, slot, n_pages, rows)
    return slot


def _compress_sample_kernel(pt_ref, cache_hbm, xnew_ref, pe_ref, wk1_ref, wv1_ref, wk2_ref, wv2_ref,
                            kc_ref, vc_ref, buf, sem, y_ref):
    n_pages = pt_ref.shape[1]
    rows = PAGE_SIZE // CMP_STRIDE
    n = n_pages * rows
    slot = _paged_prologue(pt_ref, cache_hbm, buf, sem, n_pages, rows)
    xnew = jnp.broadcast_to(xnew_ref[...], (8, CHUNK_WIDTH))
    _compress_compute(lambda r0, nr: buf[slot, pl.ds(r0, nr), :], n, xnew, pe_ref,
                      (wk1_ref, wv1_ref), (wk2_ref, wv2_ref), (kc_ref, vc_ref), y_ref, True)


def _compress_sample(page_table, cache, xnew, pe, wk1, wv1, wk2, wv2):
    db, n_pages = page_table.shape
    rows = PAGE_SIZE // CMP_STRIDE
    n = n_pages * rows
    out = jax.ShapeDtypeStruct((db, N_KV_HEADS, n, HEAD_DIM), BF16)
    out_spec = pl.BlockSpec((None, N_KV_HEADS, n, HEAD_DIM), lambda i, pt: (i, 0, 0, 0))
    const = lambda s: pl.BlockSpec(s, lambda i, pt: (0,) * len(s))
    return pl.pallas_call(
        _compress_sample_kernel,
        grid_spec=pltpu.PrefetchScalarGridSpec(
            num_scalar_prefetch=1,
            grid=(db,),
            in_specs=[pl.BlockSpec(memory_space=pl.ANY),
                      pl.BlockSpec((None, 1, CHUNK_WIDTH), lambda i, pt: (i, 0, 0)),
                      const(pe.shape), const(wk1.shape), const(wv1.shape), const(wk2.shape), const(wv2.shape)],
            out_specs=[out_spec, out_spec],
            scratch_shapes=[pltpu.VMEM((2, n, CHUNK_WIDTH), F32), pltpu.SemaphoreType.DMA((2,)),
                            pltpu.VMEM((2 * n + 16, CMP_STRIDE * HEAD_DIM), BF16)],
        ),
        out_shape=[out, out],
        compiler_params=_params("arbitrary"),
        name="compress_sample",
    )(page_table, cache, xnew, pe, wk1, wv1, wk2, wv2)


def _cmp_attn_kernel(q_ref, kc_ref, vc_ref, gate_ref, agg_ref, oc_ref, sel_ref, *, pos0, tiled, n_slc):
    tq = q_ref.shape[1]
    n = kc_ref.shape[1]
    nsp = agg_ref.shape[1]
    t0 = pos0 + (pl.program_id(1) * tq if tiled else 0)
    t_n = t0 + lax.broadcasted_iota(jnp.int32, (tq, n), 0)
    blk_end = lax.broadcasted_iota(jnp.int32, (tq, n), 1) * CMP_STRIDE + (CMP_BLOCK - 1)
    valid = (blk_end <= t_n)[None]
    t_s = t0 + lax.broadcasted_iota(jnp.int32, (tq, nsp), 0)
    j = lax.broadcasted_iota(jnp.int32, (tq, nsp), 1)
    cur = t_s // SLC_BLOCK
    forced = (j == 0) | (j == cur) | (j == cur - 1)
    real = j < n_slc
    gates = gate_ref[...]
    for g in range(N_KV_HEADS):
        qg = jnp.concatenate([q_ref[GROUP * g + r].astype(F32) for r in range(GROUP)], axis=0).astype(BF16)
        s = lax.dot_general(qg, kc_ref[g], _NT, preferred_element_type=F32).reshape(GROUP, tq, n)
        s = jnp.where(valid, s, NEG_INF)
        e = jnp.where(valid, jnp.exp(s - jnp.max(s, axis=-1, keepdims=True)), 0.0)
        p = e / jnp.maximum(jnp.sum(e, axis=-1, keepdims=True), 1e-30)
        o = jnp.dot(p.reshape(GROUP * tq, n).astype(BF16), vc_ref[g], preferred_element_type=F32)
        p_grp = jnp.sum(p, axis=0)
        p_hi = p_grp.astype(BF16)
        p_lo = (p_grp - p_hi.astype(F32)).astype(BF16)
        p_slc = (jnp.dot(p_hi, agg_ref[...], preferred_element_type=F32)
                 + jnp.dot(p_lo, agg_ref[...], preferred_element_type=F32))
        score = jnp.where(forced, FORCE_SCORE, jnp.where(j <= cur, p_slc, NEG_INF))
        score = jnp.where(real, score, -jnp.inf)
        rank = jnp.zeros((tq, nsp), jnp.int32)
        for jp in range(n_slc):
            c = score[:, jp:jp + 1]
            rank = rank + ((c > score) | ((c == score) & (j > jp))).astype(jnp.int32)
        sel = (rank < TOPK_BLOCKS) & real
        sel_ref[:, g * nsp:(g + 1) * nsp] = jnp.where(sel, 0.0, NEG_INF).astype(sel_ref.dtype)
        for r in range(GROUP):
            hd = GROUP * g + r
            oc_ref[:, hd * HEAD_DIM:(hd + 1) * HEAD_DIM] = gates[:, 3 * hd:3 * hd + 1] * o[r * tq:(r + 1) * tq]


def _agg_matrix(n_cmp_pad, n_cmp, n_slc, n_slc_pad):
    c0 = np.arange(n_cmp)[:, None] * CMP_STRIDE
    s0 = np.arange(n_slc)[None, :] * SLC_BLOCK
    overlap = np.clip(np.minimum(c0 + CMP_BLOCK, s0 + SLC_BLOCK) - np.maximum(c0, s0), 0, None)
    agg = np.zeros((n_cmp_pad, n_slc_pad), np.float32)
    agg[:n_cmp, :n_slc] = overlap / CMP_BLOCK
    return jnp.asarray(agg, BF16)


def _cmp_attn(q_hm, kc, vc, gates, agg, *, tq, pos0, tiled, n_slc, sel_dtype):
    b, _, t, _ = q_hm.shape
    n = kc.shape[2]
    nsp = agg.shape[1]
    return pl.pallas_call(
        functools.partial(_cmp_attn_kernel, pos0=pos0, tiled=tiled, n_slc=n_slc),
        grid=(b, t // tq),
        in_specs=[
            pl.BlockSpec((None, N_HEADS, tq, HEAD_DIM), lambda i, k: (i, 0, k, 0)),
            pl.BlockSpec((None, N_KV_HEADS, n, HEAD_DIM), lambda i, k: (i, 0, 0, 0)),
            pl.BlockSpec((None, N_KV_HEADS, n, HEAD_DIM), lambda i, k: (i, 0, 0, 0)),
            pl.BlockSpec((None, tq, LANES), lambda i, k: (i, k, 0)),
            _const_spec(agg.shape),
        ],
        out_specs=[pl.BlockSpec((None, tq, NSA_WIDTH), lambda i, k: (i, k, 0)),
                   pl.BlockSpec((None, tq, N_KV_HEADS * nsp), lambda i, k: (i, k, 0))],
        out_shape=[jax.ShapeDtypeStruct((b, t, NSA_WIDTH), F32),
                   jax.ShapeDtypeStruct((b, t, N_KV_HEADS * nsp), sel_dtype)],
        compiler_params=_params("arbitrary", "arbitrary"),
        name="cmp_attn",
    )(q_hm, kc, vc, gates, agg)


ATTN_TQ = 256


def _expand_matrix(n_keys, n_blk_pad, tk):
    e = np.zeros((n_keys // tk, n_blk_pad, tk), np.float32)
    key = np.arange(n_keys)
    e[key // tk, key // SLC_BLOCK, key % tk] = 1.0
    return jnp.asarray(e, BF16)


def _softmax_step(qg_ref, k, v, bias, m_ref, l_ref, acc_ref, tq):
    s = lax.dot_general(qg_ref[...], k, _NT, preferred_element_type=F32)
    tk = s.shape[1]
    if bias is not None:
        s = (s.reshape(GROUP, tq, tk) + bias[None]).reshape(GROUP * tq, tk)
    m_prev = m_ref[...]
    m_new = jnp.maximum(m_prev, jnp.max(s, axis=-1, keepdims=True))
    alpha = jnp.exp(m_prev - m_new)
    p = jnp.exp(s - jnp.tile(m_new, (1, tk // LANES)))
    l_ref[...] = alpha * l_ref[...] + jnp.sum(p, axis=-1, keepdims=True)
    acc_ref[...] = alpha[:, :HEAD_DIM] * acc_ref[...] + jnp.dot(p.astype(BF16), v, preferred_element_type=F32)
    m_ref[...] = m_new


def _prompt_attn_kernel(q_ref, kvs_ref, kvw_ref, sel_ref, e_ref, oc_ref, gate_ref, o_ref,
                        qg_ref, m_ref, l_ref, acc_ref):
    tq = q_ref.shape[1]
    tk = tq
    nsp = sel_ref.shape[1] // N_KV_HEADS
    qt = pl.program_id(1)
    n_win = WINDOW // tk
    row = lax.broadcasted_iota(jnp.int32, (tq, tk), 0)
    col = lax.broadcasted_iota(jnp.int32, (tq, tk), 1)
    causal = jnp.where(col <= row, 0.0, NEG_INF)
    win_lo = jnp.where(col >= row, 0.0, NEG_INF)
    gates = gate_ref[...]
    oc = oc_ref[...]

    def reset():
        m_ref[...] = jnp.full(m_ref.shape, NEG_INF, F32)
        l_ref[...] = jnp.zeros(l_ref.shape, F32)
        acc_ref[...] = jnp.zeros(acc_ref.shape, F32)

    def tile(ref, g, kt):
        r0 = pl.multiple_of(kt * tk, tk)
        return ref[g, pl.ds(r0, tk), :], ref[N_KV_HEADS + g, pl.ds(r0, tk), :]

    for g in range(N_KV_HEADS):
        for r in range(GROUP):
            qg_ref[r * tq:(r + 1) * tq, :] = q_ref[GROUP * g + r]
        sel_g = sel_ref[:, g * nsp:(g + 1) * nsp]

        reset()

        def slc_step(kt, extra):
            k, v = tile(kvs_ref, g, kt)
            bias = jnp.dot(sel_g, e_ref[kt], preferred_element_type=F32)
            if extra is not None:
                bias = bias + extra
            _softmax_step(qg_ref, k, v, bias, m_ref, l_ref, acc_ref, tq)

        def slc_body(kt, carry):
            slc_step(kt, None)
            return carry
        lax.fori_loop(0, qt, slc_body, 0)
        slc_step(qt, causal)
        o_s = acc_ref[...] / l_ref[...][:, :HEAD_DIM]

        reset()

        def win_step(kt, bias):
            k, v = tile(kvw_ref, g, kt)
            _softmax_step(qg_ref, k, v, bias, m_ref, l_ref, acc_ref, tq)

        @pl.when(qt >= n_win)
        def _():
            win_step(qt - n_win, win_lo)

        def win_body(kt, carry):
            win_step(kt, None)
            return carry
        lax.fori_loop(jnp.maximum(qt - n_win + 1, 0), qt, win_body, 0)
        win_step(qt, causal)
        o_w = acc_ref[...] / l_ref[...][:, :HEAD_DIM]

        for r in range(GROUP):
            hd = GROUP * g + r
            cols = slice(hd * HEAD_DIM, (hd + 1) * HEAD_DIM)
            rows = slice(r * tq, (r + 1) * tq)
            o_ref[:, cols] = (oc[:, cols] + gates[:, 3 * hd + 1:3 * hd + 2] * o_s[rows]
                              + gates[:, 3 * hd + 2:3 * hd + 3] * o_w[rows]).astype(o_ref.dtype)


def _prompt_attn(q_hm, kvs_hm, kvw_hm, sel, expand, oc, gates):
    b, _, t, _ = q_hm.shape
    tq = ATTN_TQ
    nkv = 2 * N_KV_HEADS
    tile3 = lambda i, k: (i, k, 0)
    return pl.pallas_call(
        _prompt_attn_kernel,
        grid=(b, t // tq),
        in_specs=[
            pl.BlockSpec((None, N_HEADS, tq, HEAD_DIM), lambda i, k: (i, 0, k, 0)),
            pl.BlockSpec((None, nkv, t, HEAD_DIM), lambda i, k: (i, 0, 0, 0)),
            pl.BlockSpec((None, nkv, t, HEAD_DIM), lambda i, k: (i, 0, 0, 0)),
            pl.BlockSpec((None, tq, sel.shape[2]), tile3),
            _const_spec(expand.shape),
            pl.BlockSpec((None, tq, NSA_WIDTH), tile3),
            pl.BlockSpec((None, tq, LANES), tile3),
        ],
        out_specs=pl.BlockSpec((None, tq, NSA_WIDTH), tile3),
        out_shape=jax.ShapeDtypeStruct((b, t, NSA_WIDTH), BF16),
        scratch_shapes=[pltpu.VMEM((GROUP * tq, HEAD_DIM), BF16), pltpu.VMEM((GROUP * tq, LANES), F32),
                        pltpu.VMEM((GROUP * tq, LANES), F32), pltpu.VMEM((GROUP * tq, HEAD_DIM), F32)],
        compiler_params=_params("arbitrary", "arbitrary"),
        name="prompt_attn",
    )(q_hm, kvs_hm, kvw_hm, sel, expand, oc, gates)


Q_PAD = 8
NEW_PAD = 16


def _block_diag_q(q_s):
    db = q_s.shape[0]
    qg = q_s.reshape(db, N_KV_HEADS, GROUP * Q_PAD, HEAD_DIM)
    zero = jnp.zeros_like(qg[:, 0])
    top = jnp.concatenate([qg[:, 0], zero], axis=-1)
    bot = jnp.concatenate([zero, qg[:, 1]], axis=-1)
    return jnp.concatenate([top, bot], axis=1)


def _two_piece_attention(qbd, k_old, v_old, bias_old, k_new, v_new, bias_new):
    def scores(k, bias):
        s = lax.dot_general(qbd, k, _NT, preferred_element_type=F32)
        nk = s.shape[1]
        return (s.reshape(N_KV_HEADS, GROUP, Q_PAD, nk) + bias[:, None]).reshape(N_HEADS * Q_PAD, nk)
    s_old = scores(k_old, bias_old)
    s_new = scores(k_new, bias_new)
    m = jnp.maximum(jnp.max(s_old, axis=-1, keepdims=True), jnp.max(s_new, axis=-1, keepdims=True))
    p_old = jnp.exp(s_old - m)
    p_new = jnp.exp(s_new - m)
    den = jnp.sum(p_old, axis=-1, keepdims=True) + jnp.sum(p_new, axis=-1, keepdims=True)
    o = (jnp.dot(p_old.astype(BF16), v_old, preferred_element_type=F32)
         + jnp.dot(p_new.astype(BF16), v_new, preferred_element_type=F32)) / den
    half = GROUP * Q_PAD
    return o[:half, :HEAD_DIM], o[half:, HEAD_DIM:]


def _add_gated(prev_ref, gate_ref, o_ref, per_g, branch):
    prev, gates = prev_ref[...], gate_ref[...]
    for g in range(N_KV_HEADS):
        for r in range(GROUP):
            hd = GROUP * g + r
            cols = slice(hd * HEAD_DIM, (hd + 1) * HEAD_DIM)
            o_ref[:, cols] = (prev[:, cols] + gates[:, 3 * hd + branch:3 * hd + branch + 1]
                              * per_g[g][r * Q_PAD:(r + 1) * Q_PAD])


def _new_rows_bias(extra):
    q = lax.broadcasted_iota(jnp.int32, (N_KV_HEADS, Q_PAD, NEW_PAD), 1)
    i = lax.broadcasted_iota(jnp.int32, (N_KV_HEADS, Q_PAD, NEW_PAD), 2)
    b = jnp.where(i <= q, 0.0, NEG_INF)
    return b if extra is None else b + extra


def _sample_slc_kernel(pt_ref, cache_hbm, q_ref, sel_ref, new_ref, e_ref, prev_ref, gate_ref, o_ref,
                       buf, sem):
    n_pages = pt_ref.shape[1]
    slot = _paged_prologue(pt_ref, cache_hbm, buf, sem, n_pages, PAGE_SIZE)
    k_width = N_KV_HEADS * HEAD_DIM
    nsp = sel_ref.shape[1] // N_KV_HEADS
    n_past_blk = n_pages * (PAGE_SIZE // SLC_BLOCK)
    qbd = q_ref[...].astype(BF16)
    sel = jnp.concatenate([sel_ref[:, :nsp], sel_ref[:, nsp:]], axis=0)
    bias_old = jnp.dot(sel.astype(BF16), e_ref[...], preferred_element_type=F32)
    bias_old = bias_old.reshape(N_KV_HEADS, Q_PAD, bias_old.shape[1])
    sel_new = sel[:, n_past_blk:n_past_blk + 1].reshape(N_KV_HEADS, Q_PAD, 1)
    k_old = buf[slot, :, :k_width].astype(BF16)
    v_old = buf[slot, :, k_width:].astype(BF16)
    new = new_ref[...].astype(BF16)
    per_g = _two_piece_attention(qbd, k_old, v_old, bias_old, new[:, :k_width], new[:, k_width:],
                                 _new_rows_bias(sel_new))
    _add_gated(prev_ref, gate_ref, o_ref, per_g, 1)


def _sample_slc(page_table, cache, q_s, sel, new_rows, expand, prev, gates):
    db, n_pages = page_table.shape
    past = n_pages * PAGE_SIZE
    seq3 = lambda i, pt: (i, 0, 0)
    return pl.pallas_call(
        _sample_slc_kernel,
        grid_spec=pltpu.PrefetchScalarGridSpec(
            num_scalar_prefetch=1,
            grid=(db,),
            in_specs=[pl.BlockSpec(memory_space=pl.ANY),
                      pl.BlockSpec((None, N_HEADS * Q_PAD, LANES), seq3),
                      pl.BlockSpec((None, Q_PAD, sel.shape[2]), seq3),
                      pl.BlockSpec((None, NEW_PAD, KV_WIDTH), seq3),
                      pl.BlockSpec(expand.shape, lambda i, pt: (0, 0)),
                      pl.BlockSpec((None, Q_PAD, NSA_WIDTH), seq3),
                      pl.BlockSpec((None, Q_PAD, LANES), seq3)],
            out_specs=pl.BlockSpec((None, Q_PAD, NSA_WIDTH), seq3),
            scratch_shapes=[pltpu.VMEM((2, past, KV_WIDTH), F32), pltpu.SemaphoreType.DMA((2,))],
        ),
        out_shape=jax.ShapeDtypeStruct((db, Q_PAD, NSA_WIDTH), F32),
        compiler_params=_params("arbitrary"),
        name="sample_slc",
    )(page_table, cache, q_s, sel, new_rows, expand, prev, gates)


def _sample_win_kernel(q_ref, st_ref, new_ref, prev_ref, gate_ref, o_ref, st_out_ref, *, dec_seq):
    wb = st_ref.shape[0]
    k_width = N_KV_HEADS * HEAD_DIM
    qbd = q_ref[...].astype(BF16)
    q = lax.broadcasted_iota(jnp.int32, (N_KV_HEADS, Q_PAD, wb), 1)
    i = lax.broadcasted_iota(jnp.int32, (N_KV_HEADS, Q_PAD, wb), 2)
    bias_old = jnp.where(wb + q - i <= WINDOW, 0.0, NEG_INF)
    st = st_ref[...]
    new = new_ref[...]
    st_b, new_b = st.astype(BF16), new.astype(BF16)
    per_g = _two_piece_attention(qbd, st_b[:, :k_width], st_b[:, k_width:], bias_old,
                                 new_b[:, :k_width], new_b[:, k_width:], _new_rows_bias(None))
    _add_gated(prev_ref, gate_ref, o_ref, per_g, 2)
    st_out_ref[0:wb - dec_seq, :] = st[dec_seq:wb, :]
    st_out_ref[wb - dec_seq:wb, :] = new[0:dec_seq, :]


def _sample_win(q_s, st_win, new_rows, prev, gates, dec_seq):
    db, wb, _ = st_win.shape
    seq3 = lambda i: (i, 0, 0)
    return pl.pallas_call(
        functools.partial(_sample_win_kernel, dec_seq=dec_seq),
        grid=(db,),
        in_specs=[pl.BlockSpec((None, N_HEADS * Q_PAD, LANES), seq3),
                  pl.BlockSpec((None, wb, KV_WIDTH), seq3),
                  pl.BlockSpec((None, NEW_PAD, KV_WIDTH), seq3),
                  pl.BlockSpec((None, Q_PAD, NSA_WIDTH), seq3),
                  pl.BlockSpec((None, Q_PAD, LANES), seq3)],
        out_specs=[pl.BlockSpec((None, Q_PAD, NSA_WIDTH), seq3), pl.BlockSpec((None, wb, KV_WIDTH), seq3)],
        out_shape=[jax.ShapeDtypeStruct((db, Q_PAD, NSA_WIDTH), F32),
                   jax.ShapeDtypeStruct((db, wb, KV_WIDTH), F32)],
        compiler_params=_params("arbitrary"),
        name="sample_win",
    )(q_s, st_win, new_rows, prev, gates)


POOL_TM = 512


def _pool_project(d_groups, wp_ref, scale_ref, o_ref):
    for gi, d in enumerate(d_groups):
        cols = slice(gi * POOL_GROUP_WIDTH, (gi + 1) * POOL_GROUP_WIDTH)
        y = jnp.dot(d.astype(BF16), wp_ref[gi], preferred_element_type=F32)
        o_ref[:, cols] = (y * scale_ref[:, cols]).astype(o_ref.dtype)


def _pool_prompt_kernel(u_ref, wp_ref, scale_ref, o_ref, ext_ref):
    tm = o_ref.shape[0]
    t0 = pl.multiple_of(pl.program_id(1) * tm, tm)
    halo = u_ref[pl.ds(pl.multiple_of(jnp.maximum(t0 - POOL_HALO, 0), POOL_HALO), POOL_HALO), :]
    ext_ref[0:POOL_HALO, :] = jnp.where(t0 > 0, halo, 0.0)
    ext_ref[POOL_HALO:, :] = u_ref[pl.ds(t0, tm), :]
    pos = t0 + lax.broadcasted_iota(jnp.int32, (tm, POOL_GROUP_WIDTH), 0)
    d_groups = []
    for gi, w in enumerate(POOL_WINDOWS):
        cols = slice(gi * POOL_GROUP_WIDTH, (gi + 1) * POOL_GROUP_WIDTH)
        e = ext_ref[:, cols]
        acc = e
        span = 1
        while span < w:
            acc = acc + pltpu.roll(acc, span, axis=0)
            span *= 2
        cnt = jnp.minimum(w, pos + 1).astype(F32)
        d_groups.append(acc[POOL_HALO:] / cnt - e[POOL_HALO:])
    _pool_project(d_groups, wp_ref, scale_ref, o_ref)


def _pool_prompt(u, wp, scale):
    b, t, pw = u.shape
    tm = min(POOL_TM, t)
    return pl.pallas_call(
        _pool_prompt_kernel,
        grid=(b, t // tm),
        in_specs=[pl.BlockSpec((None, t, pw), lambda i, k: (i, 0, 0)), _const_spec(wp.shape), _const_spec(scale.shape)],
        out_specs=pl.BlockSpec((None, tm, pw), lambda i, k: (i, k, 0)),
        out_shape=jax.ShapeDtypeStruct((b, t, pw), BF16),
        scratch_shapes=[pltpu.VMEM((POOL_HALO + tm, pw), F32)],
        compiler_params=_params("arbitrary", "arbitrary"),
        name="pool_prompt",
    )(u, wp, scale)


def _pool_sample_kernel(ext_ref, wp_ref, scale_ref, o_ref, *, past_len, dec_seq):
    db = ext_ref.shape[1]
    for q in range(dec_seq):
        d_groups = []
        for gi, w in enumerate(POOL_WINDOWS):
            cols = slice(gi * POOL_GROUP_WIDTH, (gi + 1) * POOL_GROUP_WIDTH)
            row = POOL_HALO + q
            acc = ext_ref[row, :, cols]
            for i in range(1, w):
                acc = acc + ext_ref[row - i, :, cols]
            cnt = float(min(w, past_len + q + 1))
            d_groups.append(acc / cnt - ext_ref[row, :, cols])
        _pool_project(d_groups, wp_ref, scale_ref, o_ref.at[pl.ds(q * db, db)])


def _pool_sample(ext, wp, scale, past_len, dec_seq):
    rows, db, pw = ext.shape
    return pl.pallas_call(
        functools.partial(_pool_sample_kernel, past_len=past_len, dec_seq=dec_seq),
        grid=(1,),
        in_specs=[_const_spec(ext.shape), _const_spec(wp.shape), _const_spec(scale.shape)],
        out_specs=_const_spec((dec_seq * db, pw)),
        out_shape=jax.ShapeDtypeStruct((dec_seq * db, pw), BF16),
        compiler_params=_params("arbitrary"),
        name="pool_sample",
    )(ext, wp, scale)


OUTPROJ_TM = 512


def _outproj_kernel(x_ref, o_ref, p_ref, wo_ref, wp_ref, y_ref):
    y_ref[...] = (x_ref[...] + jnp.dot(o_ref[...].astype(BF16), wo_ref[...], preferred_element_type=F32)
                  + jnp.dot(p_ref[...], wp_ref[...], preferred_element_type=F32))


def _outproj(x, o, p, wo, wp):
    m, d = x.shape
    tm = min(OUTPROJ_TM, m)
    row = lambda i: (i, 0)
    return pl.pallas_call(
        _outproj_kernel,
        grid=(m // tm,),
        in_specs=[pl.BlockSpec((tm, d), row), pl.BlockSpec((tm, o.shape[1]), row), pl.BlockSpec((tm, p.shape[1]), row),
                  _const_spec(wo.shape), _const_spec(wp.shape)],
        out_specs=pl.BlockSpec((tm, d), row),
        out_shape=jax.ShapeDtypeStruct((m, d), F32),
        compiler_params=_params("arbitrary"),
        name="outproj",
    )(x, o, p, wo, wp)


def _cmp_weights(w1, w2, pe):
    n_slots = CMP_BLOCK // CMP_STRIDE
    w1s = w1.reshape(n_slots, CMP_STRIDE * HEAD_DIM, CMP_HIDDEN)
    w1cat = jnp.concatenate([w1s[h] for h in range(n_slots)], axis=1).astype(BF16)
    pe8 = jnp.pad(pe.reshape(n_slots, CMP_STRIDE * HEAD_DIM), ((0, 8 - n_slots), (0, 0)))
    return w1cat, w2.astype(BF16), pe8


def _pad_axis(x, axis, size):
    pad = [(0, 0)] * x.ndim
    pad[axis] = (0, size - x.shape[axis])
    return jnp.pad(x, pad)


def kernel(x_prompt, x_sample, cache_kv_cmp, cache_kv_slc, page_table, state_kv_win, state_pool, n_ffn1, w_ffn1_gate, w_ffn1_up, w_ffn1_down, n_mix, w_in, w_cmp_k1, w_cmp_k2, pe_cmp_k, w_cmp_v1, w_cmp_v2, pe_cmp_v, w_pool, pool_scale, w_out, n_ffn2, w_ffn2_gate, w_ffn2_up, w_ffn2_down, n_final):
    b, t, d = x_prompt.shape
    db, ds, _ = x_sample.shape
    depth = w_in.shape[0]
    n_pages = page_table.shape[1]
    past = n_pages * PAGE_SIZE
    n_phys = cache_kv_cmp.shape[1]
    wb = state_kv_win.shape[2]
    assert t % ATTN_TQ == 0 and t % CMP_STRIDE == 0 and WINDOW % ATTN_TQ == 0 and t >= WINDOW
    assert ds <= Q_PAD and ds <= CMP_STRIDE and wb == WINDOW and past % SLC_BLOCK == 0
    assert (t // CMP_STRIDE) % CMP_ROWS == 0 and (past // CMP_STRIDE) % CMP_ROWS == 0

    xp = x_prompt.reshape(b * t, d)
    xs = x_sample.reshape(db * ds, d)
    pos_p = jnp.arange(t, dtype=jnp.int32)
    pos_s = past + jnp.arange(ds, dtype=jnp.int32)
    tab_p = _rope_tables(pos_p)
    tab_s = tuple(jnp.tile(a, (db, 1)) for a in _rope_tables(pos_s))

    n_chunk_p = t // CMP_STRIDE
    n_slc_p = -(-t // SLC_BLOCK)
    agg_p = _agg_matrix(n_chunk_p, n_chunk_p - 1, n_slc_p, LANES * (-(-n_slc_p // LANES)))
    expand_p = _expand_matrix(t, agg_p.shape[1], ATTN_TQ)
    n_chunk_s = past // CMP_STRIDE
    n_slc_s = -(-(past + ds) // SLC_BLOCK)
    agg_s = _agg_matrix(n_chunk_s, n_chunk_s, n_slc_s, LANES * (-(-n_slc_s // LANES)))
    expand_s = _expand_matrix(past, agg_s.shape[1], past)[0]

    st_p = ([], [], [], [])
    st_s = ([], [], [], [])
    y_p = y_s = None
    for l in range(depth):
        last = l == depth - 1
        ffn1 = (n_ffn1[l], w_ffn1_gate[l].astype(BF16), w_ffn1_up[l].astype(BF16), w_ffn1_down[l].astype(BF16))
        ffn2 = (n_ffn2[l], w_ffn2_gate[l].astype(BF16), w_ffn2_up[l].astype(BF16), w_ffn2_down[l].astype(BF16))
        w = w_in[l]
        o_kv, o_gate, o_pool = NSA_WIDTH, NSA_WIDTH + 3 * KV_WIDTH, NSA_WIDTH + 3 * KV_WIDTH + N_GATES
        wq = w[:, :o_kv].astype(BF16)
        wkv = w[:, o_kv:o_gate].astype(BF16)
        wgt = _pad_axis(w[:, o_gate:o_pool], 1, LANES).astype(BF16)
        wu = w[:, o_pool:].astype(BF16)
        wk1, wk2, pek = _cmp_weights(w_cmp_k1[l], w_cmp_k2[l], pe_cmp_k[l])
        wv1, wv2, pev = _cmp_weights(w_cmp_v1[l], w_cmp_v2[l], pe_cmp_v[l])
        pe = jnp.stack([pek, pev])
        wp = w_pool[l].astype(BF16)
        scale = pool_scale[l].reshape(1, POOL_WIDTH)
        wo_nsa = w_out[l][:NSA_WIDTH].astype(BF16)
        wo_pool = w_out[l][NSA_WIDTH:].astype(BF16)

        xp = _ffn(xp, *ffn1)
        q, kvc, kvs, kvw, kvs_hm, kvw_hm, gates, u = _inproj(xp, n_mix[l], wq, wkv, wgt, wu, tab_p, BF16)
        kc, vc = _compress_prompt(kvc.reshape(b, n_chunk_p, CHUNK_WIDTH), pe, wk1, wv1, wk2, wv2)
        q_hm = q
        gates3 = gates.reshape(b, t, LANES)
        oc, sel = _cmp_attn(q_hm, kc, vc, gates3, agg_p, tq=ATTN_TQ, pos0=0, tiled=True,
                            n_slc=n_slc_p, sel_dtype=BF16)
        o_mix = _prompt_attn(q_hm, kvs_hm, kvw_hm, sel, expand_p, oc, gates3)
        u3 = u.reshape(b, t, POOL_WIDTH)
        pool_out = _pool_prompt(u3, wp, scale)
        xp = _outproj(xp, o_mix.reshape(b * t, NSA_WIDTH), pool_out.reshape(b * t, POOL_WIDTH), wo_nsa, wo_pool)
        xp = _ffn(xp, *ffn2, g_final=n_final if last else None)
        st_p[0].append(kvc.reshape(b, t, 2, N_KV_HEADS, HEAD_DIM))
        st_p[1].append(kvs.reshape(b, t, 2, N_KV_HEADS, HEAD_DIM))
        st_p[2].append(kvw.reshape(b, t, 2, N_KV_HEADS, HEAD_DIM)[:, t - min(WINDOW, t):])
        st_p[3].append(u3[:, t - POOL_STATE:])

        xs = _ffn(xs, *ffn1)
        q, kvc, kvs, kvw, _, _, gates, u = _inproj(xs, n_mix[l], wq, wkv, wgt, wu, tab_s, F32)
        kvc3, kvs3, kvw3 = (a.reshape(db, ds, KV_WIDTH) for a in (kvc, kvs, kvw))
        xnew = _pad_axis(kvc3, 1, CMP_STRIDE).reshape(db, 1, CHUNK_WIDTH)
        cache_c = cache_kv_cmp[l].reshape(n_phys, PAGE_SIZE // CMP_STRIDE, CHUNK_WIDTH)
        kc, vc = _compress_sample(page_table, cache_c, xnew, pe, wk1, wv1, wk2, wv2)
        q_s = _pad_axis(q.reshape(N_HEADS, db, ds, HEAD_DIM).transpose(1, 0, 2, 3), 2, Q_PAD)
        q_bd = _block_diag_q(q_s)
        gates_s = _pad_axis(gates.reshape(db, ds, LANES), 1, Q_PAD)
        oc, sel = _cmp_attn(q_s, kc, vc, gates_s, agg_s, tq=Q_PAD, pos0=past, tiled=False,
                            n_slc=n_slc_s, sel_dtype=F32)
        cache_s = cache_kv_slc[l].reshape(n_phys, PAGE_SIZE, KV_WIDTH)
        o_cs = _sample_slc(page_table, cache_s, q_bd, sel, _pad_axis(kvs3, 1, NEW_PAD), expand_s, oc, gates_s)
        st_win = state_kv_win[l].reshape(db, wb, KV_WIDTH)
        o_mix, st_win_new = _sample_win(q_bd, st_win, _pad_axis(kvw3, 1, NEW_PAD), o_cs, gates_s, ds)
        u3 = u.reshape(db, ds, POOL_WIDTH)
        ext = jnp.concatenate([jnp.zeros((db, POOL_HALO - POOL_STATE, POOL_WIDTH), F32), state_pool[l], u3], axis=1)
        pool_out = _pool_sample(ext.transpose(1, 0, 2), wp, scale, past, ds)
        pool_out = pool_out.reshape(ds, db, POOL_WIDTH).transpose(1, 0, 2).reshape(db * ds, POOL_WIDTH)
        xs = _outproj(xs, o_mix[:, :ds].reshape(db * ds, NSA_WIDTH), pool_out, wo_nsa, wo_pool)
        xs = _ffn(xs, *ffn2, g_final=n_final if last else None)
        st_s[0].append(kvc3.reshape(db, ds, 2, N_KV_HEADS, HEAD_DIM))
        st_s[1].append(kvs3.reshape(db, ds, 2, N_KV_HEADS, HEAD_DIM))
        st_s[2].append(st_win_new.reshape(db, wb, 2, N_KV_HEADS, HEAD_DIM))
        st_s[3].append(jnp.concatenate([state_pool[l], u3], axis=1)[:, ds:])

    return (xp.reshape(b, t, d), xs.reshape(db, ds, d),
            jnp.stack(st_p[0]), jnp.stack(st_p[1]), jnp.stack(st_p[2]), jnp.stack(st_p[3]),
            jnp.stack(st_s[0]), jnp.stack(st_s[1]), jnp.stack(st_s[2]), jnp.stack(st_s[3]))
```

```python
import functools

import numpy as np
import jax
import jax.numpy as jnp
from jax import lax
from jax.experimental import pallas as pl
from jax.experimental.pallas import tpu as pltpu

F32 = jnp.float32
BF16 = jnp.bfloat16

HEAD_DIM = 64
N_HEADS = 8
N_KV_HEADS = 2
GROUP = N_HEADS // N_KV_HEADS
NSA_WIDTH = N_HEADS * HEAD_DIM
N_KV = 2 * N_KV_HEADS
KV_WIDTH = N_KV * HEAD_DIM
ROPE_DIM = HEAD_DIM // 4
ROPE_HALF = ROPE_DIM // 2
ROPE_THETA = 500000.0
CMP_BLOCK = 32
CMP_STRIDE = 16
CMP_HIDDEN = 256
CMP_FLAT = CMP_STRIDE * HEAD_DIM
SLC_BLOCK = 64
TOPK_BLOCKS = 16
WINDOW = 512
PAGE_SIZE = 128
POOL_WINDOWS = (2, 4, 8, 16)
POOL_GROUP_WIDTH = 128
POOL_WIDTH = POOL_GROUP_WIDTH * len(POOL_WINDOWS)
POOL_STATE = max(POOL_WINDOWS) - 1
POOL_HALO = 16
RMS_EPS = 1e-6
FORCE_SCORE = 1e4
NEG_INF = -1e30
ATTN_SCALE = HEAD_DIM ** -0.5
N_GATES = 3 * N_HEADS
CHUNK_WIDTH = CMP_STRIDE * KV_WIDTH

LANES = 128
SUBLANES = 8
VMEM_LIMIT_BYTES = 56 * 1024 * 1024

_NT = (((1,), (1,)), ((), ()))


def _params(*sem):
    return pltpu.CompilerParams(dimension_semantics=sem, vmem_limit_bytes=VMEM_LIMIT_BYTES)


def _rmsnorm(x, g):
    return x * lax.rsqrt(jnp.mean(x * x, axis=-1, keepdims=True) + RMS_EPS) * g


def _const_spec(shape):
    n = len(shape)
    return pl.BlockSpec(shape, lambda *_: (0,) * n)


FFN_TM = 512
FFN_CHUNK = 256


def _ffn_kernel(x_ref, g_ref, wg_ref, wu_ref, wd_ref, gf_ref, o_ref, act_ref, *, final_norm):
    x = x_ref[...]
    h = _rmsnorm(x, g_ref[...]).astype(BF16)
    d_ff = wg_ref.shape[1]
    for c in range(d_ff // FFN_CHUNK):
        sl = slice(c * FFN_CHUNK, (c + 1) * FFN_CHUNK)
        a = jnp.dot(h, wg_ref[:, sl], preferred_element_type=F32)
        u = jnp.dot(h, wu_ref[:, sl], preferred_element_type=F32)
        act_ref[:, sl] = (a * jax.nn.sigmoid(a) * u).astype(BF16)
    y = x + 0.5 * jnp.dot(act_ref[...], wd_ref[...], preferred_element_type=F32)
    if final_norm:
        y = _rmsnorm(y, gf_ref[...])
    o_ref[...] = y


def _ffn(x, g, wg, wu, wd, g_final=None):
    m, d = x.shape
    d_ff = wg.shape[1]
    tm = min(FFN_TM, m)
    final_norm = g_final is not None
    gf = g_final if final_norm else g
    return pl.pallas_call(
        functools.partial(_ffn_kernel, final_norm=final_norm),
        grid=(m // tm,),
        in_specs=[
            pl.BlockSpec((tm, d), lambda i: (i, 0)),
            _const_spec((1, d)),
            pl.BlockSpec((d, d_ff), lambda i: (0, 0), pipeline_mode=pl.Buffered(1)),
            pl.BlockSpec((d, d_ff), lambda i: (0, 0), pipeline_mode=pl.Buffered(1)),
            pl.BlockSpec((d_ff, d), lambda i: (0, 0), pipeline_mode=pl.Buffered(1)),
            _const_spec((1, d)),
        ],
        out_specs=pl.BlockSpec((tm, d), lambda i: (i, 0)),
        out_shape=jax.ShapeDtypeStruct((m, d), F32),
        scratch_shapes=[pltpu.VMEM((tm, d_ff), BF16)],
        compiler_params=_params("arbitrary"),
        name="ffn",
    )(x, g.reshape(1, d), wg, wu, wd, gf.reshape(1, d))


INPROJ_TM = 512
ATTN_TQ = 256


def _rope_tables(pos):
    inv = jnp.power(ROPE_THETA, -jnp.arange(ROPE_HALF, dtype=F32) / ROPE_HALF)
    ang = pos.astype(F32)[:, None] * inv[None, :]
    cos, sin = jnp.cos(ang), jnp.sin(ang)
    t = pos.shape[0]
    rest = HEAD_DIM - ROPE_DIM
    z_half = jnp.zeros((t, ROPE_HALF), F32)
    z_rest = jnp.zeros((t, rest), F32)
    cos_t = jnp.concatenate([cos, cos, jnp.ones((t, rest), F32)], axis=1)
    sin_a = jnp.concatenate([-sin, z_half, z_rest], axis=1)
    sin_b = jnp.concatenate([z_half, sin, z_rest], axis=1)
    rep = LANES // HEAD_DIM
    return tuple(jnp.tile(a, (1, rep)) for a in (cos_t, sin_a, sin_b)) + (cos.T, sin.T)


def _rope(z, cos_t, sin_a, sin_b):
    outs = []
    for c in range(z.shape[1] // LANES):
        zc = z[:, c * LANES:(c + 1) * LANES]
        outs.append(zc * cos_t + pltpu.roll(zc, LANES - ROPE_HALF, axis=1) * sin_a
                    + pltpu.roll(zc, ROPE_HALF, axis=1) * sin_b)
    return outs[0] if len(outs) == 1 else jnp.concatenate(outs, axis=1)


def _rope_fm(kt, cos, sin):
    x1, x2 = kt[0:ROPE_HALF], kt[ROPE_HALF:ROPE_DIM]
    return jnp.concatenate([x1 * cos - x2 * sin, x2 * cos + x1 * sin, kt[ROPE_DIM:]], axis=0)


def _inproj_kernel(x_ref, g_ref, wq_ref, wkv_t_ref, wkc_ref, wgt_ref, wu_ref,
                   cos_ref, sa_ref, sb_ref, cos_fm_ref, sin_fm_ref,
                   q_ref, kvc_rm_ref, kvc_ref, kvs_ref, kvw_ref, kvs_b_ref, kvw_b_ref, gate_ref, u_ref):
    h = _rmsnorm(x_ref[...], g_ref[...]).astype(BF16)
    tm = h.shape[0]
    tk = kvs_b_ref.shape[-1]
    cos_t, sin_a, sin_b = cos_ref[...], sa_ref[...], sb_ref[...]
    cos_fm, sin_fm = cos_fm_ref[...], sin_fm_ref[...]
    q = _rope(jnp.dot(h, wq_ref[...], preferred_element_type=F32), cos_t, sin_a, sin_b) * ATTN_SCALE
    for hd in range(N_HEADS):
        q_ref[hd] = q[:, hd * HEAD_DIM:(hd + 1) * HEAD_DIM].astype(q_ref.dtype)
    k_width = N_KV_HEADS * HEAD_DIM
    kvc = jnp.dot(h, wkc_ref[...], preferred_element_type=F32)
    kvc_rm_ref[...] = jnp.concatenate([_rope(kvc[:, :k_width], cos_t, sin_a, sin_b), kvc[:, k_width:]], axis=1)
    kv_t = lax.dot_general(wkv_t_ref[...], h, _NT, preferred_element_type=F32)
    for i, (f_ref, b_ref) in enumerate(((kvc_ref, None), (kvs_ref, kvs_b_ref), (kvw_ref, kvw_b_ref))):
        for j in range(N_KV):
            r0 = i * KV_WIDTH + j * HEAD_DIM
            blk = kv_t[r0:r0 + HEAD_DIM]
            if j < N_KV_HEADS:
                blk = _rope_fm(blk, cos_fm, sin_fm)
            f_ref[j] = blk
            if b_ref is not None:
                for c in range(tm // tk):
                    b_ref[j, c] = blk[:, c * tk:(c + 1) * tk].astype(BF16)
    gate_ref[...] = jax.nn.sigmoid(jnp.dot(h, wgt_ref[...], preferred_element_type=F32))
    u_ref[...] = jnp.dot(h, wu_ref[...], preferred_element_type=F32)


def _inproj(x, g, wq, wkv_t, wkc, wgt, wu, tables, q_dtype):
    m, d = x.shape
    seq = tables[0].shape[0]
    nseq = m // seq
    tm = min(INPROJ_TM, seq)
    tk = min(ATTN_TQ, tm)
    n_tab = seq // tm
    row = lambda i: (i, 0)
    tab = lambda i: (i % n_tab, 0)
    tab_fm = lambda i: (0, i % n_tab)
    hm = lambda i: (i // n_tab, 0, i % n_tab, 0)
    fm = lambda i: (i // n_tab, 0, 0, i % n_tab)
    fmb = lambda i: (i // n_tab, 0, i % n_tab, 0, 0)
    fm_shape = jax.ShapeDtypeStruct((nseq, N_KV, HEAD_DIM, seq), F32)
    fmb_shape = jax.ShapeDtypeStruct((nseq, N_KV, seq // tk, HEAD_DIM, tk), BF16)
    fm_spec = pl.BlockSpec((None, N_KV, HEAD_DIM, tm), fm)
    fmb_spec = pl.BlockSpec((None, N_KV, tm // tk, HEAD_DIM, tk), fmb)
    return pl.pallas_call(
        _inproj_kernel,
        grid=(m // tm,),
        in_specs=[
            pl.BlockSpec((tm, d), row), _const_spec((1, d)),
            _const_spec(wq.shape), _const_spec(wkv_t.shape), _const_spec(wkc.shape),
            _const_spec(wgt.shape), _const_spec(wu.shape),
            pl.BlockSpec((tm, LANES), tab), pl.BlockSpec((tm, LANES), tab), pl.BlockSpec((tm, LANES), tab),
            pl.BlockSpec((ROPE_HALF, tm), tab_fm), pl.BlockSpec((ROPE_HALF, tm), tab_fm),
        ],
        out_specs=[
            pl.BlockSpec((None, N_HEADS, tm, HEAD_DIM), hm),
            pl.BlockSpec((tm, KV_WIDTH), row),
            fm_spec, fm_spec, fm_spec, fmb_spec, fmb_spec,
            pl.BlockSpec((tm, LANES), row), pl.BlockSpec((tm, POOL_WIDTH), row),
        ],
        out_shape=[
            jax.ShapeDtypeStruct((nseq, N_HEADS, seq, HEAD_DIM), q_dtype),
            jax.ShapeDtypeStruct((m, KV_WIDTH), F32),
            fm_shape, fm_shape, fm_shape, fmb_shape, fmb_shape,
            jax.ShapeDtypeStruct((m, LANES), F32), jax.ShapeDtypeStruct((m, POOL_WIDTH), F32),
        ],
        compiler_params=_params("arbitrary"),
        name="inproj",
    )(x, g.reshape(1, d), wq, wkv_t, wkc, wgt, wu, *tables)


CMP_ROWS = 128


def _gelu_tanh(x):
    return 0.5 * x * (1.0 + jnp.tanh(np.sqrt(2.0 / np.pi).astype(np.float32) * (x + 0.044715 * (x * x * x))))


def _interleave_heads(a0, a1):
    low = lax.broadcasted_iota(jnp.int32, a0.shape, 1) < HEAD_DIM
    return (jnp.where(low, a0, pltpu.roll(a1, HEAD_DIM, axis=1)),
            jnp.where(low, pltpu.roll(a0, HEAD_DIM, axis=1), a1))


def _pick_row(blk, r):
    row8 = lax.broadcasted_iota(jnp.int32, (SUBLANES, 1), 0)
    return jnp.sum(jnp.where(row8 == r, blk, 0.0), axis=0, keepdims=True)


def _compress_finish(kv, n, new8, pe_ref, w1_ref, w2_ref, out_ref, y_ref, has_new):
    y_ref[pl.ds(2 * n, 2 * SUBLANES), :] = jnp.concatenate([new8, pe_ref[kv]], axis=0).astype(BF16)
    p = jnp.dot(y_ref[...], w1_ref[...], preferred_element_type=F32)
    p_new, p_pe = p[2 * n:2 * n + SUBLANES], p[2 * n + SUBLANES:2 * n + 2 * SUBLANES]
    bias = _pick_row(p_pe[:, :CMP_HIDDEN], 0) + _pick_row(p_pe[:, CMP_HIDDEN:], 1)
    last = lax.broadcasted_iota(jnp.int32, (n, CMP_HIDDEN), 0) == n - 1
    for g in range(N_KV_HEADS):
        slot0 = p[g * n:(g + 1) * n, :CMP_HIDDEN]
        slot1 = pltpu.roll(p[g * n:(g + 1) * n, CMP_HIDDEN:], n - 1, axis=0)
        if has_new:
            slot1 = jnp.where(last, _pick_row(p_new[:, CMP_HIDDEN:], g), slot1)
        hid = _gelu_tanh(slot0 + slot1 + bias).astype(BF16)
        out_ref[g] = jnp.dot(hid, w2_ref[...], preferred_element_type=F32).astype(BF16)


def _compress_prompt_kernel(x_ref, pe_ref, wk1_ref, wv1_ref, wk2_ref, wv2_ref, kc_ref, vc_ref, y_ref):
    n = x_ref.shape[0]
    for kv, (w1_ref, w2_ref, out_ref) in enumerate(((wk1_ref, wk2_ref, kc_ref), (wv1_ref, wv2_ref, vc_ref))):
        def fill(r, carry):
            r0 = pl.multiple_of(r * CMP_ROWS, CMP_ROWS)
            for j in range(CMP_STRIDE // 2):
                c0 = (2 * j) * KV_WIDTH + kv * LANES
                c1 = (2 * j + 1) * KV_WIDTH + kv * LANES
                y0, y1 = _interleave_heads(x_ref[pl.ds(r0, CMP_ROWS), c0:c0 + LANES],
                                           x_ref[pl.ds(r0, CMP_ROWS), c1:c1 + LANES])
                y_ref[pl.ds(r0, CMP_ROWS), j * LANES:(j + 1) * LANES] = y0.astype(BF16)
                y_ref[pl.ds(n + r0, CMP_ROWS), j * LANES:(j + 1) * LANES] = y1.astype(BF16)
            return carry
        lax.fori_loop(0, n // CMP_ROWS, fill, 0)
        _compress_finish(kv, n, jnp.zeros((SUBLANES, CMP_FLAT), F32), pe_ref, w1_ref, w2_ref, out_ref, y_ref, False)


def _compress_prompt(x, pe, wk1, wv1, wk2, wv2):
    b, n, cw = x.shape
    out = jax.ShapeDtypeStruct((b, N_KV_HEADS, n, HEAD_DIM), BF16)
    out_spec = pl.BlockSpec((None, N_KV_HEADS, n, HEAD_DIM), lambda i: (i, 0, 0, 0))
    return pl.pallas_call(
        _compress_prompt_kernel,
        grid=(b,),
        in_specs=[pl.BlockSpec((None, n, cw), lambda i: (i, 0, 0)), _const_spec(pe.shape),
                  _const_spec(wk1.shape), _const_spec(wv1.shape), _const_spec(wk2.shape), _const_spec(wv2.shape)],
        out_specs=[out_spec, out_spec],
        out_shape=[out, out],
        scratch_shapes=[pltpu.VMEM((2 * n + 2 * SUBLANES, CMP_FLAT), BF16)],
        compiler_params=_params("arbitrary"),
        name="compress_prompt",
    )(x, pe, wk1, wv1, wk2, wv2)


def _paged_prologue(start_fetch, wait_fetch):
    b = pl.program_id(0)
    slot = lax.rem(b, 2)

    @pl.when(b == 0)
    def _():
        start_fetch(0, 0)

    @pl.when(b + 1 < pl.num_programs(0))
    def _():
        start_fetch(b + 1, 1 - slot)

    wait_fetch(slot)
    return slot


def _compress_sample_kernel(pt_ref, cache_hbm, ynew_ref, pe_ref, wk1_ref, wv1_ref, wk2_ref, wv2_ref,
                            kc_ref, vc_ref, buf, sem, rows_ref, y_ref):
    n_pages = pt_ref.shape[1]
    n = n_pages * (PAGE_SIZE // CMP_STRIDE)

    def copy(page, slot, p):
        return pltpu.make_async_copy(cache_hbm.at[page], buf.at[slot, p], sem.at[slot])

    def start_fetch(seq, slot):
        def body(p, carry):
            copy(pt_ref[seq, p], slot, p).start()
            return carry
        lax.fori_loop(0, n_pages, body, 0)

    def wait_fetch(slot):
        def body(p, carry):
            copy(0, slot, p).wait()
            return carry
        lax.fori_loop(0, n_pages, body, 0)

    slot = _paged_prologue(start_fetch, wait_fetch)
    ynew = ynew_ref[...]
    row8 = lax.broadcasted_iota(jnp.int32, (SUBLANES, 1), 0)
    for kv, (w1_ref, w2_ref, out_ref) in enumerate(((wk1_ref, wk2_ref, kc_ref), (wv1_ref, wv2_ref, vc_ref))):
        def to_rows(p, carry):
            tile = buf[slot, p, pl.ds(N_KV_HEADS * kv, N_KV_HEADS)].reshape(LANES, PAGE_SIZE)
            rows_ref[pl.ds(pl.multiple_of(p * PAGE_SIZE, PAGE_SIZE), PAGE_SIZE), :] = tile.T
            return carry
        lax.fori_loop(0, n_pages, to_rows, 0)
        for j in range(CMP_STRIDE // 2):
            y0, y1 = _interleave_heads(rows_ref[pl.ds(2 * j, n, stride=CMP_STRIDE), :],
                                       rows_ref[pl.ds(2 * j + 1, n, stride=CMP_STRIDE), :])
            y_ref[0:n, j * LANES:(j + 1) * LANES] = y0.astype(BF16)
            y_ref[n:2 * n, j * LANES:(j + 1) * LANES] = y1.astype(BF16)
        new8 = jnp.where(row8 == 0, _pick_row(ynew, N_KV_HEADS * kv),
                         jnp.where(row8 == 1, _pick_row(ynew, N_KV_HEADS * kv + 1), 0.0))
        _compress_finish(kv, n, new8, pe_ref, w1_ref, w2_ref, out_ref, y_ref, True)


def _compress_sample(page_table, cache, ynew, pe, wk1, wv1, wk2, wv2):
    db, n_pages = page_table.shape
    n = n_pages * (PAGE_SIZE // CMP_STRIDE)
    out = jax.ShapeDtypeStruct((db, N_KV_HEADS, n, HEAD_DIM), BF16)
    out_spec = pl.BlockSpec((None, N_KV_HEADS, n, HEAD_DIM), lambda i, pt: (i, 0, 0, 0))
    const = lambda s: pl.BlockSpec(s, lambda i, pt: (0,) * len(s))
    return pl.pallas_call(
        _compress_sample_kernel,
        grid_spec=pltpu.PrefetchScalarGridSpec(
            num_scalar_prefetch=1,
            grid=(db,),
            in_specs=[pl.BlockSpec(memory_space=pl.ANY),
                      pl.BlockSpec((None, SUBLANES, CMP_FLAT), lambda i, pt: (i, 0, 0)),
                      const(pe.shape), const(wk1.shape), const(wv1.shape), const(wk2.shape), const(wv2.shape)],
            out_specs=[out_spec, out_spec],
            scratch_shapes=[pltpu.VMEM((2, n_pages, N_KV, HEAD_DIM, PAGE_SIZE), F32),
                            pltpu.SemaphoreType.DMA((2,)),
                            pltpu.VMEM((n_pages * PAGE_SIZE, LANES), F32),
                            pltpu.VMEM((2 * n + 2 * SUBLANES, CMP_FLAT), BF16)],
        ),
        out_shape=[out, out],
        compiler_params=_params("arbitrary"),
        name="compress_sample",
    )(page_table, cache, ynew, pe, wk1, wv1, wk2, wv2)


def _cmp_attn_kernel(q_ref, kc_ref, vc_ref, gate_ref, agg_ref, oc_ref, sel_ref, *, pos0, tiled, n_slc):
    tq = q_ref.shape[1]
    n = kc_ref.shape[1]
    nsp = agg_ref.shape[1]
    t0 = pos0 + (pl.program_id(1) * tq if tiled else 0)
    t_n = t0 + lax.broadcasted_iota(jnp.int32, (tq, n), 0)
    blk_end = lax.broadcasted_iota(jnp.int32, (tq, n), 1) * CMP_STRIDE + (CMP_BLOCK - 1)
    valid = (blk_end <= t_n)[None]
    t_s = t0 + lax.broadcasted_iota(jnp.int32, (tq, nsp), 0)
    j = lax.broadcasted_iota(jnp.int32, (tq, nsp), 1)
    cur = t_s // SLC_BLOCK
    forced = (j == 0) | (j == cur) | (j == cur - 1)
    real = j < n_slc
    gates = gate_ref[...]
    for g in range(N_KV_HEADS):
        qg = jnp.concatenate([q_ref[GROUP * g + r].astype(F32) for r in range(GROUP)], axis=0).astype(BF16)
        s = lax.dot_general(qg, kc_ref[g], _NT, preferred_element_type=F32).reshape(GROUP, tq, n)
        s = jnp.where(valid, s, NEG_INF)
        e = jnp.where(valid, jnp.exp(s - jnp.max(s, axis=-1, keepdims=True)), 0.0)
        p = e / jnp.maximum(jnp.sum(e, axis=-1, keepdims=True), 1e-30)
        o = jnp.dot(p.reshape(GROUP * tq, n).astype(BF16), vc_ref[g], preferred_element_type=F32)
        p_grp = jnp.sum(p, axis=0)
        p_hi = p_grp.astype(BF16)
        p_lo = (p_grp - p_hi.astype(F32)).astype(BF16)
        p_slc = (jnp.dot(p_hi, agg_ref[...], preferred_element_type=F32)
                 + jnp.dot(p_lo, agg_ref[...], preferred_element_type=F32))
        score = jnp.where(forced, FORCE_SCORE, jnp.where(j <= cur, p_slc, NEG_INF))
        score = jnp.where(real, score, -jnp.inf)
        rank = jnp.zeros((tq, nsp), jnp.int32)
        for jp in range(n_slc):
            c = score[:, jp:jp + 1]
            rank = rank + ((c > score) | ((c == score) & (j > jp))).astype(jnp.int32)
        sel = (rank < TOPK_BLOCKS) & real
        sel_ref[:, g * nsp:(g + 1) * nsp] = jnp.where(sel, 0.0, NEG_INF).astype(sel_ref.dtype)
        for r in range(GROUP):
            hd = GROUP * g + r
            oc_ref[:, hd * HEAD_DIM:(hd + 1) * HEAD_DIM] = gates[:, 3 * hd:3 * hd + 1] * o[r * tq:(r + 1) * tq]


def _agg_matrix(n_cmp_pad, n_cmp, n_slc, n_slc_pad):
    c0 = np.arange(n_cmp)[:, None] * CMP_STRIDE
    s0 = np.arange(n_slc)[None, :] * SLC_BLOCK
    overlap = np.clip(np.minimum(c0 + CMP_BLOCK, s0 + SLC_BLOCK) - np.maximum(c0, s0), 0, None)
    agg = np.zeros((n_cmp_pad, n_slc_pad), np.float32)
    agg[:n_cmp, :n_slc] = overlap / CMP_BLOCK
    return jnp.asarray(agg, BF16)


def _cmp_attn(q_hm, kc, vc, gates, agg, *, tq, pos0, tiled, n_slc, sel_dtype):
    b, _, t, _ = q_hm.shape
    n = kc.shape[2]
    nsp = agg.shape[1]
    return pl.pallas_call(
        functools.partial(_cmp_attn_kernel, pos0=pos0, tiled=tiled, n_slc=n_slc),
        grid=(b, t // tq),
        in_specs=[
            pl.BlockSpec((None, N_HEADS, tq, HEAD_DIM), lambda i, k: (i, 0, k, 0)),
            pl.BlockSpec((None, N_KV_HEADS, n, HEAD_DIM), lambda i, k: (i, 0, 0, 0)),
            pl.BlockSpec((None, N_KV_HEADS, n, HEAD_DIM), lambda i, k: (i, 0, 0, 0)),
            pl.BlockSpec((None, tq, LANES), lambda i, k: (i, k, 0)),
            _const_spec(agg.shape),
        ],
        out_specs=[pl.BlockSpec((None, tq, NSA_WIDTH), lambda i, k: (i, k, 0)),
                   pl.BlockSpec((None, tq, N_KV_HEADS * nsp), lambda i, k: (i, k, 0))],
        out_shape=[jax.ShapeDtypeStruct((b, t, NSA_WIDTH), F32),
                   jax.ShapeDtypeStruct((b, t, N_KV_HEADS * nsp), sel_dtype)],
        compiler_params=_params("arbitrary", "arbitrary"),
        name="cmp_attn",
    )(q_hm, kc, vc, gates, agg)


def _expand_matrix(n_keys, n_blk_pad, tk):
    e = np.zeros((n_keys // tk, n_blk_pad, tk), np.float32)
    key = np.arange(n_keys)
    e[key // tk, key // SLC_BLOCK, key % tk] = 1.0
    return jnp.asarray(e, BF16)


def _softmax_step(qg_ref, k_t, v_t, bias, m_ref, l_ref, acc_ref, tq):
    s = jnp.dot(qg_ref[...], k_t, preferred_element_type=F32)
    tk = s.shape[1]
    if bias is not None:
        s = (s.reshape(GROUP, tq, tk) + bias[None]).reshape(GROUP * tq, tk)
    m_prev = m_ref[...]
    m_new = jnp.maximum(m_prev, jnp.max(s, axis=-1, keepdims=True))
    alpha = jnp.exp(m_prev - m_new)
    p = jnp.exp(s - jnp.tile(m_new, (1, tk // LANES)))
    l_ref[...] = alpha * l_ref[...] + jnp.sum(p, axis=-1, keepdims=True)
    pv = lax.dot_general(p.astype(BF16), v_t, _NT, preferred_element_type=F32)
    acc_ref[...] = alpha[:, :HEAD_DIM] * acc_ref[...] + pv
    m_ref[...] = m_new


def _prompt_attn_kernel(q_ref, kvs_ref, kvw_ref, sel_ref, e_ref, oc_ref, gate_ref, o_ref,
                        qg_ref, m_ref, l_ref, acc_ref):
    tq = q_ref.shape[1]
    tk = kvs_ref.shape[-1]
    nsp = sel_ref.shape[1] // N_KV_HEADS
    qt = pl.program_id(1)
    n_win = WINDOW // tk
    row = lax.broadcasted_iota(jnp.int32, (tq, tk), 0)
    col = lax.broadcasted_iota(jnp.int32, (tq, tk), 1)
    causal = jnp.where(col <= row, 0.0, NEG_INF)
    win_lo = jnp.where(col >= row, 0.0, NEG_INF)
    gates = gate_ref[...]
    oc = oc_ref[...]

    def reset():
        m_ref[...] = jnp.full(m_ref.shape, NEG_INF, F32)
        l_ref[...] = jnp.zeros(l_ref.shape, F32)
        acc_ref[...] = jnp.zeros(acc_ref.shape, F32)

    for g in range(N_KV_HEADS):
        for r in range(GROUP):
            qg_ref[r * tq:(r + 1) * tq, :] = q_ref[GROUP * g + r]
        sel_g = sel_ref[:, g * nsp:(g + 1) * nsp]

        reset()

        def slc_step(kt, extra):
            bias = jnp.dot(sel_g, e_ref[kt], preferred_element_type=F32)
            if extra is not None:
                bias = bias + extra
            _softmax_step(qg_ref, kvs_ref[g, kt], kvs_ref[N_KV_HEADS + g, kt], bias, m_ref, l_ref, acc_ref, tq)

        def slc_body(kt, carry):
            slc_step(kt, None)
            return carry
        lax.fori_loop(0, qt, slc_body, 0)
        slc_step(qt, causal)
        o_s = acc_ref[...] / l_ref[...][:, :HEAD_DIM]

        reset()

        def win_step(kt, bias):
            _softmax_step(qg_ref, kvw_ref[g, kt], kvw_ref[N_KV_HEADS + g, kt], bias, m_ref, l_ref, acc_ref, tq)

        @pl.when(qt >= n_win)
        def _():
            win_step(qt - n_win, win_lo)

        def win_body(kt, carry):
            win_step(kt, None)
            return carry
        lax.fori_loop(jnp.maximum(qt - n_win + 1, 0), qt, win_body, 0)
        win_step(qt, causal)
        o_w = acc_ref[...] / l_ref[...][:, :HEAD_DIM]

        for r in range(GROUP):
            hd = GROUP * g + r
            cols = slice(hd * HEAD_DIM, (hd + 1) * HEAD_DIM)
            rows = slice(r * tq, (r + 1) * tq)
            o_ref[:, cols] = (oc[:, cols] + gates[:, 3 * hd + 1:3 * hd + 2] * o_s[rows]
                              + gates[:, 3 * hd + 2:3 * hd + 3] * o_w[rows]).astype(o_ref.dtype)


def _prompt_attn(q_hm, kvs_b, kvw_b, sel, expand, oc, gates):
    b, _, t, _ = q_hm.shape
    tq = kvs_b.shape[-1]
    tile3 = lambda i, k: (i, k, 0)
    kv_spec = pl.BlockSpec((None,) + kvs_b.shape[1:], lambda i, k: (i, 0, 0, 0, 0))
    return pl.pallas_call(
        _prompt_attn_kernel,
        grid=(b, t // tq),
        in_specs=[
            pl.BlockSpec((None, N_HEADS, tq, HEAD_DIM), lambda i, k: (i, 0, k, 0)),
            kv_spec, kv_spec,
            pl.BlockSpec((None, tq, sel.shape[2]), tile3),
            _const_spec(expand.shape),
            pl.BlockSpec((None, tq, NSA_WIDTH), tile3),
            pl.BlockSpec((None, tq, LANES), tile3),
        ],
        out_specs=pl.BlockSpec((None, tq, NSA_WIDTH), tile3),
        out_shape=jax.ShapeDtypeStruct((b, t, NSA_WIDTH), BF16),
        scratch_shapes=[pltpu.VMEM((GROUP * tq, HEAD_DIM), BF16), pltpu.VMEM((GROUP * tq, LANES), F32),
                        pltpu.VMEM((GROUP * tq, LANES), F32), pltpu.VMEM((GROUP * tq, HEAD_DIM), F32)],
        compiler_params=_params("arbitrary", "arbitrary"),
        name="prompt_attn",
    )(q_hm, kvs_b, kvw_b, sel, expand, oc, gates)


Q_PAD = 8


def _group_q(q_ref, g):
    return jnp.concatenate([q_ref[GROUP * g + r] for r in range(GROUP)], axis=0).astype(BF16)


def _two_piece_attention(qg, k_old, v_old, bias_old, k_new, v_new, bias_new):
    def scores(k_t, bias):
        s = jnp.dot(qg, k_t.astype(BF16), preferred_element_type=F32)
        nk = s.shape[1]
        return (s.reshape(GROUP, Q_PAD, nk) + bias[None]).reshape(GROUP * Q_PAD, nk)
    s_old = scores(k_old, bias_old)
    s_new = scores(k_new, bias_new)
    m = jnp.maximum(jnp.max(s_old, axis=-1, keepdims=True), jnp.max(s_new, axis=-1, keepdims=True))
    p_old = jnp.exp(s_old - m)
    p_new = jnp.exp(s_new - m)
    den = jnp.sum(p_old, axis=-1, keepdims=True) + jnp.sum(p_new, axis=-1, keepdims=True)
    pv = (lax.dot_general(p_old.astype(BF16), v_old.astype(BF16), _NT, preferred_element_type=F32)
          + lax.dot_general(p_new.astype(BF16), v_new.astype(BF16), _NT, preferred_element_type=F32))
    return pv / den


def _add_gated(prev_ref, gate_ref, o_ref, per_g, branch):
    prev, gates = prev_ref[...], gate_ref[...]
    for g in range(N_KV_HEADS):
        for r in range(GROUP):
            hd = GROUP * g + r
            cols = slice(hd * HEAD_DIM, (hd + 1) * HEAD_DIM)
            o_ref[:, cols] = (prev[:, cols] + gates[:, 3 * hd + branch:3 * hd + branch + 1]
                              * per_g[g][r * Q_PAD:(r + 1) * Q_PAD])


def _new_rows_bias(dec_seq):
    q = lax.broadcasted_iota(jnp.int32, (Q_PAD, LANES), 0)
    i = lax.broadcasted_iota(jnp.int32, (Q_PAD, LANES), 1) - (LANES - dec_seq)
    return jnp.where((i >= 0) & (i <= q), 0.0, NEG_INF)


def _sample_slc_kernel(pt_ref, cache_hbm, q_ref, sel_ref, new_ref, e_ref, prev_ref, gate_ref, o_ref,
                       buf, sem, *, dec_seq):
    n_pages = pt_ref.shape[1]

    def copy(page, slot, p):
        return pltpu.make_async_copy(cache_hbm.at[page],
                                     buf.at[slot, :, :, pl.ds(p * PAGE_SIZE, PAGE_SIZE)], sem.at[slot])

    def start_fetch(seq, slot):
        for p in range(n_pages):
            copy(pt_ref[seq, p], slot, p).start()

    def wait_fetch(slot):
        for p in range(n_pages):
            copy(0, slot, p).wait()

    slot = _paged_prologue(start_fetch, wait_fetch)
    nsp = sel_ref.shape[1] // N_KV_HEADS
    n_past_blk = n_pages * (PAGE_SIZE // SLC_BLOCK)
    new_bias = _new_rows_bias(dec_seq)
    per_g = []
    for g in range(N_KV_HEADS):
        sel_g = sel_ref[:, g * nsp:(g + 1) * nsp]
        bias_old = jnp.dot(sel_g.astype(BF16), e_ref[...], preferred_element_type=F32)
        bias_new = new_bias + sel_g[:, n_past_blk:n_past_blk + 1]
        per_g.append(_two_piece_attention(_group_q(q_ref, g), buf[slot, g], buf[slot, N_KV_HEADS + g], bias_old,
                                          new_ref[g], new_ref[N_KV_HEADS + g], bias_new))
    _add_gated(prev_ref, gate_ref, o_ref, per_g, 1)


def _sample_slc(page_table, cache, q_s, sel, new_t, expand, prev, gates, dec_seq):
    db, n_pages = page_table.shape
    past = n_pages * PAGE_SIZE
    seq3 = lambda i, pt: (i, 0, 0)
    seq4 = lambda i, pt: (i, 0, 0, 0)
    return pl.pallas_call(
        functools.partial(_sample_slc_kernel, dec_seq=dec_seq),
        grid_spec=pltpu.PrefetchScalarGridSpec(
            num_scalar_prefetch=1,
            grid=(db,),
            in_specs=[pl.BlockSpec(memory_space=pl.ANY),
                      pl.BlockSpec((None, N_HEADS, Q_PAD, HEAD_DIM), seq4),
                      pl.BlockSpec((None, Q_PAD, sel.shape[2]), seq3),
                      pl.BlockSpec((None, N_KV, HEAD_DIM, LANES), seq4),
                      pl.BlockSpec(expand.shape, lambda i, pt: (0, 0)),
                      pl.BlockSpec((None, Q_PAD, NSA_WIDTH), seq3),
                      pl.BlockSpec((None, Q_PAD, LANES), seq3)],
            out_specs=pl.BlockSpec((None, Q_PAD, NSA_WIDTH), seq3),
            scratch_shapes=[pltpu.VMEM((2, N_KV, HEAD_DIM, past), F32), pltpu.SemaphoreType.DMA((2,))],
        ),
        out_shape=jax.ShapeDtypeStruct((db, Q_PAD, NSA_WIDTH), F32),
        compiler_params=_params("arbitrary"),
        name="sample_slc",
    )(page_table, cache, q_s, sel, new_t, expand, prev, gates)


def _sample_win_kernel(q_ref, st_ref, new_ref, prev_ref, gate_ref, o_ref, st_out_ref, *, dec_seq):
    wb = st_ref.shape[-1]
    q = lax.broadcasted_iota(jnp.int32, (Q_PAD, wb), 0)
    i = lax.broadcasted_iota(jnp.int32, (Q_PAD, wb), 1)
    bias_old = jnp.where(wb + q - i <= WINDOW, 0.0, NEG_INF)
    new_bias = _new_rows_bias(dec_seq)
    per_g = [_two_piece_attention(_group_q(q_ref, g), st_ref[g], st_ref[N_KV_HEADS + g], bias_old,
                                  new_ref[g], new_ref[N_KV_HEADS + g], new_bias)
             for g in range(N_KV_HEADS)]
    _add_gated(prev_ref, gate_ref, o_ref, per_g, 2)
    lane = lax.broadcasted_iota(jnp.int32, (HEAD_DIM, wb), 1)
    for j in range(N_KV):
        shifted = pltpu.roll(st_ref[j], wb - dec_seq, axis=1)
        st_out_ref[j] = jnp.where(lane >= wb - dec_seq, jnp.tile(new_ref[j], (1, wb // LANES)), shifted)


def _sample_win(q_s, st_win, new_t, prev, gates, dec_seq):
    db, _, _, wb = st_win.shape
    seq3 = lambda i: (i, 0, 0)
    seq4 = lambda i: (i, 0, 0, 0)
    return pl.pallas_call(
        functools.partial(_sample_win_kernel, dec_seq=dec_seq),
        grid=(db,),
        in_specs=[pl.BlockSpec((None, N_HEADS, Q_PAD, HEAD_DIM), seq4),
                  pl.BlockSpec((None, N_KV, HEAD_DIM, wb), seq4),
                  pl.BlockSpec((None, N_KV, HEAD_DIM, LANES), seq4),
                  pl.BlockSpec((None, Q_PAD, NSA_WIDTH), seq3),
                  pl.BlockSpec((None, Q_PAD, LANES), seq3)],
        out_specs=[pl.BlockSpec((None, Q_PAD, NSA_WIDTH), seq3), pl.BlockSpec((None, N_KV, HEAD_DIM, wb), seq4)],
        out_shape=[jax.ShapeDtypeStruct((db, Q_PAD, NSA_WIDTH), F32),
                   jax.ShapeDtypeStruct((db, N_KV, HEAD_DIM, wb), F32)],
        compiler_params=_params("arbitrary"),
        name="sample_win",
    )(q_s, st_win, new_t, prev, gates)


POOL_TM = 512


def _pool_project(d_groups, wp_ref, scale_ref, o_ref):
    for gi, d in enumerate(d_groups):
        cols = slice(gi * POOL_GROUP_WIDTH, (gi + 1) * POOL_GROUP_WIDTH)
        y = jnp.dot(d.astype(BF16), wp_ref[gi], preferred_element_type=F32)
        o_ref[:, cols] = (y * scale_ref[:, cols]).astype(o_ref.dtype)


def _pool_prompt_kernel(u_ref, wp_ref, scale_ref, o_ref, ext_ref):
    tm = o_ref.shape[0]
    t0 = pl.multiple_of(pl.program_id(1) * tm, tm)
    halo = u_ref[pl.ds(pl.multiple_of(jnp.maximum(t0 - POOL_HALO, 0), POOL_HALO), POOL_HALO), :]
    ext_ref[0:POOL_HALO, :] = jnp.where(t0 > 0, halo, 0.0)
    ext_ref[POOL_HALO:, :] = u_ref[pl.ds(t0, tm), :]
    pos = t0 + lax.broadcasted_iota(jnp.int32, (tm, POOL_GROUP_WIDTH), 0)
    d_groups = []
    for gi, w in enumerate(POOL_WINDOWS):
        cols = slice(gi * POOL_GROUP_WIDTH, (gi + 1) * POOL_GROUP_WIDTH)
        e = ext_ref[:, cols]
        acc = e
        span = 1
        while span < w:
            acc = acc + pltpu.roll(acc, span, axis=0)
            span *= 2
        cnt = jnp.minimum(w, pos + 1).astype(F32)
        d_groups.append(acc[POOL_HALO:] / cnt - e[POOL_HALO:])
    _pool_project(d_groups, wp_ref, scale_ref, o_ref)


def _pool_prompt(u, wp, scale):
    b, t, pw = u.shape
    tm = min(POOL_TM, t)
    return pl.pallas_call(
        _pool_prompt_kernel,
        grid=(b, t // tm),
        in_specs=[pl.BlockSpec((None, t, pw), lambda i, k: (i, 0, 0)), _const_spec(wp.shape), _const_spec(scale.shape)],
        out_specs=pl.BlockSpec((None, tm, pw), lambda i, k: (i, k, 0)),
        out_shape=jax.ShapeDtypeStruct((b, t, pw), BF16),
        scratch_shapes=[pltpu.VMEM((POOL_HALO + tm, pw), F32)],
        compiler_params=_params("arbitrary", "arbitrary"),
        name="pool_prompt",
    )(u, wp, scale)


def _pool_sample_kernel(ext_ref, wp_ref, scale_ref, o_ref, *, past_len, dec_seq):
    db = ext_ref.shape[1]
    for q in range(dec_seq):
        d_groups = []
        for gi, w in enumerate(POOL_WINDOWS):
            cols = slice(gi * POOL_GROUP_WIDTH, (gi + 1) * POOL_GROUP_WIDTH)
            row = POOL_HALO + q
            acc = ext_ref[row, :, cols]
            for i in range(1, w):
                acc = acc + ext_ref[row - i, :, cols]
            cnt = float(min(w, past_len + q + 1))
            d_groups.append(acc / cnt - ext_ref[row, :, cols])
        _pool_project(d_groups, wp_ref, scale_ref, o_ref.at[pl.ds(q * db, db)])


def _pool_sample(ext, wp, scale, past_len, dec_seq):
    rows, db, pw = ext.shape
    return pl.pallas_call(
        functools.partial(_pool_sample_kernel, past_len=past_len, dec_seq=dec_seq),
        grid=(1,),
        in_specs=[_const_spec(ext.shape), _const_spec(wp.shape), _const_spec(scale.shape)],
        out_specs=_const_spec((dec_seq * db, pw)),
        out_shape=jax.ShapeDtypeStruct((dec_seq * db, pw), BF16),
        compiler_params=_params("arbitrary"),
        name="pool_sample",
    )(ext, wp, scale)


OUTPROJ_TM = 512


def _outproj_kernel(x_ref, o_ref, p_ref, wo_ref, wp_ref, y_ref):
    y_ref[...] = (x_ref[...] + jnp.dot(o_ref[...].astype(BF16), wo_ref[...], preferred_element_type=F32)
                  + jnp.dot(p_ref[...], wp_ref[...], preferred_element_type=F32))


def _outproj(x, o, p, wo, wp):
    m, d = x.shape
    tm = min(OUTPROJ_TM, m)
    row = lambda i: (i, 0)
    return pl.pallas_call(
        _outproj_kernel,
        grid=(m // tm,),
        in_specs=[pl.BlockSpec((tm, d), row), pl.BlockSpec((tm, o.shape[1]), row), pl.BlockSpec((tm, p.shape[1]), row),
                  _const_spec(wo.shape), _const_spec(wp.shape)],
        out_specs=pl.BlockSpec((tm, d), row),
        out_shape=jax.ShapeDtypeStruct((m, d), F32),
        compiler_params=_params("arbitrary"),
        name="outproj",
    )(x, o, p, wo, wp)


def _cmp_weights(w1, w2, pe):
    n_slots = CMP_BLOCK // CMP_STRIDE
    w1s = w1.reshape(n_slots, CMP_FLAT, CMP_HIDDEN)
    w1cat = jnp.concatenate([w1s[h] for h in range(n_slots)], axis=1).astype(BF16)
    pe8 = jnp.pad(pe.reshape(n_slots, CMP_FLAT), ((0, SUBLANES - n_slots), (0, 0)))
    return w1cat, w2.astype(BF16), pe8


def _pad_axis(x, axis, size, front=False):
    pad = [(0, 0)] * x.ndim
    extra = size - x.shape[axis]
    pad[axis] = (extra, 0) if front else (0, extra)
    return jnp.pad(x, pad)


def _rows_to_state(x_fm):
    b, _, _, t = x_fm.shape
    return x_fm.reshape(b, 2, N_KV_HEADS, HEAD_DIM, t).transpose(0, 4, 1, 2, 3)


def _state_to_fm(x):
    b, r = x.shape[:2]
    return x.transpose(0, 2, 3, 4, 1).reshape(b, N_KV, HEAD_DIM, r)


def kernel(x_prompt, x_sample, cache_kv_cmp, cache_kv_slc, page_table, state_kv_win, state_pool, n_ffn1, w_ffn1_gate, w_ffn1_up, w_ffn1_down, n_mix, w_in, w_cmp_k1, w_cmp_k2, pe_cmp_k, w_cmp_v1, w_cmp_v2, pe_cmp_v, w_pool, pool_scale, w_out, n_ffn2, w_ffn2_gate, w_ffn2_up, w_ffn2_down, n_final):
    b, t, d = x_prompt.shape
    db, ds, _ = x_sample.shape
    depth = w_in.shape[0]
    n_pages = page_table.shape[1]
    past = n_pages * PAGE_SIZE
    wb = state_kv_win.shape[2]
    assert t % ATTN_TQ == 0 and t % INPROJ_TM == 0 and WINDOW % ATTN_TQ == 0 and t >= WINDOW
    assert ds <= Q_PAD and ds <= CMP_STRIDE and wb == WINDOW and past % SLC_BLOCK == 0
    assert (t // CMP_STRIDE) % CMP_ROWS == 0 and (db * ds) % SUBLANES == 0

    xp = x_prompt.reshape(b * t, d)
    xs = x_sample.reshape(db * ds, d)
    pos_p = jnp.arange(t, dtype=jnp.int32)
    pos_s = past + jnp.arange(ds, dtype=jnp.int32)
    tab_p = _rope_tables(pos_p)
    tab_s = tuple(jnp.tile(a, (db, 1)) for a in _rope_tables(pos_s)[:3]) + tuple(
        jnp.tile(a, (1, db)) for a in _rope_tables(pos_s)[3:])

    n_chunk_p = t // CMP_STRIDE
    n_slc_p = -(-t // SLC_BLOCK)
    agg_p = _agg_matrix(n_chunk_p, n_chunk_p - 1, n_slc_p, LANES * (-(-n_slc_p // LANES)))
    expand_p = _expand_matrix(t, agg_p.shape[1], ATTN_TQ)
    n_chunk_s = past // CMP_STRIDE
    n_slc_s = -(-(past + ds) // SLC_BLOCK)
    agg_s = _agg_matrix(n_chunk_s, n_chunk_s, n_slc_s, LANES * (-(-n_slc_s // LANES)))
    expand_s = _expand_matrix(past, agg_s.shape[1], past)[0]

    st_p = ([], [], [], [])
    st_s = ([], [], [], [])
    for l in range(depth):
        last = l == depth - 1
        ffn1 = (n_ffn1[l], w_ffn1_gate[l].astype(BF16), w_ffn1_up[l].astype(BF16), w_ffn1_down[l].astype(BF16))
        ffn2 = (n_ffn2[l], w_ffn2_gate[l].astype(BF16), w_ffn2_up[l].astype(BF16), w_ffn2_down[l].astype(BF16))
        w = w_in[l]
        o_kv, o_gate, o_pool = NSA_WIDTH, NSA_WIDTH + 3 * KV_WIDTH, NSA_WIDTH + 3 * KV_WIDTH + N_GATES
        wq = w[:, :o_kv].astype(BF16)
        wkv_t = w[:, o_kv:o_gate].T.astype(BF16)
        wkc = w[:, o_kv:o_kv + KV_WIDTH].astype(BF16)
        wgt = _pad_axis(w[:, o_gate:o_pool], 1, LANES).astype(BF16)
        wu = w[:, o_pool:].astype(BF16)
        proj = (n_mix[l], wq, wkv_t, wkc, wgt, wu)
        wk1, wk2, pek = _cmp_weights(w_cmp_k1[l], w_cmp_k2[l], pe_cmp_k[l])
        wv1, wv2, pev = _cmp_weights(w_cmp_v1[l], w_cmp_v2[l], pe_cmp_v[l])
        pe = jnp.stack([pek, pev])
        wp = w_pool[l].astype(BF16)
        scale = pool_scale[l].reshape(1, POOL_WIDTH)
        wo_nsa = w_out[l][:NSA_WIDTH].astype(BF16)
        wo_pool = w_out[l][NSA_WIDTH:].astype(BF16)

        xp = _ffn(xp, *ffn1)
        q_hm, kvc_rm, kvc, kvs, kvw, kvs_b, kvw_b, gates, u = _inproj(xp, *proj, tab_p, BF16)
        kc, vc = _compress_prompt(kvc_rm.reshape(b, n_chunk_p, CHUNK_WIDTH), pe, wk1, wv1, wk2, wv2)
        gates3 = gates.reshape(b, t, LANES)
        oc, sel = _cmp_attn(q_hm, kc, vc, gates3, agg_p, tq=ATTN_TQ, pos0=0, tiled=True,
                            n_slc=n_slc_p, sel_dtype=BF16)
        o_mix = _prompt_attn(q_hm, kvs_b, kvw_b, sel, expand_p, oc, gates3)
        u3 = u.reshape(b, t, POOL_WIDTH)
        pool_out = _pool_prompt(u3, wp, scale)
        xp = _outproj(xp, o_mix.reshape(b * t, NSA_WIDTH), pool_out.reshape(b * t, POOL_WIDTH), wo_nsa, wo_pool)
        xp = _ffn(xp, *ffn2, g_final=n_final if last else None)
        st_p[0].append(_rows_to_state(kvc))
        st_p[1].append(_rows_to_state(kvs))
        st_p[2].append(_rows_to_state(kvw[..., t - min(WINDOW, t):]))
        st_p[3].append(u3[:, t - POOL_STATE:])

        xs = _ffn(xs, *ffn1)
        q, _, kvc, kvs, kvw, _, _, gates, u = _inproj(xs, *proj, tab_s, F32)
        per_seq = lambda a: a.reshape(N_KV, HEAD_DIM, db, ds).transpose(2, 0, 1, 3)
        kvc_n, kvs_n, kvw_n = per_seq(kvc), per_seq(kvs), per_seq(kvw)
        ynew = _pad_axis(kvc_n.transpose(0, 1, 3, 2), 2, CMP_STRIDE).reshape(db, N_KV, CMP_FLAT)
        cache_c = _state_to_fm(cache_kv_cmp[l])
        kc, vc = _compress_sample(page_table, cache_c, _pad_axis(ynew, 1, SUBLANES), pe, wk1, wv1, wk2, wv2)
        q_s = _pad_axis(q.reshape(N_HEADS, db, ds, HEAD_DIM).transpose(1, 0, 2, 3), 2, Q_PAD)
        gates_s = _pad_axis(gates.reshape(db, ds, LANES), 1, Q_PAD)
        oc, sel = _cmp_attn(q_s, kc, vc, gates_s, agg_s, tq=Q_PAD, pos0=past, tiled=False,
                            n_slc=n_slc_s, sel_dtype=F32)
        o_cs = _sample_slc(page_table, _state_to_fm(cache_kv_slc[l]), q_s, sel,
                           _pad_axis(kvs_n, 3, LANES, front=True), expand_s, oc, gates_s, ds)
        o_mix, st_win_new = _sample_win(q_s, _state_to_fm(state_kv_win[l]),
                                        _pad_axis(kvw_n, 3, LANES, front=True), o_cs, gates_s, ds)
        u3 = u.reshape(db, ds, POOL_WIDTH)
        ext = jnp.concatenate([jnp.zeros((db, POOL_HALO - POOL_STATE, POOL_WIDTH), F32), state_pool[l], u3], axis=1)
        pool_out = _pool_sample(ext.transpose(1, 0, 2), wp, scale, past, ds)
        pool_out = pool_out.reshape(ds, db, POOL_WIDTH).transpose(1, 0, 2).reshape(db * ds, POOL_WIDTH)
        xs = _outproj(xs, o_mix[:, :ds].reshape(db * ds, NSA_WIDTH), pool_out, wo_nsa, wo_pool)
        xs = _ffn(xs, *ffn2, g_final=n_final if last else None)
        st_s[0].append(_rows_to_state(kvc_n))
        st_s[1].append(_rows_to_state(kvs_n))
        st_s[2].append(_rows_to_state(st_win_new))
        st_s[3].append(jnp.concatenate([state_pool[l], u3], axis=1)[:, ds:])

    return (xp.reshape(b, t, d), xs.reshape(db, ds, d),
            jnp.stack(st_p[0]), jnp.stack(st_p[1]), jnp.stack(st_p[2]), jnp.stack(st_p[3]),
            jnp.stack(st_s[0]), jnp.stack(st_s[1]), jnp.stack(st_s[2]), jnp.stack(st_s[3]))
```

```python
import functools

import numpy as np
import jax
import jax.numpy as jnp
from jax import lax
from jax.experimental import pallas as pl
from jax.experimental.pallas import tpu as pltpu

F32 = jnp.float32
BF16 = jnp.bfloat16

HEAD_DIM = 64
N_HEADS = 8
N_KV_HEADS = 2
GROUP = N_HEADS // N_KV_HEADS
NSA_WIDTH = N_HEADS * HEAD_DIM
N_KV = 2 * N_KV_HEADS
KV_WIDTH = N_KV * HEAD_DIM
ROPE_DIM = HEAD_DIM // 4
ROPE_HALF = ROPE_DIM // 2
ROPE_THETA = 500000.0
CMP_BLOCK = 32
CMP_STRIDE = 16
CMP_HIDDEN = 256
CMP_FLAT = CMP_STRIDE * HEAD_DIM
SLC_BLOCK = 64
TOPK_BLOCKS = 16
WINDOW = 512
PAGE_SIZE = 128
POOL_WINDOWS = (2, 4, 8, 16)
POOL_GROUP_WIDTH = 128
POOL_WIDTH = POOL_GROUP_WIDTH * len(POOL_WINDOWS)
POOL_STATE = max(POOL_WINDOWS) - 1
POOL_HALO = 16
RMS_EPS = 1e-6
FORCE_SCORE = 1e4
NEG_INF = -1e30
ATTN_SCALE = HEAD_DIM ** -0.5
LOG2_E = float(np.log2(np.e))
N_GATES = 3 * N_HEADS
CHUNK_WIDTH = CMP_STRIDE * KV_WIDTH

LANES = 128
SUBLANES = 8
VMEM_LIMIT_BYTES = 56 * 1024 * 1024

_NT = (((1,), (1,)), ((), ()))


def _params(*sem):
    return pltpu.CompilerParams(dimension_semantics=sem, vmem_limit_bytes=VMEM_LIMIT_BYTES)


def _rmsnorm(x, g):
    return x * lax.rsqrt(jnp.mean(x * x, axis=-1, keepdims=True) + RMS_EPS) * g


def _const_spec(shape):
    n = len(shape)
    return pl.BlockSpec(shape, lambda *_: (0,) * n)


FFN_TM = 512
FFN_CHUNK = 256


def _ffn_kernel(x_ref, g_ref, wg_ref, wu_ref, wd_ref, gf_ref, o_ref, act_ref, *, final_norm):
    x = x_ref[...]
    h = _rmsnorm(x, g_ref[...]).astype(BF16)
    d_ff = wg_ref.shape[1]
    for c in range(d_ff // FFN_CHUNK):
        sl = slice(c * FFN_CHUNK, (c + 1) * FFN_CHUNK)
        a = jnp.dot(h, wg_ref[:, sl], preferred_element_type=F32)
        u = jnp.dot(h, wu_ref[:, sl], preferred_element_type=F32)
        act_ref[:, sl] = (a * jax.nn.sigmoid(a) * u).astype(BF16)
    y = x + 0.5 * jnp.dot(act_ref[...], wd_ref[...], preferred_element_type=F32)
    if final_norm:
        y = _rmsnorm(y, gf_ref[...])
    o_ref[...] = y


def _ffn(x, g, wg, wu, wd, g_final=None):
    m, d = x.shape
    d_ff = wg.shape[1]
    tm = min(FFN_TM, m)
    final_norm = g_final is not None
    gf = g_final if final_norm else g
    return pl.pallas_call(
        functools.partial(_ffn_kernel, final_norm=final_norm),
        grid=(m // tm,),
        in_specs=[
            pl.BlockSpec((tm, d), lambda i: (i, 0)),
            _const_spec((1, d)),
            pl.BlockSpec((d, d_ff), lambda i: (0, 0), pipeline_mode=pl.Buffered(1)),
            pl.BlockSpec((d, d_ff), lambda i: (0, 0), pipeline_mode=pl.Buffered(1)),
            pl.BlockSpec((d_ff, d), lambda i: (0, 0), pipeline_mode=pl.Buffered(1)),
            _const_spec((1, d)),
        ],
        out_specs=pl.BlockSpec((tm, d), lambda i: (i, 0)),
        out_shape=jax.ShapeDtypeStruct((m, d), F32),
        scratch_shapes=[pltpu.VMEM((tm, d_ff), BF16)],
        compiler_params=_params("arbitrary"),
        name="ffn",
    )(x, g.reshape(1, d), wg, wu, wd, gf.reshape(1, d))


INPROJ_TM = 512
ATTN_TQ = 256


def _rope_tables(pos):
    inv = jnp.power(ROPE_THETA, -jnp.arange(ROPE_HALF, dtype=F32) / ROPE_HALF)
    ang = pos.astype(F32)[:, None] * inv[None, :]
    cos, sin = jnp.cos(ang), jnp.sin(ang)
    t = pos.shape[0]
    rest = HEAD_DIM - ROPE_DIM
    z_half = jnp.zeros((t, ROPE_HALF), F32)
    z_rest = jnp.zeros((t, rest), F32)
    cos_t = jnp.concatenate([cos, cos, jnp.ones((t, rest), F32)], axis=1)
    sin_a = jnp.concatenate([-sin, z_half, z_rest], axis=1)
    sin_b = jnp.concatenate([z_half, sin, z_rest], axis=1)
    rep = LANES // HEAD_DIM
    return tuple(jnp.tile(a, (1, rep)) for a in (cos_t, sin_a, sin_b)) + (cos.T, sin.T)


def _rope(z, cos_t, sin_a, sin_b):
    outs = []
    for c in range(z.shape[1] // LANES):
        zc = z[:, c * LANES:(c + 1) * LANES]
        outs.append(zc * cos_t + pltpu.roll(zc, LANES - ROPE_HALF, axis=1) * sin_a
                    + pltpu.roll(zc, ROPE_HALF, axis=1) * sin_b)
    return outs[0] if len(outs) == 1 else jnp.concatenate(outs, axis=1)


def _rope_fm(kt, cos, sin):
    x1, x2 = kt[0:ROPE_HALF], kt[ROPE_HALF:ROPE_DIM]
    return jnp.concatenate([x1 * cos - x2 * sin, x2 * cos + x1 * sin, kt[ROPE_DIM:]], axis=0)


def _inproj_kernel(x_ref, g_ref, wq_ref, wkv_t_ref, wkc_ref, wgt_ref, wu_ref,
                   cos_ref, sa_ref, sb_ref, cos_fm_ref, sin_fm_ref,
                   q_ref, q2_ref, kvc_rm_ref, kvc_ref, kvs_ref, kvw_ref, kvs_b_ref, kvw_b_ref, gate_ref, u_ref):
    h = _rmsnorm(x_ref[...], g_ref[...]).astype(BF16)
    tm = h.shape[0]
    tk = kvs_b_ref.shape[-1]
    cos_t, sin_a, sin_b = cos_ref[...], sa_ref[...], sb_ref[...]
    cos_fm, sin_fm = cos_fm_ref[...], sin_fm_ref[...]
    q = _rope(jnp.dot(h, wq_ref[...], preferred_element_type=F32), cos_t, sin_a, sin_b) * ATTN_SCALE
    q2 = q * LOG2_E
    for hd in range(N_HEADS):
        q_ref[hd] = q[:, hd * HEAD_DIM:(hd + 1) * HEAD_DIM].astype(q_ref.dtype)
        q2_ref[hd] = q2[:, hd * HEAD_DIM:(hd + 1) * HEAD_DIM].astype(BF16)
    k_width = N_KV_HEADS * HEAD_DIM
    kvc = jnp.dot(h, wkc_ref[...], preferred_element_type=F32)
    kvc_rm_ref[...] = jnp.concatenate([_rope(kvc[:, :k_width], cos_t, sin_a, sin_b), kvc[:, k_width:]], axis=1)
    kv_t = lax.dot_general(wkv_t_ref[...], h, _NT, preferred_element_type=F32)
    for i, (f_ref, b_ref) in enumerate(((kvc_ref, None), (kvs_ref, kvs_b_ref), (kvw_ref, kvw_b_ref))):
        for j in range(N_KV):
            r0 = i * KV_WIDTH + j * HEAD_DIM
            blk = kv_t[r0:r0 + HEAD_DIM]
            if j < N_KV_HEADS:
                blk = _rope_fm(blk, cos_fm, sin_fm)
            f_ref[j] = blk
            if b_ref is not None:
                for c in range(tm // tk):
                    b_ref[j, c] = blk[:, c * tk:(c + 1) * tk].astype(BF16)
    gate_ref[...] = jax.nn.sigmoid(jnp.dot(h, wgt_ref[...], preferred_element_type=F32))
    u_ref[...] = jnp.dot(h, wu_ref[...], preferred_element_type=F32)


def _inproj(x, g, wq, wkv_t, wkc, wgt, wu, tables, q_dtype):
    m, d = x.shape
    seq = tables[0].shape[0]
    nseq = m // seq
    tm = min(INPROJ_TM, seq)
    tk = min(ATTN_TQ, tm)
    n_tab = seq // tm
    row = lambda i: (i, 0)
    tab = lambda i: (i % n_tab, 0)
    tab_fm = lambda i: (0, i % n_tab)
    hm = lambda i: (i // n_tab, 0, i % n_tab, 0)
    fm = lambda i: (i // n_tab, 0, 0, i % n_tab)
    fmb = lambda i: (i // n_tab, 0, i % n_tab, 0, 0)
    fm_shape = jax.ShapeDtypeStruct((nseq, N_KV, HEAD_DIM, seq), F32)
    fmb_shape = jax.ShapeDtypeStruct((nseq, N_KV, seq // tk, HEAD_DIM, tk), BF16)
    fm_spec = pl.BlockSpec((None, N_KV, HEAD_DIM, tm), fm)
    fmb_spec = pl.BlockSpec((None, N_KV, tm // tk, HEAD_DIM, tk), fmb)
    return pl.pallas_call(
        _inproj_kernel,
        grid=(m // tm,),
        in_specs=[
            pl.BlockSpec((tm, d), row), _const_spec((1, d)),
            _const_spec(wq.shape), _const_spec(wkv_t.shape), _const_spec(wkc.shape),
            _const_spec(wgt.shape), _const_spec(wu.shape),
            pl.BlockSpec((tm, LANES), tab), pl.BlockSpec((tm, LANES), tab), pl.BlockSpec((tm, LANES), tab),
            pl.BlockSpec((ROPE_HALF, tm), tab_fm), pl.BlockSpec((ROPE_HALF, tm), tab_fm),
        ],
        out_specs=[
            pl.BlockSpec((None, N_HEADS, tm, HEAD_DIM), hm), pl.BlockSpec((None, N_HEADS, tm, HEAD_DIM), hm),
            pl.BlockSpec((tm, KV_WIDTH), row),
            fm_spec, fm_spec, fm_spec, fmb_spec, fmb_spec,
            pl.BlockSpec((tm, LANES), row), pl.BlockSpec((tm, POOL_WIDTH), row),
        ],
        out_shape=[
            jax.ShapeDtypeStruct((nseq, N_HEADS, seq, HEAD_DIM), q_dtype),
            jax.ShapeDtypeStruct((nseq, N_HEADS, seq, HEAD_DIM), BF16),
            jax.ShapeDtypeStruct((m, KV_WIDTH), F32),
            fm_shape, fm_shape, fm_shape, fmb_shape, fmb_shape,
            jax.ShapeDtypeStruct((m, LANES), F32), jax.ShapeDtypeStruct((m, POOL_WIDTH), F32),
        ],
        compiler_params=_params("arbitrary"),
        name="inproj",
    )(x, g.reshape(1, d), wq, wkv_t, wkc, wgt, wu, *tables)


CMP_ROWS = 128


def _gelu_tanh(x):
    return 0.5 * x * (1.0 + jnp.tanh(np.sqrt(2.0 / np.pi).astype(np.float32) * (x + 0.044715 * (x * x * x))))


def _interleave_heads(a0, a1):
    low = lax.broadcasted_iota(jnp.int32, a0.shape, 1) < HEAD_DIM
    return (jnp.where(low, a0, pltpu.roll(a1, HEAD_DIM, axis=1)),
            jnp.where(low, pltpu.roll(a0, HEAD_DIM, axis=1), a1))


def _pick_row(blk, r):
    row8 = lax.broadcasted_iota(jnp.int32, (SUBLANES, 1), 0)
    return jnp.sum(jnp.where(row8 == r, blk, 0.0), axis=0, keepdims=True)


def _compress_finish(kv, n, new8, pe_ref, w1_ref, w2_ref, out_ref, y_ref, has_new):
    y_ref[pl.ds(2 * n, 2 * SUBLANES), :] = jnp.concatenate([new8, pe_ref[kv]], axis=0).astype(BF16)
    p = jnp.dot(y_ref[...], w1_ref[...], preferred_element_type=F32)
    p_new, p_pe = p[2 * n:2 * n + SUBLANES], p[2 * n + SUBLANES:2 * n + 2 * SUBLANES]
    bias = _pick_row(p_pe[:, :CMP_HIDDEN], 0) + _pick_row(p_pe[:, CMP_HIDDEN:], 1)
    last = lax.broadcasted_iota(jnp.int32, (n, CMP_HIDDEN), 0) == n - 1
    for g in range(N_KV_HEADS):
        slot0 = p[g * n:(g + 1) * n, :CMP_HIDDEN]
        slot1 = pltpu.roll(p[g * n:(g + 1) * n, CMP_HIDDEN:], n - 1, axis=0)
        if has_new:
            slot1 = jnp.where(last, _pick_row(p_new[:, CMP_HIDDEN:], g), slot1)
        hid = _gelu_tanh(slot0 + slot1 + bias).astype(BF16)
        out_ref[g] = jnp.dot(hid, w2_ref[...], preferred_element_type=F32).astype(BF16)


def _compress_prompt_kernel(x_ref, pe_ref, wk1_ref, wv1_ref, wk2_ref, wv2_ref, kc_ref, vc_ref, y_ref):
    n = x_ref.shape[0]
    for kv, (w1_ref, w2_ref, out_ref) in enumerate(((wk1_ref, wk2_ref, kc_ref), (wv1_ref, wv2_ref, vc_ref))):
        def fill(r, carry):
            r0 = pl.multiple_of(r * CMP_ROWS, CMP_ROWS)
            for j in range(CMP_STRIDE // 2):
                c0 = (2 * j) * KV_WIDTH + kv * LANES
                c1 = (2 * j + 1) * KV_WIDTH + kv * LANES
                y0, y1 = _interleave_heads(x_ref[pl.ds(r0, CMP_ROWS), c0:c0 + LANES],
                                           x_ref[pl.ds(r0, CMP_ROWS), c1:c1 + LANES])
                y_ref[pl.ds(r0, CMP_ROWS), j * LANES:(j + 1) * LANES] = y0.astype(BF16)
                y_ref[pl.ds(n + r0, CMP_ROWS), j * LANES:(j + 1) * LANES] = y1.astype(BF16)
            return carry
        lax.fori_loop(0, n // CMP_ROWS, fill, 0)
        _compress_finish(kv, n, jnp.zeros((SUBLANES, CMP_FLAT), F32), pe_ref, w1_ref, w2_ref, out_ref, y_ref, False)


def _compress_prompt(x, pe, wk1, wv1, wk2, wv2):
    b, n, cw = x.shape
    out = jax.ShapeDtypeStruct((b, N_KV_HEADS, n, HEAD_DIM), BF16)
    out_spec = pl.BlockSpec((None, N_KV_HEADS, n, HEAD_DIM), lambda i: (i, 0, 0, 0))
    return pl.pallas_call(
        _compress_prompt_kernel,
        grid=(b,),
        in_specs=[pl.BlockSpec((None, n, cw), lambda i: (i, 0, 0)), _const_spec(pe.shape),
                  _const_spec(wk1.shape), _const_spec(wv1.shape), _const_spec(wk2.shape), _const_spec(wv2.shape)],
        out_specs=[out_spec, out_spec],
        out_shape=[out, out],
        scratch_shapes=[pltpu.VMEM((2 * n + 2 * SUBLANES, CMP_FLAT), BF16)],
        compiler_params=_params("arbitrary"),
        name="compress_prompt",
    )(x, pe, wk1, wv1, wk2, wv2)


def _paged_prologue(start_fetch, wait_fetch):
    b = pl.program_id(0)
    slot = lax.rem(b, 2)

    @pl.when(b == 0)
    def _():
        start_fetch(0, 0)

    @pl.when(b + 1 < pl.num_programs(0))
    def _():
        start_fetch(b + 1, 1 - slot)

    wait_fetch(slot)
    return slot


CMP_PAGE_GROUP = 8


def _dechunk_perm():
    perm = np.zeros((N_KV_HEADS * PAGE_SIZE, 2 * LANES), np.float32)
    cpp = PAGE_SIZE // CMP_STRIDE
    for g in range(N_KV_HEADS):
        for c in range(cpp):
            for j in range(CMP_STRIDE // 2):
                for par in range(2):
                    perm[g * PAGE_SIZE + CMP_STRIDE * c + 2 * j + par, par * LANES + g * HEAD_DIM + j * cpp + c] = 1.0
    return jnp.asarray(perm, BF16)


def _compress_sample_kernel(pt_ref, cache_hbm, ynew_ref, pe_ref, perm_ref, wk1_ref, wv1_ref, wk2_ref, wv2_ref,
                            kc_ref, vc_ref, buf, sem, y_ref):
    n_pages = pt_ref.shape[1]
    n = n_pages * (PAGE_SIZE // CMP_STRIDE)

    def copy(page, slot, p):
        return pltpu.make_async_copy(cache_hbm.at[page], buf.at[slot, p], sem.at[slot])

    def start_fetch(seq, slot):
        def body(p, carry):
            copy(pt_ref[seq, p], slot, p).start()
            return carry
        lax.fori_loop(0, n_pages, body, 0)

    def wait_fetch(slot):
        def body(p, carry):
            copy(0, slot, p).wait()
            return carry
        lax.fori_loop(0, n_pages, body, 0)

    slot = _paged_prologue(start_fetch, wait_fetch)
    ynew = ynew_ref[...]
    row8 = lax.broadcasted_iota(jnp.int32, (SUBLANES, 1), 0)
    pg = CMP_PAGE_GROUP
    cpp = PAGE_SIZE // CMP_STRIDE
    for kv, (w1_ref, w2_ref, out_ref) in enumerate(((wk1_ref, wk2_ref, kc_ref), (wv1_ref, wv2_ref, vc_ref))):
        for grp in range(n_pages // pg):
            lhs = jnp.concatenate(
                [jnp.concatenate([buf[slot, grp * pg + q, N_KV_HEADS * kv + g] for g in range(N_KV_HEADS)], axis=1)
                 for q in range(pg)], axis=0).astype(BF16)
            out = jnp.dot(lhs, perm_ref[...], preferred_element_type=F32)
            x = jnp.concatenate(
                [jnp.concatenate([out[q * HEAD_DIM:(q + 1) * HEAD_DIM, :LANES],
                                  out[q * HEAD_DIM:(q + 1) * HEAD_DIM, LANES:]], axis=0) for q in range(pg)], axis=1)
            xt = x.T
            r0 = grp * (pg * cpp)
            for g in range(N_KV_HEADS):
                for j in range(CMP_STRIDE // 2):
                    piece = jnp.concatenate(
                        [xt[q * LANES + g * HEAD_DIM + j * cpp:q * LANES + g * HEAD_DIM + (j + 1) * cpp]
                         for q in range(pg)], axis=0)
                    y_ref[g * n + r0:g * n + r0 + pg * cpp, j * LANES:(j + 1) * LANES] = piece.astype(BF16)
        new8 = jnp.where(row8 == 0, _pick_row(ynew, N_KV_HEADS * kv),
                         jnp.where(row8 == 1, _pick_row(ynew, N_KV_HEADS * kv + 1), 0.0))
        _compress_finish(kv, n, new8, pe_ref, w1_ref, w2_ref, out_ref, y_ref, True)


def _compress_sample(page_table, cache, ynew, pe, wk1, wv1, wk2, wv2):
    db, n_pages = page_table.shape
    n = n_pages * (PAGE_SIZE // CMP_STRIDE)
    out = jax.ShapeDtypeStruct((db, N_KV_HEADS, n, HEAD_DIM), BF16)
    out_spec = pl.BlockSpec((None, N_KV_HEADS, n, HEAD_DIM), lambda i, pt: (i, 0, 0, 0))
    const = lambda s: pl.BlockSpec(s, lambda i, pt: (0,) * len(s))
    perm = _dechunk_perm()
    return pl.pallas_call(
        _compress_sample_kernel,
        grid_spec=pltpu.PrefetchScalarGridSpec(
            num_scalar_prefetch=1,
            grid=(db,),
            in_specs=[pl.BlockSpec(memory_space=pl.ANY),
                      pl.BlockSpec((None, SUBLANES, CMP_FLAT), lambda i, pt: (i, 0, 0)),
                      const(pe.shape), const(perm.shape),
                      const(wk1.shape), const(wv1.shape), const(wk2.shape), const(wv2.shape)],
            out_specs=[out_spec, out_spec],
            scratch_shapes=[pltpu.VMEM((2, n_pages, N_KV, HEAD_DIM, PAGE_SIZE), F32),
                            pltpu.SemaphoreType.DMA((2,)),
                            pltpu.VMEM((2 * n + 2 * SUBLANES, CMP_FLAT), BF16)],
        ),
        out_shape=[out, out],
        compiler_params=_params("arbitrary"),
        name="compress_sample",
    )(page_table, cache, ynew, pe, perm, wk1, wv1, wk2, wv2)


def _select_mask(p_slc, t_pos, j, n_slc, blk_axis):
    cur = t_pos // SLC_BLOCK
    forced = (j == 0) | (j == cur) | (j == cur - 1)
    real = j < n_slc
    score = jnp.where(forced, FORCE_SCORE, jnp.where(j <= cur, p_slc, NEG_INF))
    score = jnp.where(real, score, -jnp.inf)
    rank = jnp.zeros(score.shape, jnp.int32)
    for jp in range(n_slc):
        c = score[jp:jp + 1, :] if blk_axis == 0 else score[:, jp:jp + 1]
        rank = rank + ((c > score) | ((c == score) & (j > jp))).astype(jnp.int32)
    return jnp.where((rank < TOPK_BLOCKS) & real, 0.0, NEG_INF)


def _cmp_attn_kernel(q_ref, kc_ref, vc_ref, gate_ref, agg_ref, oc_ref, sel_ref, *, pos0, tiled, n_slc, blocks_major):
    tq = q_ref.shape[1]
    n = kc_ref.shape[1]
    nsp = agg_ref.shape[0] if blocks_major else agg_ref.shape[1]
    t0 = pos0 + (pl.program_id(1) * tq if tiled else 0)
    t_n = t0 + lax.broadcasted_iota(jnp.int32, (tq, n), 0)
    blk_end = lax.broadcasted_iota(jnp.int32, (tq, n), 1) * CMP_STRIDE + (CMP_BLOCK - 1)
    valid = (blk_end <= t_n)[None]
    if blocks_major:
        n_rows = SUBLANES * (-(-n_slc // SUBLANES))
        t_s = t0 + lax.broadcasted_iota(jnp.int32, (n_rows, tq), 1)
        j = lax.broadcasted_iota(jnp.int32, (n_rows, tq), 0)
    else:
        t_s = t0 + lax.broadcasted_iota(jnp.int32, (tq, nsp), 0)
        j = lax.broadcasted_iota(jnp.int32, (tq, nsp), 1)
    gates = gate_ref[...]
    for g in range(N_KV_HEADS):
        qg = jnp.concatenate([q_ref[GROUP * g + r].astype(F32) for r in range(GROUP)], axis=0).astype(BF16)
        s = lax.dot_general(qg, kc_ref[g], _NT, preferred_element_type=F32).reshape(GROUP, tq, n)
        s = jnp.where(valid, s, NEG_INF)
        e = jnp.where(valid, jnp.exp(s - jnp.max(s, axis=-1, keepdims=True)), 0.0)
        p = e / jnp.maximum(jnp.sum(e, axis=-1, keepdims=True), 1e-30)
        o = jnp.dot(p.reshape(GROUP * tq, n).astype(BF16), vc_ref[g], preferred_element_type=F32)
        p_grp = jnp.sum(p, axis=0)
        p_hi = p_grp.astype(BF16)
        p_lo = (p_grp - p_hi.astype(F32)).astype(BF16)
        if blocks_major:
            p_slc = (lax.dot_general(agg_ref[...], p_hi, _NT, preferred_element_type=F32)
                     + lax.dot_general(agg_ref[...], p_lo, _NT, preferred_element_type=F32))
            mask = _select_mask(p_slc[:n_rows], t_s, j, n_slc, 0)
            if n_rows < nsp:
                mask = jnp.concatenate([mask, jnp.zeros((nsp - n_rows, tq), F32)], axis=0)
            mask = mask.T
        else:
            p_slc = (jnp.dot(p_hi, agg_ref[...], preferred_element_type=F32)
                     + jnp.dot(p_lo, agg_ref[...], preferred_element_type=F32))
            mask = _select_mask(p_slc, t_s, j, n_slc, 1)
        sel_ref[:, g * nsp:(g + 1) * nsp] = mask.astype(sel_ref.dtype)
        for r in range(GROUP):
            hd = GROUP * g + r
            oc_ref[:, hd * HEAD_DIM:(hd + 1) * HEAD_DIM] = gates[:, 3 * hd:3 * hd + 1] * o[r * tq:(r + 1) * tq]


def _agg_matrix(n_cmp_pad, n_cmp, n_slc, n_slc_pad):
    c0 = np.arange(n_cmp)[:, None] * CMP_STRIDE
    s0 = np.arange(n_slc)[None, :] * SLC_BLOCK
    overlap = np.clip(np.minimum(c0 + CMP_BLOCK, s0 + SLC_BLOCK) - np.maximum(c0, s0), 0, None)
    agg = np.zeros((n_cmp_pad, n_slc_pad), np.float32)
    agg[:n_cmp, :n_slc] = overlap / CMP_BLOCK
    return jnp.asarray(agg, BF16)


def _cmp_attn(q_hm, kc, vc, gates, agg, *, tq, pos0, tiled, n_slc, sel_dtype):
    b, _, t, _ = q_hm.shape
    n = kc.shape[2]
    nsp = agg.shape[1]
    blocks_major = tq % LANES == 0
    if blocks_major:
        agg = agg.T
    return pl.pallas_call(
        functools.partial(_cmp_attn_kernel, pos0=pos0, tiled=tiled, n_slc=n_slc, blocks_major=blocks_major),
        grid=(b, t // tq),
        in_specs=[
            pl.BlockSpec((None, N_HEADS, tq, HEAD_DIM), lambda i, k: (i, 0, k, 0)),
            pl.BlockSpec((None, N_KV_HEADS, n, HEAD_DIM), lambda i, k: (i, 0, 0, 0)),
            pl.BlockSpec((None, N_KV_HEADS, n, HEAD_DIM), lambda i, k: (i, 0, 0, 0)),
            pl.BlockSpec((None, tq, LANES), lambda i, k: (i, k, 0)),
            _const_spec(agg.shape),
        ],
        out_specs=[pl.BlockSpec((None, tq, NSA_WIDTH), lambda i, k: (i, k, 0)),
                   pl.BlockSpec((None, tq, N_KV_HEADS * nsp), lambda i, k: (i, k, 0))],
        out_shape=[jax.ShapeDtypeStruct((b, t, NSA_WIDTH), F32),
                   jax.ShapeDtypeStruct((b, t, N_KV_HEADS * nsp), sel_dtype)],
        compiler_params=_params("arbitrary", "arbitrary"),
        name="cmp_attn",
    )(q_hm, kc, vc, gates, agg)


def _expand_matrix(n_keys, n_blk_pad, tk):
    e = np.zeros((n_keys // tk, n_blk_pad, tk), np.float32)
    key = np.arange(n_keys)
    e[key // tk, key // SLC_BLOCK, key % tk] = 1.0
    return jnp.asarray(e, BF16)


def _col_blocks(x, op):
    out = x[:, :LANES]
    for c in range(1, x.shape[1] // LANES):
        out = op(out, x[:, c * LANES:(c + 1) * LANES])
    return out


def _prompt_attn_kernel(q_ref, kvs_ref, kvw_ref, sel_ref, e_ref, oc_ref, gate_ref, o_ref,
                        qx_ref, qg_ref, s_ref, m_ref, l_ref, acc_ref):
    tq = q_ref.shape[1]
    tk = kvs_ref.shape[-1]
    nsp = sel_ref.shape[1] // N_KV_HEADS
    rows = GROUP * tq
    qt = pl.program_id(1)
    n_win = WINDOW // tk
    row = lax.broadcasted_iota(jnp.int32, (tq, tk), 0)
    col = lax.broadcasted_iota(jnp.int32, (tq, tk), 1)
    causal = jnp.where(col <= row, 0.0, NEG_INF)
    win_lo = jnp.where(col >= row, 0.0, NEG_INF)
    gates = gate_ref[...]
    oc = oc_ref[...]
    k_pad = jnp.zeros((qx_ref.shape[1] - nsp - HEAD_DIM, tk), BF16)
    qx_ref[:, nsp + HEAD_DIM:] = jnp.zeros((rows, qx_ref.shape[1] - nsp - HEAD_DIM), BF16)

    def scores_slc(g, kt):
        rhs = jnp.concatenate([e_ref[kt], kvs_ref[g, kt], k_pad], axis=0)
        return jnp.dot(qx_ref[...], rhs, preferred_element_type=F32)

    def scores_win(g, kt):
        return jnp.dot(qg_ref[...], kvw_ref[g, kt], preferred_element_type=F32)

    def branch(scores, v_ref, g, first_kt, first_bias, lo):
        def find_max(kt, bias):
            s = scores(g, kt)
            if bias is not None:
                s = (s.reshape(GROUP, tq, tk) + bias[None]).reshape(rows, tk)
            s_ref[kt] = s
            m_ref[...] = jnp.maximum(m_ref[...], _col_blocks(s, jnp.maximum))

        def accumulate(kt):
            p = jnp.exp2(s_ref[kt] - jnp.tile(m_ref[...], (1, tk // LANES)))
            l_ref[...] += _col_blocks(p, jnp.add)
            acc_ref[...] += lax.dot_general(p.astype(BF16), v_ref[N_KV_HEADS + g, kt], _NT,
                                            preferred_element_type=F32)

        def over_tiles(first_fn, mid_fn, last_fn):
            if first_bias is not None:
                @pl.when(qt >= n_win)
                def _():
                    first_fn()

            def body(kt, carry):
                mid_fn(kt)
                return carry
            lax.fori_loop(lo, qt, body, 0)
            last_fn()

        m_ref[...] = jnp.full(m_ref.shape, NEG_INF, F32)
        over_tiles(lambda: find_max(first_kt, first_bias), lambda kt: find_max(kt, None),
                   lambda: find_max(qt, causal))
        m_ref[...] = jnp.broadcast_to(jnp.max(m_ref[...], axis=-1, keepdims=True), m_ref.shape)
        l_ref[...] = jnp.zeros(l_ref.shape, F32)
        acc_ref[...] = jnp.zeros(acc_ref.shape, F32)
        over_tiles(lambda: accumulate(first_kt), accumulate, lambda: accumulate(qt))
        return acc_ref[...] / jnp.sum(l_ref[...], axis=-1, keepdims=True)

    for g in range(N_KV_HEADS):
        sel_g = sel_ref[:, g * nsp:(g + 1) * nsp]
        for r in range(GROUP):
            qx_ref[r * tq:(r + 1) * tq, :nsp] = sel_g
            qx_ref[r * tq:(r + 1) * tq, nsp:nsp + HEAD_DIM] = q_ref[GROUP * g + r]
            qg_ref[r * tq:(r + 1) * tq, :] = q_ref[GROUP * g + r]
        o_s = branch(scores_slc, kvs_ref, g, None, None, 0)
        o_w = branch(scores_win, kvw_ref, g, qt - n_win, win_lo, jnp.maximum(qt - n_win + 1, 0))

        for r in range(GROUP):
            hd = GROUP * g + r
            cols = slice(hd * HEAD_DIM, (hd + 1) * HEAD_DIM)
            head = slice(r * tq, (r + 1) * tq)
            o_ref[:, cols] = (oc[:, cols] + gates[:, 3 * hd + 1:3 * hd + 2] * o_s[head]
                              + gates[:, 3 * hd + 2:3 * hd + 3] * o_w[head]).astype(o_ref.dtype)


def _prompt_attn(q_hm, kvs_b, kvw_b, sel, expand, oc, gates):
    b, _, t, _ = q_hm.shape
    tq = kvs_b.shape[-1]
    rows = GROUP * tq
    nsp = sel.shape[2] // N_KV_HEADS
    k_ext = LANES * (-(-(nsp + HEAD_DIM) // LANES))
    tile3 = lambda i, k: (i, k, 0)
    kv_spec = pl.BlockSpec((None,) + kvs_b.shape[1:], lambda i, k: (i, 0, 0, 0, 0))
    return pl.pallas_call(
        _prompt_attn_kernel,
        grid=(b, t // tq),
        in_specs=[
            pl.BlockSpec((None, N_HEADS, tq, HEAD_DIM), lambda i, k: (i, 0, k, 0)),
            kv_spec, kv_spec,
            pl.BlockSpec((None, tq, sel.shape[2]), tile3),
            _const_spec(expand.shape),
            pl.BlockSpec((None, tq, NSA_WIDTH), tile3),
            pl.BlockSpec((None, tq, LANES), tile3),
        ],
        out_specs=pl.BlockSpec((None, tq, NSA_WIDTH), tile3),
        out_shape=jax.ShapeDtypeStruct((b, t, NSA_WIDTH), BF16),
        scratch_shapes=[pltpu.VMEM((rows, k_ext), BF16), pltpu.VMEM((rows, HEAD_DIM), BF16),
                        pltpu.VMEM((t // tq, rows, tq), F32),
                        pltpu.VMEM((rows, LANES), F32), pltpu.VMEM((rows, LANES), F32),
                        pltpu.VMEM((rows, HEAD_DIM), F32)],
        compiler_params=_params("arbitrary", "arbitrary"),
        name="prompt_attn",
    )(q_hm, kvs_b, kvw_b, sel, expand, oc, gates)


Q_PAD = 8


def _group_q(q_ref, g):
    return jnp.concatenate([q_ref[GROUP * g + r] for r in range(GROUP)], axis=0).astype(BF16)


def _two_piece_attention(qg, k_old, v_old, bias_old, k_new, v_new, bias_new):
    def scores(k_t, bias):
        s = jnp.dot(qg, k_t.astype(BF16), preferred_element_type=F32)
        nk = s.shape[1]
        return (s.reshape(GROUP, Q_PAD, nk) + bias[None]).reshape(GROUP * Q_PAD, nk)
    s_old = scores(k_old, bias_old)
    s_new = scores(k_new, bias_new)
    m = jnp.maximum(jnp.max(s_old, axis=-1, keepdims=True), jnp.max(s_new, axis=-1, keepdims=True))
    p_old = jnp.exp(s_old - m)
    p_new = jnp.exp(s_new - m)
    den = jnp.sum(p_old, axis=-1, keepdims=True) + jnp.sum(p_new, axis=-1, keepdims=True)
    pv = (lax.dot_general(p_old.astype(BF16), v_old.astype(BF16), _NT, preferred_element_type=F32)
          + lax.dot_general(p_new.astype(BF16), v_new.astype(BF16), _NT, preferred_element_type=F32))
    return pv / den


def _add_gated(prev_ref, gate_ref, o_ref, per_g, branch):
    prev, gates = prev_ref[...], gate_ref[...]
    for g in range(N_KV_HEADS):
        for r in range(GROUP):
            hd = GROUP * g + r
            cols = slice(hd * HEAD_DIM, (hd + 1) * HEAD_DIM)
            o_ref[:, cols] = (prev[:, cols] + gates[:, 3 * hd + branch:3 * hd + branch + 1]
                              * per_g[g][r * Q_PAD:(r + 1) * Q_PAD])


def _new_rows_bias(dec_seq):
    q = lax.broadcasted_iota(jnp.int32, (Q_PAD, LANES), 0)
    i = lax.broadcasted_iota(jnp.int32, (Q_PAD, LANES), 1) - (LANES - dec_seq)
    return jnp.where((i >= 0) & (i <= q), 0.0, NEG_INF)


def _sample_slc_kernel(pt_ref, cache_hbm, q_ref, sel_ref, new_ref, e_ref, prev_ref, gate_ref, o_ref,
                       buf, sem, *, dec_seq):
    n_pages = pt_ref.shape[1]

    def copy(page, slot, p):
        return pltpu.make_async_copy(cache_hbm.at[page],
                                     buf.at[slot, :, :, pl.ds(p * PAGE_SIZE, PAGE_SIZE)], sem.at[slot])

    def start_fetch(seq, slot):
        for p in range(n_pages):
            copy(pt_ref[seq, p], slot, p).start()

    def wait_fetch(slot):
        for p in range(n_pages):
            copy(0, slot, p).wait()

    slot = _paged_prologue(start_fetch, wait_fetch)
    nsp = sel_ref.shape[1] // N_KV_HEADS
    n_past_blk = n_pages * (PAGE_SIZE // SLC_BLOCK)
    new_bias = _new_rows_bias(dec_seq)
    per_g = []
    for g in range(N_KV_HEADS):
        sel_g = sel_ref[:, g * nsp:(g + 1) * nsp]
        bias_old = jnp.dot(sel_g.astype(BF16), e_ref[...], preferred_element_type=F32)
        bias_new = new_bias + sel_g[:, n_past_blk:n_past_blk + 1]
        per_g.append(_two_piece_attention(_group_q(q_ref, g), buf[slot, g], buf[slot, N_KV_HEADS + g], bias_old,
                                          new_ref[g], new_ref[N_KV_HEADS + g], bias_new))
    _add_gated(prev_ref, gate_ref, o_ref, per_g, 1)


def _sample_slc(page_table, cache, q_s, sel, new_t, expand, prev, gates, dec_seq):
    db, n_pages = page_table.shape
    past = n_pages * PAGE_SIZE
    seq3 = lambda i, pt: (i, 0, 0)
    seq4 = lambda i, pt: (i, 0, 0, 0)
    return pl.pallas_call(
        functools.partial(_sample_slc_kernel, dec_seq=dec_seq),
        grid_spec=pltpu.PrefetchScalarGridSpec(
            num_scalar_prefetch=1,
            grid=(db,),
            in_specs=[pl.BlockSpec(memory_space=pl.ANY),
                      pl.BlockSpec((None, N_HEADS, Q_PAD, HEAD_DIM), seq4),
                      pl.BlockSpec((None, Q_PAD, sel.shape[2]), seq3),
                      pl.BlockSpec((None, N_KV, HEAD_DIM, LANES), seq4),
                      pl.BlockSpec(expand.shape, lambda i, pt: (0, 0)),
                      pl.BlockSpec((None, Q_PAD, NSA_WIDTH), seq3),
                      pl.BlockSpec((None, Q_PAD, LANES), seq3)],
            out_specs=pl.BlockSpec((None, Q_PAD, NSA_WIDTH), seq3),
            scratch_shapes=[pltpu.VMEM((2, N_KV, HEAD_DIM, past), F32), pltpu.SemaphoreType.DMA((2,))],
        ),
        out_shape=jax.ShapeDtypeStruct((db, Q_PAD, NSA_WIDTH), F32),
        compiler_params=_params("arbitrary"),
        name="sample_slc",
    )(page_table, cache, q_s, sel, new_t, expand, prev, gates)


def _sample_win_kernel(q_ref, st_ref, new_ref, prev_ref, gate_ref, o_ref, st_out_ref, *, dec_seq):
    wb = st_ref.shape[-1]
    q = lax.broadcasted_iota(jnp.int32, (Q_PAD, wb), 0)
    i = lax.broadcasted_iota(jnp.int32, (Q_PAD, wb), 1)
    bias_old = jnp.where(wb + q - i <= WINDOW, 0.0, NEG_INF)
    new_bias = _new_rows_bias(dec_seq)
    per_g = [_two_piece_attention(_group_q(q_ref, g), st_ref[g], st_ref[N_KV_HEADS + g], bias_old,
                                  new_ref[g], new_ref[N_KV_HEADS + g], new_bias)
             for g in range(N_KV_HEADS)]
    _add_gated(prev_ref, gate_ref, o_ref, per_g, 2)
    lane = lax.broadcasted_iota(jnp.int32, (HEAD_DIM, wb), 1)
    for j in range(N_KV):
        shifted = pltpu.roll(st_ref[j], wb - dec_seq, axis=1)
        st_out_ref[j] = jnp.where(lane >= wb - dec_seq, jnp.tile(new_ref[j], (1, wb // LANES)), shifted)


def _sample_win(q_s, st_win, new_t, prev, gates, dec_seq):
    db, _, _, wb = st_win.shape
    seq3 = lambda i: (i, 0, 0)
    seq4 = lambda i: (i, 0, 0, 0)
    return pl.pallas_call(
        functools.partial(_sample_win_kernel, dec_seq=dec_seq),
        grid=(db,),
        in_specs=[pl.BlockSpec((None, N_HEADS, Q_PAD, HEAD_DIM), seq4),
                  pl.BlockSpec((None, N_KV, HEAD_DIM, wb), seq4),
                  pl.BlockSpec((None, N_KV, HEAD_DIM, LANES), seq4),
                  pl.BlockSpec((None, Q_PAD, NSA_WIDTH), seq3),
                  pl.BlockSpec((None, Q_PAD, LANES), seq3)],
        out_specs=[pl.BlockSpec((None, Q_PAD, NSA_WIDTH), seq3), pl.BlockSpec((None, N_KV, HEAD_DIM, wb), seq4)],
        out_shape=[jax.ShapeDtypeStruct((db, Q_PAD, NSA_WIDTH), F32),
                   jax.ShapeDtypeStruct((db, N_KV, HEAD_DIM, wb), F32)],
        compiler_params=_params("arbitrary"),
        name="sample_win",
    )(q_s, st_win, new_t, prev, gates)


POOL_TM = 512


def _pool_project(d_groups, wp_ref, scale_ref, o_ref):
    for gi, d in enumerate(d_groups):
        cols = slice(gi * POOL_GROUP_WIDTH, (gi + 1) * POOL_GROUP_WIDTH)
        y = jnp.dot(d.astype(BF16), wp_ref[gi], preferred_element_type=F32)
        o_ref[:, cols] = (y * scale_ref[:, cols]).astype(o_ref.dtype)


def _pool_prompt_kernel(u_ref, wp_ref, scale_ref, o_ref, ext_ref):
    tm = o_ref.shape[0]
    t0 = pl.multiple_of(pl.program_id(1) * tm, tm)
    halo = u_ref[pl.ds(pl.multiple_of(jnp.maximum(t0 - POOL_HALO, 0), POOL_HALO), POOL_HALO), :]
    ext_ref[0:POOL_HALO, :] = jnp.where(t0 > 0, halo, 0.0)
    ext_ref[POOL_HALO:, :] = u_ref[pl.ds(t0, tm), :]
    pos = t0 + lax.broadcasted_iota(jnp.int32, (tm, POOL_GROUP_WIDTH), 0)
    d_groups = []
    for gi, w in enumerate(POOL_WINDOWS):
        cols = slice(gi * POOL_GROUP_WIDTH, (gi + 1) * POOL_GROUP_WIDTH)
        e = ext_ref[:, cols]
        acc = e
        span = 1
        while span < w:
            acc = acc + pltpu.roll(acc, span, axis=0)
            span *= 2
        cnt = jnp.minimum(w, pos + 1).astype(F32)
        d_groups.append(acc[POOL_HALO:] / cnt - e[POOL_HALO:])
    _pool_project(d_groups, wp_ref, scale_ref, o_ref)


def _pool_prompt(u, wp, scale):
    b, t, pw = u.shape
    tm = min(POOL_TM, t)
    return pl.pallas_call(
        _pool_prompt_kernel,
        grid=(b, t // tm),
        in_specs=[pl.BlockSpec((None, t, pw), lambda i, k: (i, 0, 0)), _const_spec(wp.shape), _const_spec(scale.shape)],
        out_specs=pl.BlockSpec((None, tm, pw), lambda i, k: (i, k, 0)),
        out_shape=jax.ShapeDtypeStruct((b, t, pw), BF16),
        scratch_shapes=[pltpu.VMEM((POOL_HALO + tm, pw), F32)],
        compiler_params=_params("arbitrary", "arbitrary"),
        name="pool_prompt",
    )(u, wp, scale)


def _pool_sample_kernel(ext_ref, wp_ref, scale_ref, o_ref, *, past_len, dec_seq):
    db = ext_ref.shape[1]
    for q in range(dec_seq):
        d_groups = []
        for gi, w in enumerate(POOL_WINDOWS):
            cols = slice(gi * POOL_GROUP_WIDTH, (gi + 1) * POOL_GROUP_WIDTH)
            row = POOL_HALO + q
            acc = ext_ref[row, :, cols]
            for i in range(1, w):
                acc = acc + ext_ref[row - i, :, cols]
            cnt = float(min(w, past_len + q + 1))
            d_groups.append(acc / cnt - ext_ref[row, :, cols])
        _pool_project(d_groups, wp_ref, scale_ref, o_ref.at[pl.ds(q * db, db)])


def _pool_sample(ext, wp, scale, past_len, dec_seq):
    rows, db, pw = ext.shape
    return pl.pallas_call(
        functools.partial(_pool_sample_kernel, past_len=past_len, dec_seq=dec_seq),
        grid=(1,),
        in_specs=[_const_spec(ext.shape), _const_spec(wp.shape), _const_spec(scale.shape)],
        out_specs=_const_spec((dec_seq * db, pw)),
        out_shape=jax.ShapeDtypeStruct((dec_seq * db, pw), BF16),
        compiler_params=_params("arbitrary"),
        name="pool_sample",
    )(ext, wp, scale)


OUTPROJ_TM = 512


def _outproj_kernel(x_ref, o_ref, p_ref, wo_ref, wp_ref, y_ref):
    y_ref[...] = (x_ref[...] + jnp.dot(o_ref[...].astype(BF16), wo_ref[...], preferred_element_type=F32)
                  + jnp.dot(p_ref[...], wp_ref[...], preferred_element_type=F32))


def _outproj(x, o, p, wo, wp):
    m, d = x.shape
    tm = min(OUTPROJ_TM, m)
    row = lambda i: (i, 0)
    return pl.pallas_call(
        _outproj_kernel,
        grid=(m // tm,),
        in_specs=[pl.BlockSpec((tm, d), row), pl.BlockSpec((tm, o.shape[1]), row), pl.BlockSpec((tm, p.shape[1]), row),
                  _const_spec(wo.shape), _const_spec(wp.shape)],
        out_specs=pl.BlockSpec((tm, d), row),
        out_shape=jax.ShapeDtypeStruct((m, d), F32),
        compiler_params=_params("arbitrary"),
        name="outproj",
    )(x, o, p, wo, wp)


def _cmp_weights(w1, w2, pe):
    n_slots = CMP_BLOCK // CMP_STRIDE
    w1s = w1.reshape(n_slots, CMP_FLAT, CMP_HIDDEN)
    w1cat = jnp.concatenate([w1s[h] for h in range(n_slots)], axis=1).astype(BF16)
    pe8 = jnp.pad(pe.reshape(n_slots, CMP_FLAT), ((0, SUBLANES - n_slots), (0, 0)))
    return w1cat, w2.astype(BF16), pe8


def _pad_axis(x, axis, size, front=False):
    pad = [(0, 0)] * x.ndim
    extra = size - x.shape[axis]
    pad[axis] = (extra, 0) if front else (0, extra)
    return jnp.pad(x, pad)


def _rows_to_state(x_fm):
    b, _, _, t = x_fm.shape
    return x_fm.reshape(b, 2, N_KV_HEADS, HEAD_DIM, t).transpose(0, 4, 1, 2, 3)


def _state_to_fm(x):
    b, r = x.shape[:2]
    return x.transpose(0, 2, 3, 4, 1).reshape(b, N_KV, HEAD_DIM, r)


def kernel(x_prompt, x_sample, cache_kv_cmp, cache_kv_slc, page_table, state_kv_win, state_pool, n_ffn1, w_ffn1_gate, w_ffn1_up, w_ffn1_down, n_mix, w_in, w_cmp_k1, w_cmp_k2, pe_cmp_k, w_cmp_v1, w_cmp_v2, pe_cmp_v, w_pool, pool_scale, w_out, n_ffn2, w_ffn2_gate, w_ffn2_up, w_ffn2_down, n_final):
    b, t, d = x_prompt.shape
    db, ds, _ = x_sample.shape
    depth = w_in.shape[0]
    n_pages = page_table.shape[1]
    past = n_pages * PAGE_SIZE
    wb = state_kv_win.shape[2]
    assert t % ATTN_TQ == 0 and t % INPROJ_TM == 0 and WINDOW % ATTN_TQ == 0 and t >= WINDOW
    assert ds <= Q_PAD and ds <= CMP_STRIDE and wb == WINDOW and past % SLC_BLOCK == 0
    assert (t // CMP_STRIDE) % CMP_ROWS == 0 and (db * ds) % SUBLANES == 0 and n_pages % CMP_PAGE_GROUP == 0

    xp = x_prompt.reshape(b * t, d)
    xs = x_sample.reshape(db * ds, d)
    pos_p = jnp.arange(t, dtype=jnp.int32)
    pos_s = past + jnp.arange(ds, dtype=jnp.int32)
    tab_p = _rope_tables(pos_p)
    tab_s = tuple(jnp.tile(a, (db, 1)) for a in _rope_tables(pos_s)[:3]) + tuple(
        jnp.tile(a, (1, db)) for a in _rope_tables(pos_s)[3:])

    n_chunk_p = t // CMP_STRIDE
    n_slc_p = -(-t // SLC_BLOCK)
    agg_p = _agg_matrix(n_chunk_p, n_chunk_p - 1, n_slc_p, LANES * (-(-n_slc_p // LANES)))
    expand_p = _expand_matrix(t, agg_p.shape[1], ATTN_TQ)
    n_chunk_s = past // CMP_STRIDE
    n_slc_s = -(-(past + ds) // SLC_BLOCK)
    agg_s = _agg_matrix(n_chunk_s, n_chunk_s, n_slc_s, LANES * (-(-n_slc_s // LANES)))
    expand_s = _expand_matrix(past, agg_s.shape[1], past)[0]

    st_p = ([], [], [], [])
    st_s = ([], [], [], [])
    for l in range(depth):
        last = l == depth - 1
        ffn1 = (n_ffn1[l], w_ffn1_gate[l].astype(BF16), w_ffn1_up[l].astype(BF16), w_ffn1_down[l].astype(BF16))
        ffn2 = (n_ffn2[l], w_ffn2_gate[l].astype(BF16), w_ffn2_up[l].astype(BF16), w_ffn2_down[l].astype(BF16))
        w = w_in[l]
        o_kv, o_gate, o_pool = NSA_WIDTH, NSA_WIDTH + 3 * KV_WIDTH, NSA_WIDTH + 3 * KV_WIDTH + N_GATES
        wq = w[:, :o_kv].astype(BF16)
        wkv_t = w[:, o_kv:o_gate].T.astype(BF16)
        wkc = w[:, o_kv:o_kv + KV_WIDTH].astype(BF16)
        wgt = _pad_axis(w[:, o_gate:o_pool], 1, LANES).astype(BF16)
        wu = w[:, o_pool:].astype(BF16)
        proj = (n_mix[l], wq, wkv_t, wkc, wgt, wu)
        wk1, wk2, pek = _cmp_weights(w_cmp_k1[l], w_cmp_k2[l], pe_cmp_k[l])
        wv1, wv2, pev = _cmp_weights(w_cmp_v1[l], w_cmp_v2[l], pe_cmp_v[l])
        pe = jnp.stack([pek, pev])
        wp = w_pool[l].astype(BF16)
        scale = pool_scale[l].reshape(1, POOL_WIDTH)
        wo_nsa = w_out[l][:NSA_WIDTH].astype(BF16)
        wo_pool = w_out[l][NSA_WIDTH:].astype(BF16)

        xp = _ffn(xp, *ffn1)
        q_hm, q2_hm, kvc_rm, kvc, kvs, kvw, kvs_b, kvw_b, gates, u = _inproj(xp, *proj, tab_p, BF16)
        kc, vc = _compress_prompt(kvc_rm.reshape(b, n_chunk_p, CHUNK_WIDTH), pe, wk1, wv1, wk2, wv2)
        gates3 = gates.reshape(b, t, LANES)
        oc, sel = _cmp_attn(q_hm, kc, vc, gates3, agg_p, tq=ATTN_TQ, pos0=0, tiled=True,
                            n_slc=n_slc_p, sel_dtype=BF16)
        o_mix = _prompt_attn(q2_hm, kvs_b, kvw_b, sel, expand_p, oc, gates3)
        u3 = u.reshape(b, t, POOL_WIDTH)
        pool_out = _pool_prompt(u3, wp, scale)
        xp = _outproj(xp, o_mix.reshape(b * t, NSA_WIDTH), pool_out.reshape(b * t, POOL_WIDTH), wo_nsa, wo_pool)
        xp = _ffn(xp, *ffn2, g_final=n_final if last else None)
        st_p[0].append(_rows_to_state(kvc))
        st_p[1].append(_rows_to_state(kvs))
        st_p[2].append(_rows_to_state(kvw[..., t - min(WINDOW, t):]))
        st_p[3].append(u3[:, t - POOL_STATE:])

        xs = _ffn(xs, *ffn1)
        q, _, _, kvc, kvs, kvw, _, _, gates, u = _inproj(xs, *proj, tab_s, F32)
        per_seq = lambda a: a.reshape(N_KV, HEAD_DIM, db, ds).transpose(2, 0, 1, 3)
        kvc_n, kvs_n, kvw_n = per_seq(kvc), per_seq(kvs), per_seq(kvw)
        ynew = _pad_axis(kvc_n.transpose(0, 1, 3, 2), 2, CMP_STRIDE).reshape(db, N_KV, CMP_FLAT)
        cache_c = _state_to_fm(cache_kv_cmp[l])
        kc, vc = _compress_sample(page_table, cache_c, _pad_axis(ynew, 1, SUBLANES), pe, wk1, wv1, wk2, wv2)
        q_s = _pad_axis(q.reshape(N_HEADS, db, ds, HEAD_DIM).transpose(1, 0, 2, 3), 2, Q_PAD)
        gates_s = _pad_axis(gates.reshape(db, ds, LANES), 1, Q_PAD)
        oc, sel = _cmp_attn(q_s, kc, vc, gates_s, agg_s, tq=Q_PAD, pos0=past, tiled=False,
                            n_slc=n_slc_s, sel_dtype=F32)
        o_cs = _sample_slc(page_table, _state_to_fm(cache_kv_slc[l]), q_s, sel,
                           _pad_axis(kvs_n, 3, LANES, front=True), expand_s, oc, gates_s, ds)
        o_mix, st_win_new = _sample_win(q_s, _state_to_fm(state_kv_win[l]),
                                        _pad_axis(kvw_n, 3, LANES, front=True), o_cs, gates_s, ds)
        u3 = u.reshape(db, ds, POOL_WIDTH)
        ext = jnp.concatenate([jnp.zeros((db, POOL_HALO - POOL_STATE, POOL_WIDTH), F32), state_pool[l], u3], axis=1)
        pool_out = _pool_sample(ext.transpose(1, 0, 2), wp, scale, past, ds)
        pool_out = pool_out.reshape(ds, db, POOL_WIDTH).transpose(1, 0, 2).reshape(db * ds, POOL_WIDTH)
        xs = _outproj(xs, o_mix[:, :ds].reshape(db * ds, NSA_WIDTH), pool_out, wo_nsa, wo_pool)
        xs = _ffn(xs, *ffn2, g_final=n_final if last else None)
        st_s[0].append(_rows_to_state(kvc_n))
        st_s[1].append(_rows_to_state(kvs_n))
        st_s[2].append(_rows_to_state(st_win_new))
        st_s[3].append(jnp.concatenate([state_pool[l], u3], axis=1)[:, ds:])

    return (xp.reshape(b, t, d), xs.reshape(db, ds, d),
            jnp.stack(st_p[0]), jnp.stack(st_p[1]), jnp.stack(st_p[2]), jnp.stack(st_p[3]),
            jnp.stack(st_s[0]), jnp.stack(st_s[1]), jnp.stack(st_s[2]), jnp.stack(st_s[3]))
```

```python
import functools

import numpy as np
import jax
import jax.numpy as jnp
from jax import lax
from jax.experimental import pallas as pl
from jax.experimental.pallas import tpu as pltpu

F32 = jnp.float32
BF16 = jnp.bfloat16

HEAD_DIM = 64
N_HEADS = 8
N_KV_HEADS = 2
GROUP = N_HEADS // N_KV_HEADS
NSA_WIDTH = N_HEADS * HEAD_DIM
N_KV = 2 * N_KV_HEADS
KV_WIDTH = N_KV * HEAD_DIM
ROPE_DIM = HEAD_DIM // 4
ROPE_HALF = ROPE_DIM // 2
ROPE_THETA = 500000.0
CMP_BLOCK = 32
CMP_STRIDE = 16
CMP_HIDDEN = 256
CMP_FLAT = CMP_STRIDE * HEAD_DIM
SLC_BLOCK = 64
TOPK_BLOCKS = 16
WINDOW = 512
PAGE_SIZE = 128
POOL_WINDOWS = (2, 4, 8, 16)
POOL_GROUP_WIDTH = 128
POOL_WIDTH = POOL_GROUP_WIDTH * len(POOL_WINDOWS)
POOL_STATE = max(POOL_WINDOWS) - 1
POOL_HALO = 16
RMS_EPS = 1e-6
FORCE_SCORE = 1e4
NEG_INF = -1e30
ATTN_SCALE = HEAD_DIM ** -0.5
LOG2_E = float(np.log2(np.e))
N_GATES = 3 * N_HEADS
CHUNK_WIDTH = CMP_STRIDE * KV_WIDTH

LANES = 128
SUBLANES = 8
VMEM_LIMIT_BYTES = 56 * 1024 * 1024

_NT = (((1,), (1,)), ((), ()))


def _params(*sem):
    return pltpu.CompilerParams(dimension_semantics=sem, vmem_limit_bytes=VMEM_LIMIT_BYTES)


def _rmsnorm(x, g):
    return x * lax.rsqrt(jnp.mean(x * x, axis=-1, keepdims=True) + RMS_EPS) * g


def _const_spec(shape):
    n = len(shape)
    return pl.BlockSpec(shape, lambda *_: (0,) * n)


FFN_TM = 512
FFN_CHUNK = 256


def _ffn_kernel(x_ref, g_ref, wg_ref, wu_ref, wd_ref, gf_ref, *rest, final_norm, mix):
    if mix:
        mo_ref, mp_ref, wo_ref, wp_ref, o_ref, act_ref = rest
        x = (x_ref[...] + jnp.dot(mo_ref[...].astype(BF16), wo_ref[...], preferred_element_type=F32)
             + jnp.dot(mp_ref[...], wp_ref[...], preferred_element_type=F32))
    else:
        o_ref, act_ref = rest
        x = x_ref[...]
    h = _rmsnorm(x, g_ref[...]).astype(BF16)
    d_ff = wg_ref.shape[1]
    for c in range(d_ff // FFN_CHUNK):
        sl = slice(c * FFN_CHUNK, (c + 1) * FFN_CHUNK)
        a = jnp.dot(h, wg_ref[:, sl], preferred_element_type=F32)
        u = jnp.dot(h, wu_ref[:, sl], preferred_element_type=F32)
        act_ref[:, sl] = (a * jax.nn.sigmoid(a) * u).astype(BF16)
    y = x + 0.5 * jnp.dot(act_ref[...], wd_ref[...], preferred_element_type=F32)
    if final_norm:
        y = _rmsnorm(y, gf_ref[...])
    o_ref[...] = y


def _ffn(x, g, wg, wu, wd, g_final=None, mix=None):
    m, d = x.shape
    d_ff = wg.shape[1]
    tm = min(FFN_TM, m)
    final_norm = g_final is not None
    gf = g_final if final_norm else g
    row = lambda i: (i, 0)
    once = lambda shape: pl.BlockSpec(shape, lambda i: (0, 0), pipeline_mode=pl.Buffered(1))
    in_specs = [pl.BlockSpec((tm, d), row), _const_spec((1, d)),
                once((d, d_ff)), once((d, d_ff)), once((d_ff, d)), _const_spec((1, d))]
    args = [x, g.reshape(1, d), wg, wu, wd, gf.reshape(1, d)]
    if mix is not None:
        mo, mp, wo, wp = mix
        in_specs += [pl.BlockSpec((tm, mo.shape[1]), row), pl.BlockSpec((tm, mp.shape[1]), row),
                     once(wo.shape), once(wp.shape)]
        args += [mo, mp, wo, wp]
    return pl.pallas_call(
        functools.partial(_ffn_kernel, final_norm=final_norm, mix=mix is not None),
        grid=(m // tm,),
        in_specs=in_specs,
        out_specs=pl.BlockSpec((tm, d), row),
        out_shape=jax.ShapeDtypeStruct((m, d), F32),
        scratch_shapes=[pltpu.VMEM((tm, d_ff), BF16)],
        compiler_params=_params("arbitrary"),
        name="ffn",
    )(*args)


INPROJ_TM = 512
ATTN_TQ = 256


def _rope_tables(pos):
    inv = jnp.power(ROPE_THETA, -jnp.arange(ROPE_HALF, dtype=F32) / ROPE_HALF)
    ang = pos.astype(F32)[:, None] * inv[None, :]
    cos, sin = jnp.cos(ang), jnp.sin(ang)
    t = pos.shape[0]
    rest = HEAD_DIM - ROPE_DIM
    z_half = jnp.zeros((t, ROPE_HALF), F32)
    z_rest = jnp.zeros((t, rest), F32)
    cos_t = jnp.concatenate([cos, cos, jnp.ones((t, rest), F32)], axis=1)
    sin_a = jnp.concatenate([-sin, z_half, z_rest], axis=1)
    sin_b = jnp.concatenate([z_half, sin, z_rest], axis=1)
    rep = LANES // HEAD_DIM
    return tuple(jnp.tile(a, (1, rep)) for a in (cos_t, sin_a, sin_b)) + (cos.T, sin.T)


def _rope(z, cos_t, sin_a, sin_b):
    outs = []
    for c in range(z.shape[1] // LANES):
        zc = z[:, c * LANES:(c + 1) * LANES]
        outs.append(zc * cos_t + pltpu.roll(zc, LANES - ROPE_HALF, axis=1) * sin_a
                    + pltpu.roll(zc, ROPE_HALF, axis=1) * sin_b)
    return outs[0] if len(outs) == 1 else jnp.concatenate(outs, axis=1)


def _rope_fm(kt, cos, sin):
    x1, x2 = kt[0:ROPE_HALF], kt[ROPE_HALF:ROPE_DIM]
    return jnp.concatenate([x1 * cos - x2 * sin, x2 * cos + x1 * sin, kt[ROPE_DIM:]], axis=0)


def _inproj_kernel(x_ref, g_ref, wq_ref, wkv_t_ref, wkc_ref, wgt_ref, wu_ref,
                   cos_ref, sa_ref, sb_ref, cos_fm_ref, sin_fm_ref,
                   q_ref, q2_ref, kvc_rm_ref, kvc_ref, kvs_ref, kvw_ref, kvs_b_ref, kvw_b_ref, gate_ref, u_ref):
    h = _rmsnorm(x_ref[...], g_ref[...]).astype(BF16)
    tm = h.shape[0]
    tk = kvs_b_ref.shape[-1]
    cos_t, sin_a, sin_b = cos_ref[...], sa_ref[...], sb_ref[...]
    cos_fm, sin_fm = cos_fm_ref[...], sin_fm_ref[...]
    q = _rope(jnp.dot(h, wq_ref[...], preferred_element_type=F32), cos_t, sin_a, sin_b) * ATTN_SCALE
    q2 = q * LOG2_E
    for hd in range(N_HEADS):
        q_ref[hd] = q[:, hd * HEAD_DIM:(hd + 1) * HEAD_DIM].astype(q_ref.dtype)
        q2_ref[hd] = q2[:, hd * HEAD_DIM:(hd + 1) * HEAD_DIM].astype(BF16)
    k_width = N_KV_HEADS * HEAD_DIM
    kvc = jnp.dot(h, wkc_ref[...], preferred_element_type=F32)
    kvc_rm_ref[...] = jnp.concatenate([_rope(kvc[:, :k_width], cos_t, sin_a, sin_b), kvc[:, k_width:]], axis=1)
    kv_t = lax.dot_general(wkv_t_ref[...], h, _NT, preferred_element_type=F32)
    for i, (f_ref, b_ref) in enumerate(((kvc_ref, None), (kvs_ref, kvs_b_ref), (kvw_ref, kvw_b_ref))):
        for j in range(N_KV):
            r0 = i * KV_WIDTH + j * HEAD_DIM
            blk = kv_t[r0:r0 + HEAD_DIM]
            if j < N_KV_HEADS:
                blk = _rope_fm(blk, cos_fm, sin_fm)
            f_ref[j] = blk
            if b_ref is not None:
                for c in range(tm // tk):
                    b_ref[j, c] = blk[:, c * tk:(c + 1) * tk].astype(BF16)
    gate_ref[...] = jax.nn.sigmoid(jnp.dot(h, wgt_ref[...], preferred_element_type=F32))
    u_ref[...] = jnp.dot(h, wu_ref[...], preferred_element_type=F32)


def _inproj(x, g, wq, wkv_t, wkc, wgt, wu, tables, q_dtype):
    m, d = x.shape
    seq = tables[0].shape[0]
    nseq = m // seq
    tm = min(INPROJ_TM, seq)
    tk = min(ATTN_TQ, tm)
    n_tab = seq // tm
    row = lambda i: (i, 0)
    tab = lambda i: (i % n_tab, 0)
    tab_fm = lambda i: (0, i % n_tab)
    hm = lambda i: (i // n_tab, 0, i % n_tab, 0)
    fm = lambda i: (i // n_tab, 0, 0, i % n_tab)
    fmb = lambda i: (i // n_tab, 0, i % n_tab, 0, 0)
    fm_shape = jax.ShapeDtypeStruct((nseq, N_KV, HEAD_DIM, seq), F32)
    fmb_shape = jax.ShapeDtypeStruct((nseq, N_KV, seq // tk, HEAD_DIM, tk), BF16)
    fm_spec = pl.BlockSpec((None, N_KV, HEAD_DIM, tm), fm)
    fmb_spec = pl.BlockSpec((None, N_KV, tm // tk, HEAD_DIM, tk), fmb)
    return pl.pallas_call(
        _inproj_kernel,
        grid=(m // tm,),
        in_specs=[
            pl.BlockSpec((tm, d), row), _const_spec((1, d)),
            _const_spec(wq.shape), _const_spec(wkv_t.shape), _const_spec(wkc.shape),
            _const_spec(wgt.shape), _const_spec(wu.shape),
            pl.BlockSpec((tm, LANES), tab), pl.BlockSpec((tm, LANES), tab), pl.BlockSpec((tm, LANES), tab),
            pl.BlockSpec((ROPE_HALF, tm), tab_fm), pl.BlockSpec((ROPE_HALF, tm), tab_fm),
        ],
        out_specs=[
            pl.BlockSpec((None, N_HEADS, tm, HEAD_DIM), hm), pl.BlockSpec((None, N_HEADS, tm, HEAD_DIM), hm),
            pl.BlockSpec((tm, KV_WIDTH), row),
            fm_spec, fm_spec, fm_spec, fmb_spec, fmb_spec,
            pl.BlockSpec((tm, LANES), row), pl.BlockSpec((tm, POOL_WIDTH), row),
        ],
        out_shape=[
            jax.ShapeDtypeStruct((nseq, N_HEADS, seq, HEAD_DIM), q_dtype),
            jax.ShapeDtypeStruct((nseq, N_HEADS, seq, HEAD_DIM), BF16),
            jax.ShapeDtypeStruct((m, KV_WIDTH), F32),
            fm_shape, fm_shape, fm_shape, fmb_shape, fmb_shape,
            jax.ShapeDtypeStruct((m, LANES), F32), jax.ShapeDtypeStruct((m, POOL_WIDTH), F32),
        ],
        compiler_params=_params("arbitrary"),
        name="inproj",
    )(x, g.reshape(1, d), wq, wkv_t, wkc, wgt, wu, *tables)


CMP_ROWS = 128


def _gelu_tanh(x):
    return 0.5 * x * (1.0 + jnp.tanh(np.sqrt(2.0 / np.pi).astype(np.float32) * (x + 0.044715 * (x * x * x))))


def _interleave_heads(a0, a1):
    low = lax.broadcasted_iota(jnp.int32, a0.shape, 1) < HEAD_DIM
    return (jnp.where(low, a0, pltpu.roll(a1, HEAD_DIM, axis=1)),
            jnp.where(low, pltpu.roll(a0, HEAD_DIM, axis=1), a1))


def _pick_row(blk, r):
    row8 = lax.broadcasted_iota(jnp.int32, (SUBLANES, 1), 0)
    return jnp.sum(jnp.where(row8 == r, blk, 0.0), axis=0, keepdims=True)


def _compress_finish(kv, n, new8, pe_ref, w1_ref, w2_ref, out_ref, y_ref, has_new):
    y_ref[pl.ds(2 * n, 2 * SUBLANES), :] = jnp.concatenate([new8, pe_ref[kv]], axis=0).astype(BF16)
    p = jnp.dot(y_ref[...], w1_ref[...], preferred_element_type=F32)
    p_new, p_pe = p[2 * n:2 * n + SUBLANES], p[2 * n + SUBLANES:2 * n + 2 * SUBLANES]
    bias = _pick_row(p_pe[:, :CMP_HIDDEN], 0) + _pick_row(p_pe[:, CMP_HIDDEN:], 1)
    last = lax.broadcasted_iota(jnp.int32, (n, CMP_HIDDEN), 0) == n - 1
    for g in range(N_KV_HEADS):
        slot0 = p[g * n:(g + 1) * n, :CMP_HIDDEN]
        slot1 = pltpu.roll(p[g * n:(g + 1) * n, CMP_HIDDEN:], n - 1, axis=0)
        if has_new:
            slot1 = jnp.where(last, _pick_row(p_new[:, CMP_HIDDEN:], g), slot1)
        hid = _gelu_tanh(slot0 + slot1 + bias).astype(BF16)
        out_ref[g] = jnp.dot(hid, w2_ref[...], preferred_element_type=F32).astype(BF16)


def _compress_prompt_kernel(x_ref, pe_ref, wk1_ref, wv1_ref, wk2_ref, wv2_ref, kc_ref, vc_ref, y_ref):
    n = x_ref.shape[0]
    for kv, (w1_ref, w2_ref, out_ref) in enumerate(((wk1_ref, wk2_ref, kc_ref), (wv1_ref, wv2_ref, vc_ref))):
        def fill(r, carry):
            r0 = pl.multiple_of(r * CMP_ROWS, CMP_ROWS)
            for j in range(CMP_STRIDE // 2):
                c0 = (2 * j) * KV_WIDTH + kv * LANES
                c1 = (2 * j + 1) * KV_WIDTH + kv * LANES
                y0, y1 = _interleave_heads(x_ref[pl.ds(r0, CMP_ROWS), c0:c0 + LANES],
                                           x_ref[pl.ds(r0, CMP_ROWS), c1:c1 + LANES])
                y_ref[pl.ds(r0, CMP_ROWS), j * LANES:(j + 1) * LANES] = y0.astype(BF16)
                y_ref[pl.ds(n + r0, CMP_ROWS), j * LANES:(j + 1) * LANES] = y1.astype(BF16)
            return carry
        lax.fori_loop(0, n // CMP_ROWS, fill, 0)
        _compress_finish(kv, n, jnp.zeros((SUBLANES, CMP_FLAT), F32), pe_ref, w1_ref, w2_ref, out_ref, y_ref, False)


def _compress_prompt(x, pe, wk1, wv1, wk2, wv2):
    b, n, cw = x.shape
    out = jax.ShapeDtypeStruct((b, N_KV_HEADS, n, HEAD_DIM), BF16)
    out_spec = pl.BlockSpec((None, N_KV_HEADS, n, HEAD_DIM), lambda i: (i, 0, 0, 0))
    return pl.pallas_call(
        _compress_prompt_kernel,
        grid=(b,),
        in_specs=[pl.BlockSpec((None, n, cw), lambda i: (i, 0, 0)), _const_spec(pe.shape),
                  _const_spec(wk1.shape), _const_spec(wv1.shape), _const_spec(wk2.shape), _const_spec(wv2.shape)],
        out_specs=[out_spec, out_spec],
        out_shape=[out, out],
        scratch_shapes=[pltpu.VMEM((2 * n + 2 * SUBLANES, CMP_FLAT), BF16)],
        compiler_params=_params("arbitrary"),
        name="compress_prompt",
    )(x, pe, wk1, wv1, wk2, wv2)


def _paged_prologue(start_fetch, wait_fetch):
    b = pl.program_id(0)
    slot = lax.rem(b, 2)

    @pl.when(b == 0)
    def _():
        start_fetch(0, 0)

    @pl.when(b + 1 < pl.num_programs(0))
    def _():
        start_fetch(b + 1, 1 - slot)

    wait_fetch(slot)
    return slot


CMP_PAGE_GROUP = 8


def _dechunk_perm():
    perm = np.zeros((N_KV_HEADS * PAGE_SIZE, 2 * LANES), np.float32)
    cpp = PAGE_SIZE // CMP_STRIDE
    for g in range(N_KV_HEADS):
        for c in range(cpp):
            for j in range(CMP_STRIDE // 2):
                for par in range(2):
                    perm[g * PAGE_SIZE + CMP_STRIDE * c + 2 * j + par, par * LANES + g * HEAD_DIM + j * cpp + c] = 1.0
    return jnp.asarray(perm, BF16)


def _compress_sample_kernel(pt_ref, cache_hbm, ynew_ref, pe_ref, perm_ref, wk1_ref, wv1_ref, wk2_ref, wv2_ref,
                            kc_ref, vc_ref, buf, sem, y_ref):
    n_pages = pt_ref.shape[1]
    n = n_pages * (PAGE_SIZE // CMP_STRIDE)

    def copy(page, slot, p):
        return pltpu.make_async_copy(cache_hbm.at[page], buf.at[slot, p], sem.at[slot])

    def start_fetch(seq, slot):
        def body(p, carry):
            copy(pt_ref[seq, p], slot, p).start()
            return carry
        lax.fori_loop(0, n_pages, body, 0)

    def wait_fetch(slot):
        def body(p, carry):
            copy(0, slot, p).wait()
            return carry
        lax.fori_loop(0, n_pages, body, 0)

    slot = _paged_prologue(start_fetch, wait_fetch)
    ynew = ynew_ref[...]
    row8 = lax.broadcasted_iota(jnp.int32, (SUBLANES, 1), 0)
    pg = CMP_PAGE_GROUP
    cpp = PAGE_SIZE // CMP_STRIDE
    for kv, (w1_ref, w2_ref, out_ref) in enumerate(((wk1_ref, wk2_ref, kc_ref), (wv1_ref, wv2_ref, vc_ref))):
        for grp in range(n_pages // pg):
            lhs = jnp.concatenate(
                [jnp.concatenate([buf[slot, grp * pg + q, N_KV_HEADS * kv + g] for g in range(N_KV_HEADS)], axis=1)
                 for q in range(pg)], axis=0).astype(BF16)
            out = jnp.dot(lhs, perm_ref[...], preferred_element_type=F32)
            x = jnp.concatenate(
                [jnp.concatenate([out[q * HEAD_DIM:(q + 1) * HEAD_DIM, :LANES],
                                  out[q * HEAD_DIM:(q + 1) * HEAD_DIM, LANES:]], axis=0) for q in range(pg)], axis=1)
            xt = x.T
            r0 = grp * (pg * cpp)
            for g in range(N_KV_HEADS):
                for j in range(CMP_STRIDE // 2):
                    piece = jnp.concatenate(
                        [xt[q * LANES + g * HEAD_DIM + j * cpp:q * LANES + g * HEAD_DIM + (j + 1) * cpp]
                         for q in range(pg)], axis=0)
                    y_ref[kv, g * n + r0:g * n + r0 + pg * cpp, j * LANES:(j + 1) * LANES] = piece.astype(BF16)
        new8 = jnp.where(row8 == 0, _pick_row(ynew, N_KV_HEADS * kv),
                         jnp.where(row8 == 1, _pick_row(ynew, N_KV_HEADS * kv + 1), 0.0))
        _compress_finish(kv, n, new8, pe_ref, w1_ref, w2_ref, out_ref, y_ref.at[kv], True)


def _compress_sample(page_table, cache, ynew, pe, wk1, wv1, wk2, wv2):
    db, n_pages = page_table.shape
    n = n_pages * (PAGE_SIZE // CMP_STRIDE)
    out = jax.ShapeDtypeStruct((db, N_KV_HEADS, n, HEAD_DIM), BF16)
    out_spec = pl.BlockSpec((None, N_KV_HEADS, n, HEAD_DIM), lambda i, pt: (i, 0, 0, 0))
    const = lambda s: pl.BlockSpec(s, lambda i, pt: (0,) * len(s))
    perm = _dechunk_perm()
    return pl.pallas_call(
        _compress_sample_kernel,
        grid_spec=pltpu.PrefetchScalarGridSpec(
            num_scalar_prefetch=1,
            grid=(db,),
            in_specs=[pl.BlockSpec(memory_space=pl.ANY),
                      pl.BlockSpec((None, SUBLANES, CMP_FLAT), lambda i, pt: (i, 0, 0)),
                      const(pe.shape), const(perm.shape),
                      const(wk1.shape), const(wv1.shape), const(wk2.shape), const(wv2.shape)],
            out_specs=[out_spec, out_spec],
            scratch_shapes=[pltpu.VMEM((2, n_pages, N_KV, HEAD_DIM, PAGE_SIZE), F32),
                            pltpu.SemaphoreType.DMA((2,)),
                            pltpu.VMEM((2, 2 * n + 2 * SUBLANES, CMP_FLAT), BF16)],
        ),
        out_shape=[out, out],
        compiler_params=_params("arbitrary"),
        name="compress_sample",
    )(page_table, cache, ynew, pe, perm, wk1, wv1, wk2, wv2)


def _select_mask(p_slc, t_pos, j, n_slc, blk_axis):
    cur = t_pos // SLC_BLOCK
    forced = (j == 0) | (j == cur) | (j == cur - 1)
    real = j < n_slc
    score = jnp.where(forced, FORCE_SCORE, jnp.where(j <= cur, p_slc, NEG_INF))
    score = jnp.where(real, score, -jnp.inf)
    rank = jnp.zeros(score.shape, jnp.int32)
    for jp in range(n_slc):
        c = score[jp:jp + 1, :] if blk_axis == 0 else score[:, jp:jp + 1]
        rank = rank + ((c > score) | ((c == score) & (j > jp))).astype(jnp.int32)
    return jnp.where((rank < TOPK_BLOCKS) & real, 0.0, NEG_INF)


def _cmp_attn_kernel(q_ref, kc_ref, vc_ref, gate_ref, agg_ref, oc_ref, sel_ref, *, pos0, tiled, n_slc, blocks_major):
    tq = q_ref.shape[1]
    n = kc_ref.shape[1]
    nsp = agg_ref.shape[0] if blocks_major else agg_ref.shape[1]
    t0 = pos0 + (pl.program_id(1) * tq if tiled else 0)
    t_n = t0 + lax.broadcasted_iota(jnp.int32, (tq, n), 0)
    blk_end = lax.broadcasted_iota(jnp.int32, (tq, n), 1) * CMP_STRIDE + (CMP_BLOCK - 1)
    valid = (blk_end <= t_n)[None]
    if blocks_major:
        n_rows = SUBLANES * (-(-n_slc // SUBLANES))
        t_s = t0 + lax.broadcasted_iota(jnp.int32, (n_rows, tq), 1)
        j = lax.broadcasted_iota(jnp.int32, (n_rows, tq), 0)
    else:
        t_s = t0 + lax.broadcasted_iota(jnp.int32, (tq, nsp), 0)
        j = lax.broadcasted_iota(jnp.int32, (tq, nsp), 1)
    gates = gate_ref[...]
    for g in range(N_KV_HEADS):
        qg = jnp.concatenate([q_ref[GROUP * g + r].astype(F32) for r in range(GROUP)], axis=0).astype(BF16)
        s = lax.dot_general(qg, kc_ref[g], _NT, preferred_element_type=F32).reshape(GROUP, tq, n)
        s = jnp.where(valid, s, NEG_INF)
        e = jnp.where(valid, jnp.exp(s - jnp.max(s, axis=-1, keepdims=True)), 0.0)
        p = e / jnp.maximum(jnp.sum(e, axis=-1, keepdims=True), 1e-30)
        o = jnp.dot(p.reshape(GROUP * tq, n).astype(BF16), vc_ref[g], preferred_element_type=F32)
        p_grp = jnp.sum(p, axis=0)
        p_hi = p_grp.astype(BF16)
        p_lo = (p_grp - p_hi.astype(F32)).astype(BF16)
        if blocks_major:
            p_slc = (lax.dot_general(agg_ref[...], p_hi, _NT, preferred_element_type=F32)
                     + lax.dot_general(agg_ref[...], p_lo, _NT, preferred_element_type=F32))
            mask = _select_mask(p_slc[:n_rows], t_s, j, n_slc, 0)
            if n_rows < nsp:
                mask = jnp.concatenate([mask, jnp.zeros((nsp - n_rows, tq), F32)], axis=0)
            mask = mask.T
        else:
            p_slc = (jnp.dot(p_hi, agg_ref[...], preferred_element_type=F32)
                     + jnp.dot(p_lo, agg_ref[...], preferred_element_type=F32))
            mask = _select_mask(p_slc, t_s, j, n_slc, 1)
        sel_ref[:, g * nsp:(g + 1) * nsp] = mask.astype(sel_ref.dtype)
        for r in range(GROUP):
            hd = GROUP * g + r
            oc_ref[:, hd * HEAD_DIM:(hd + 1) * HEAD_DIM] = gates[:, 3 * hd:3 * hd + 1] * o[r * tq:(r + 1) * tq]


def _agg_matrix(n_cmp_pad, n_cmp, n_slc, n_slc_pad):
    c0 = np.arange(n_cmp)[:, None] * CMP_STRIDE
    s0 = np.arange(n_slc)[None, :] * SLC_BLOCK
    overlap = np.clip(np.minimum(c0 + CMP_BLOCK, s0 + SLC_BLOCK) - np.maximum(c0, s0), 0, None)
    agg = np.zeros((n_cmp_pad, n_slc_pad), np.float32)
    agg[:n_cmp, :n_slc] = overlap / CMP_BLOCK
    return jnp.asarray(agg, BF16)


def _cmp_attn(q_hm, kc, vc, gates, agg, *, tq, pos0, tiled, n_slc, sel_dtype):
    b, _, t, _ = q_hm.shape
    n = kc.shape[2]
    nsp = agg.shape[1]
    blocks_major = tq % LANES == 0
    if blocks_major:
        agg = agg.T
    return pl.pallas_call(
        functools.partial(_cmp_attn_kernel, pos0=pos0, tiled=tiled, n_slc=n_slc, blocks_major=blocks_major),
        grid=(b, t // tq),
        in_specs=[
            pl.BlockSpec((None, N_HEADS, tq, HEAD_DIM), lambda i, k: (i, 0, k, 0)),
            pl.BlockSpec((None, N_KV_HEADS, n, HEAD_DIM), lambda i, k: (i, 0, 0, 0)),
            pl.BlockSpec((None, N_KV_HEADS, n, HEAD_DIM), lambda i, k: (i, 0, 0, 0)),
            pl.BlockSpec((None, tq, LANES), lambda i, k: (i, k, 0)),
            _const_spec(agg.shape),
        ],
        out_specs=[pl.BlockSpec((None, tq, NSA_WIDTH), lambda i, k: (i, k, 0)),
                   pl.BlockSpec((None, tq, N_KV_HEADS * nsp), lambda i, k: (i, k, 0))],
        out_shape=[jax.ShapeDtypeStruct((b, t, NSA_WIDTH), F32),
                   jax.ShapeDtypeStruct((b, t, N_KV_HEADS * nsp), sel_dtype)],
        compiler_params=_params("arbitrary", "arbitrary"),
        name="cmp_attn",
    )(q_hm, kc, vc, gates, agg)


def _expand_matrix(n_keys, n_blk_pad, tk):
    e = np.zeros((n_keys // tk, n_blk_pad, tk), np.float32)
    key = np.arange(n_keys)
    e[key // tk, key // SLC_BLOCK, key % tk] = 1.0
    return jnp.asarray(e, BF16)


def _col_blocks(x, op):
    out = x[:, :LANES]
    for c in range(1, x.shape[1] // LANES):
        out = op(out, x[:, c * LANES:(c + 1) * LANES])
    return out


def _prompt_attn_kernel(q_ref, kvs_ref, kvw_ref, sel_ref, e_ref, oc_ref, gate_ref, o_ref,
                        qx_ref, qg_ref, s_ref, m_ref, l_ref, acc_ref):
    tq = q_ref.shape[1]
    tk = kvs_ref.shape[-1]
    nsp = sel_ref.shape[1] // N_KV_HEADS
    rows = GROUP * tq
    spare = s_ref.shape[0] - 1
    qt = pl.program_id(1)
    n_win = WINDOW // tk
    row = lax.broadcasted_iota(jnp.int32, (tq, tk), 0)
    col = lax.broadcasted_iota(jnp.int32, (tq, tk), 1)
    causal = jnp.where(col <= row, 0.0, NEG_INF)
    win_lo = jnp.where(col >= row, 0.0, NEG_INF)
    gates = gate_ref[...]
    oc = oc_ref[...]
    k_pad = jnp.zeros((qx_ref.shape[1] - nsp - HEAD_DIM, tk), BF16)
    qx_ref[:, nsp + HEAD_DIM:] = jnp.zeros((rows, qx_ref.shape[1] - nsp - HEAD_DIM), BF16)

    def scores_slc(g, kt):
        rhs = jnp.concatenate([e_ref[kt], kvs_ref[g, kt], k_pad], axis=0)
        return jnp.dot(qx_ref[...], rhs, preferred_element_type=F32)

    def scores_win(g, kt):
        return jnp.dot(qg_ref[...], kvw_ref[g, kt], preferred_element_type=F32)

    def find_max(scores, g, kt, slot, bias):
        s = scores(g, kt)
        if bias is not None:
            s = (s.reshape(GROUP, tq, tk) + bias[None]).reshape(rows, tk)
        s_ref[slot] = s
        m_ref[...] = jnp.maximum(m_ref[...], _col_blocks(s, jnp.maximum))

    def accumulate(v_ref, g, kt, slot):
        p = jnp.exp2(s_ref[slot] - jnp.tile(m_ref[...], (1, tk // LANES)))
        l_ref[...] += _col_blocks(p, jnp.add)
        acc_ref[...] += lax.dot_general(p.astype(BF16), v_ref[N_KV_HEADS + g, kt], _NT, preferred_element_type=F32)

    def maybe(kt, bias):
        exists = kt >= 0
        gate = jnp.where(exists, 0.0, NEG_INF)
        return jnp.maximum(kt, 0), jnp.where(exists, kt, spare), (gate if bias is None else bias + gate)

    def softmax_v(scores, v_ref, g, loop_pairs, tail):
        def both_passes(fn_pair, fn_tail):
            def body(i, carry):
                fn_pair(2 * i)
                fn_pair(2 * i + 1)
                return carry
            lax.fori_loop(0, loop_pairs, body, 0)
            for kt, slot, bias in tail:
                fn_tail(kt, slot, bias)

        m_ref[...] = jnp.full(m_ref.shape, NEG_INF, F32)
        both_passes(lambda kt: find_max(scores, g, kt, kt, None),
                    lambda kt, slot, bias: find_max(scores, g, kt, slot, bias))
        m_ref[...] = jnp.broadcast_to(jnp.max(m_ref[...], axis=-1, keepdims=True), m_ref.shape)
        l_ref[...] = jnp.zeros(l_ref.shape, F32)
        acc_ref[...] = jnp.zeros(acc_ref.shape, F32)
        both_passes(lambda kt: accumulate(v_ref, g, kt, kt),
                    lambda kt, slot, bias: accumulate(v_ref, g, kt, slot))
        return acc_ref[...] / jnp.sum(l_ref[...], axis=-1, keepdims=True)

    for g in range(N_KV_HEADS):
        sel_g = sel_ref[:, g * nsp:(g + 1) * nsp]
        for r in range(GROUP):
            qx_ref[r * tq:(r + 1) * tq, :nsp] = sel_g
            qx_ref[r * tq:(r + 1) * tq, nsp:nsp + HEAD_DIM] = q_ref[GROUP * g + r]
            qg_ref[r * tq:(r + 1) * tq, :] = q_ref[GROUP * g + r]
        odd_kt = jnp.where(qt % 2 == 1, qt - 1, -1)
        o_s = softmax_v(scores_slc, kvs_ref, g, qt // 2, [maybe(odd_kt, None), (qt, qt, causal)])
        tail = [maybe(qt - n_win, win_lo)] + [maybe(qt - d, None) for d in range(n_win - 1, 0, -1)]
        o_w = softmax_v(scores_win, kvw_ref, g, 0, tail + [(qt, qt, causal)])

        for r in range(GROUP):
            hd = GROUP * g + r
            cols = slice(hd * HEAD_DIM, (hd + 1) * HEAD_DIM)
            head = slice(r * tq, (r + 1) * tq)
            o_ref[:, cols] = (oc[:, cols] + gates[:, 3 * hd + 1:3 * hd + 2] * o_s[head]
                              + gates[:, 3 * hd + 2:3 * hd + 3] * o_w[head]).astype(o_ref.dtype)


def _prompt_attn(q_hm, kvs_b, kvw_b, sel, expand, oc, gates):
    b, _, t, _ = q_hm.shape
    tq = kvs_b.shape[-1]
    rows = GROUP * tq
    nsp = sel.shape[2] // N_KV_HEADS
    k_ext = LANES * (-(-(nsp + HEAD_DIM) // LANES))
    tile3 = lambda i, k: (i, k, 0)
    kv_spec = pl.BlockSpec((None,) + kvs_b.shape[1:], lambda i, k: (i, 0, 0, 0, 0))
    return pl.pallas_call(
        _prompt_attn_kernel,
        grid=(b, t // tq),
        in_specs=[
            pl.BlockSpec((None, N_HEADS, tq, HEAD_DIM), lambda i, k: (i, 0, k, 0)),
            kv_spec, kv_spec,
            pl.BlockSpec((None, tq, sel.shape[2]), tile3),
            _const_spec(expand.shape),
            pl.BlockSpec((None, tq, NSA_WIDTH), tile3),
            pl.BlockSpec((None, tq, LANES), tile3),
        ],
        out_specs=pl.BlockSpec((None, tq, NSA_WIDTH), tile3),
        out_shape=jax.ShapeDtypeStruct((b, t, NSA_WIDTH), BF16),
        scratch_shapes=[pltpu.VMEM((rows, k_ext), BF16), pltpu.VMEM((rows, HEAD_DIM), BF16),
                        pltpu.VMEM((t // tq + 1, rows, tq), F32),
                        pltpu.VMEM((rows, LANES), F32), pltpu.VMEM((rows, LANES), F32),
                        pltpu.VMEM((rows, HEAD_DIM), F32)],
        compiler_params=_params("arbitrary", "arbitrary"),
        name="prompt_attn",
    )(q_hm, kvs_b, kvw_b, sel, expand, oc, gates)


Q_PAD = 8


def _group_q(q_ref, g):
    return jnp.concatenate([q_ref[GROUP * g + r] for r in range(GROUP)], axis=0).astype(BF16)


def _two_piece_attention(qg, k_old, v_old, bias_old, k_new, v_new, bias_new):
    def scores(k_t, bias):
        s = jnp.dot(qg, k_t.astype(BF16), preferred_element_type=F32)
        nk = s.shape[1]
        return (s.reshape(GROUP, Q_PAD, nk) + bias[None]).reshape(GROUP * Q_PAD, nk)
    s_old = scores(k_old, bias_old)
    s_new = scores(k_new, bias_new)
    m = jnp.maximum(jnp.max(s_old, axis=-1, keepdims=True), jnp.max(s_new, axis=-1, keepdims=True))
    p_old = jnp.exp(s_old - m)
    p_new = jnp.exp(s_new - m)
    den = jnp.sum(p_old, axis=-1, keepdims=True) + jnp.sum(p_new, axis=-1, keepdims=True)
    pv = (lax.dot_general(p_old.astype(BF16), v_old.astype(BF16), _NT, preferred_element_type=F32)
          + lax.dot_general(p_new.astype(BF16), v_new.astype(BF16), _NT, preferred_element_type=F32))
    return pv / den


def _add_gated(prev_ref, gate_ref, o_ref, per_g, branch):
    prev, gates = prev_ref[...], gate_ref[...]
    for g in range(N_KV_HEADS):
        for r in range(GROUP):
            hd = GROUP * g + r
            cols = slice(hd * HEAD_DIM, (hd + 1) * HEAD_DIM)
            o_ref[:, cols] = (prev[:, cols] + gates[:, 3 * hd + branch:3 * hd + branch + 1]
                              * per_g[g][r * Q_PAD:(r + 1) * Q_PAD])


def _new_rows_bias(dec_seq):
    q = lax.broadcasted_iota(jnp.int32, (Q_PAD, LANES), 0)
    i = lax.broadcasted_iota(jnp.int32, (Q_PAD, LANES), 1) - (LANES - dec_seq)
    return jnp.where((i >= 0) & (i <= q), 0.0, NEG_INF)


def _sample_slc_kernel(pt_ref, cache_hbm, q_ref, sel_ref, new_ref, e_ref, prev_ref, gate_ref, o_ref,
                       buf, sem, *, dec_seq):
    n_pages = pt_ref.shape[1]

    def copy(page, slot, p):
        return pltpu.make_async_copy(cache_hbm.at[page],
                                     buf.at[slot, :, :, pl.ds(p * PAGE_SIZE, PAGE_SIZE)], sem.at[slot])

    def start_fetch(seq, slot):
        for p in range(n_pages):
            copy(pt_ref[seq, p], slot, p).start()

    def wait_fetch(slot):
        for p in range(n_pages):
            copy(0, slot, p).wait()

    slot = _paged_prologue(start_fetch, wait_fetch)
    nsp = sel_ref.shape[1] // N_KV_HEADS
    n_past_blk = n_pages * (PAGE_SIZE // SLC_BLOCK)
    new_bias = _new_rows_bias(dec_seq)
    per_g = []
    for g in range(N_KV_HEADS):
        sel_g = sel_ref[:, g * nsp:(g + 1) * nsp]
        bias_old = jnp.dot(sel_g.astype(BF16), e_ref[...], preferred_element_type=F32)
        bias_new = new_bias + sel_g[:, n_past_blk:n_past_blk + 1]
        per_g.append(_two_piece_attention(_group_q(q_ref, g), buf[slot, g], buf[slot, N_KV_HEADS + g], bias_old,
                                          new_ref[g], new_ref[N_KV_HEADS + g], bias_new))
    _add_gated(prev_ref, gate_ref, o_ref, per_g, 1)


def _sample_slc(page_table, cache, q_s, sel, new_t, expand, prev, gates, dec_seq):
    db, n_pages = page_table.shape
    past = n_pages * PAGE_SIZE
    seq3 = lambda i, pt: (i, 0, 0)
    seq4 = lambda i, pt: (i, 0, 0, 0)
    return pl.pallas_call(
        functools.partial(_sample_slc_kernel, dec_seq=dec_seq),
        grid_spec=pltpu.PrefetchScalarGridSpec(
            num_scalar_prefetch=1,
            grid=(db,),
            in_specs=[pl.BlockSpec(memory_space=pl.ANY),
                      pl.BlockSpec((None, N_HEADS, Q_PAD, HEAD_DIM), seq4),
                      pl.BlockSpec((None, Q_PAD, sel.shape[2]), seq3),
                      pl.BlockSpec((None, N_KV, HEAD_DIM, LANES), seq4),
                      pl.BlockSpec(expand.shape, lambda i, pt: (0, 0)),
                      pl.BlockSpec((None, Q_PAD, NSA_WIDTH), seq3),
                      pl.BlockSpec((None, Q_PAD, LANES), seq3)],
            out_specs=pl.BlockSpec((None, Q_PAD, NSA_WIDTH), seq3),
            scratch_shapes=[pltpu.VMEM((2, N_KV, HEAD_DIM, past), F32), pltpu.SemaphoreType.DMA((2,))],
        ),
        out_shape=jax.ShapeDtypeStruct((db, Q_PAD, NSA_WIDTH), F32),
        compiler_params=_params("arbitrary"),
        name="sample_slc",
    )(page_table, cache, q_s, sel, new_t, expand, prev, gates)


def _sample_win_kernel(q_ref, st_ref, new_ref, prev_ref, gate_ref, o_ref, st_out_ref, *, dec_seq):
    wb = st_ref.shape[-1]
    q = lax.broadcasted_iota(jnp.int32, (Q_PAD, wb), 0)
    i = lax.broadcasted_iota(jnp.int32, (Q_PAD, wb), 1)
    bias_old = jnp.where(wb + q - i <= WINDOW, 0.0, NEG_INF)
    new_bias = _new_rows_bias(dec_seq)
    per_g = [_two_piece_attention(_group_q(q_ref, g), st_ref[g], st_ref[N_KV_HEADS + g], bias_old,
                                  new_ref[g], new_ref[N_KV_HEADS + g], new_bias)
             for g in range(N_KV_HEADS)]
    _add_gated(prev_ref, gate_ref, o_ref, per_g, 2)
    lane = lax.broadcasted_iota(jnp.int32, (HEAD_DIM, wb), 1)
    for j in range(N_KV):
        shifted = pltpu.roll(st_ref[j], wb - dec_seq, axis=1)
        st_out_ref[j] = jnp.where(lane >= wb - dec_seq, jnp.tile(new_ref[j], (1, wb // LANES)), shifted)


def _sample_win(q_s, st_win, new_t, prev, gates, dec_seq):
    db, _, _, wb = st_win.shape
    seq3 = lambda i: (i, 0, 0)
    seq4 = lambda i: (i, 0, 0, 0)
    return pl.pallas_call(
        functools.partial(_sample_win_kernel, dec_seq=dec_seq),
        grid=(db,),
        in_specs=[pl.BlockSpec((None, N_HEADS, Q_PAD, HEAD_DIM), seq4),
                  pl.BlockSpec((None, N_KV, HEAD_DIM, wb), seq4),
                  pl.BlockSpec((None, N_KV, HEAD_DIM, LANES), seq4),
                  pl.BlockSpec((None, Q_PAD, NSA_WIDTH), seq3),
                  pl.BlockSpec((None, Q_PAD, LANES), seq3)],
        out_specs=[pl.BlockSpec((None, Q_PAD, NSA_WIDTH), seq3), pl.BlockSpec((None, N_KV, HEAD_DIM, wb), seq4)],
        out_shape=[jax.ShapeDtypeStruct((db, Q_PAD, NSA_WIDTH), F32),
                   jax.ShapeDtypeStruct((db, N_KV, HEAD_DIM, wb), F32)],
        compiler_params=_params("arbitrary"),
        name="sample_win",
    )(q_s, st_win, new_t, prev, gates)


POOL_TM = 512


def _pool_project(d_groups, wp_ref, scale_ref, o_ref):
    for gi, d in enumerate(d_groups):
        cols = slice(gi * POOL_GROUP_WIDTH, (gi + 1) * POOL_GROUP_WIDTH)
        y = jnp.dot(d.astype(BF16), wp_ref[gi], preferred_element_type=F32)
        o_ref[:, cols] = (y * scale_ref[:, cols]).astype(o_ref.dtype)


def _pool_prompt_kernel(u_ref, wp_ref, scale_ref, o_ref, ext_ref):
    tm = o_ref.shape[0]
    t0 = pl.multiple_of(pl.program_id(1) * tm, tm)
    halo = u_ref[pl.ds(pl.multiple_of(jnp.maximum(t0 - POOL_HALO, 0), POOL_HALO), POOL_HALO), :]
    ext_ref[0:POOL_HALO, :] = jnp.where(t0 > 0, halo, 0.0)
    ext_ref[POOL_HALO:, :] = u_ref[pl.ds(t0, tm), :]
    pos = t0 + lax.broadcasted_iota(jnp.int32, (tm, POOL_GROUP_WIDTH), 0)
    d_groups = []
    for gi, w in enumerate(POOL_WINDOWS):
        cols = slice(gi * POOL_GROUP_WIDTH, (gi + 1) * POOL_GROUP_WIDTH)
        e = ext_ref[:, cols]
        acc = e
        span = 1
        while span < w:
            acc = acc + pltpu.roll(acc, span, axis=0)
            span *= 2
        cnt = jnp.minimum(w, pos + 1).astype(F32)
        d_groups.append(acc[POOL_HALO:] / cnt - e[POOL_HALO:])
    _pool_project(d_groups, wp_ref, scale_ref, o_ref)


def _pool_prompt(u, wp, scale):
    b, t, pw = u.shape
    tm = min(POOL_TM, t)
    return pl.pallas_call(
        _pool_prompt_kernel,
        grid=(b, t // tm),
        in_specs=[pl.BlockSpec((None, t, pw), lambda i, k: (i, 0, 0)), _const_spec(wp.shape), _const_spec(scale.shape)],
        out_specs=pl.BlockSpec((None, tm, pw), lambda i, k: (i, k, 0)),
        out_shape=jax.ShapeDtypeStruct((b, t, pw), BF16),
        scratch_shapes=[pltpu.VMEM((POOL_HALO + tm, pw), F32)],
        compiler_params=_params("arbitrary", "arbitrary"),
        name="pool_prompt",
    )(u, wp, scale)


def _pool_sample_kernel(ext_ref, wp_ref, scale_ref, o_ref, *, past_len, dec_seq):
    db = ext_ref.shape[1]
    for q in range(dec_seq):
        d_groups = []
        for gi, w in enumerate(POOL_WINDOWS):
            cols = slice(gi * POOL_GROUP_WIDTH, (gi + 1) * POOL_GROUP_WIDTH)
            row = POOL_HALO + q
            acc = ext_ref[row, :, cols]
            for i in range(1, w):
                acc = acc + ext_ref[row - i, :, cols]
            cnt = float(min(w, past_len + q + 1))
            d_groups.append(acc / cnt - ext_ref[row, :, cols])
        _pool_project(d_groups, wp_ref, scale_ref, o_ref.at[pl.ds(q * db, db)])


def _pool_sample(ext, wp, scale, past_len, dec_seq):
    rows, db, pw = ext.shape
    return pl.pallas_call(
        functools.partial(_pool_sample_kernel, past_len=past_len, dec_seq=dec_seq),
        grid=(1,),
        in_specs=[_const_spec(ext.shape), _const_spec(wp.shape), _const_spec(scale.shape)],
        out_specs=_const_spec((dec_seq * db, pw)),
        out_shape=jax.ShapeDtypeStruct((dec_seq * db, pw), BF16),
        compiler_params=_params("arbitrary"),
        name="pool_sample",
    )(ext, wp, scale)


def _cmp_weights(w1, w2, pe):
    n_slots = CMP_BLOCK // CMP_STRIDE
    w1s = w1.reshape(n_slots, CMP_FLAT, CMP_HIDDEN)
    w1cat = jnp.concatenate([w1s[h] for h in range(n_slots)], axis=1).astype(BF16)
    pe8 = jnp.pad(pe.reshape(n_slots, CMP_FLAT), ((0, SUBLANES - n_slots), (0, 0)))
    return w1cat, w2.astype(BF16), pe8


def _pad_axis(x, axis, size, front=False):
    pad = [(0, 0)] * x.ndim
    extra = size - x.shape[axis]
    pad[axis] = (extra, 0) if front else (0, extra)
    return jnp.pad(x, pad)


def _rows_to_state(x_fm):
    b, _, _, t = x_fm.shape
    return x_fm.reshape(b, 2, N_KV_HEADS, HEAD_DIM, t).transpose(0, 4, 1, 2, 3)


def _state_to_fm(x):
    b, r = x.shape[:2]
    return x.transpose(0, 2, 3, 4, 1).reshape(b, N_KV, HEAD_DIM, r)


def kernel(x_prompt, x_sample, cache_kv_cmp, cache_kv_slc, page_table, state_kv_win, state_pool, n_ffn1, w_ffn1_gate, w_ffn1_up, w_ffn1_down, n_mix, w_in, w_cmp_k1, w_cmp_k2, pe_cmp_k, w_cmp_v1, w_cmp_v2, pe_cmp_v, w_pool, pool_scale, w_out, n_ffn2, w_ffn2_gate, w_ffn2_up, w_ffn2_down, n_final):
    b, t, d = x_prompt.shape
    db, ds, _ = x_sample.shape
    depth = w_in.shape[0]
    n_pages = page_table.shape[1]
    past = n_pages * PAGE_SIZE
    wb = state_kv_win.shape[2]
    assert t % ATTN_TQ == 0 and t % INPROJ_TM == 0 and WINDOW % ATTN_TQ == 0 and t >= WINDOW
    assert ds <= Q_PAD and ds <= CMP_STRIDE and wb == WINDOW and past % SLC_BLOCK == 0
    assert (t // CMP_STRIDE) % CMP_ROWS == 0 and (db * ds) % SUBLANES == 0 and n_pages % CMP_PAGE_GROUP == 0

    xp = x_prompt.reshape(b * t, d)
    xs = x_sample.reshape(db * ds, d)
    pos_p = jnp.arange(t, dtype=jnp.int32)
    pos_s = past + jnp.arange(ds, dtype=jnp.int32)
    tab_p = _rope_tables(pos_p)
    tab_s = tuple(jnp.tile(a, (db, 1)) for a in _rope_tables(pos_s)[:3]) + tuple(
        jnp.tile(a, (1, db)) for a in _rope_tables(pos_s)[3:])

    n_chunk_p = t // CMP_STRIDE
    n_slc_p = -(-t // SLC_BLOCK)
    agg_p = _agg_matrix(n_chunk_p, n_chunk_p - 1, n_slc_p, LANES * (-(-n_slc_p // LANES)))
    expand_p = _expand_matrix(t, agg_p.shape[1], ATTN_TQ)
    n_chunk_s = past // CMP_STRIDE
    n_slc_s = -(-(past + ds) // SLC_BLOCK)
    agg_s = _agg_matrix(n_chunk_s, n_chunk_s, n_slc_s, LANES * (-(-n_slc_s // LANES)))
    expand_s = _expand_matrix(past, agg_s.shape[1], past)[0]

    st_p = ([], [], [], [])
    st_s = ([], [], [], [])
    for l in range(depth):
        last = l == depth - 1
        ffn1 = (n_ffn1[l], w_ffn1_gate[l].astype(BF16), w_ffn1_up[l].astype(BF16), w_ffn1_down[l].astype(BF16))
        ffn2 = (n_ffn2[l], w_ffn2_gate[l].astype(BF16), w_ffn2_up[l].astype(BF16), w_ffn2_down[l].astype(BF16))
        w = w_in[l]
        o_kv, o_gate, o_pool = NSA_WIDTH, NSA_WIDTH + 3 * KV_WIDTH, NSA_WIDTH + 3 * KV_WIDTH + N_GATES
        wq = w[:, :o_kv].astype(BF16)
        wkv_t = w[:, o_kv:o_gate].T.astype(BF16)
        wkc = w[:, o_kv:o_kv + KV_WIDTH].astype(BF16)
        wgt = _pad_axis(w[:, o_gate:o_pool], 1, LANES).astype(BF16)
        wu = w[:, o_pool:].astype(BF16)
        proj = (n_mix[l], wq, wkv_t, wkc, wgt, wu)
        wk1, wk2, pek = _cmp_weights(w_cmp_k1[l], w_cmp_k2[l], pe_cmp_k[l])
        wv1, wv2, pev = _cmp_weights(w_cmp_v1[l], w_cmp_v2[l], pe_cmp_v[l])
        pe = jnp.stack([pek, pev])
        wp = w_pool[l].astype(BF16)
        scale = pool_scale[l].reshape(1, POOL_WIDTH)
        wo_nsa = w_out[l][:NSA_WIDTH].astype(BF16)
        wo_pool = w_out[l][NSA_WIDTH:].astype(BF16)

        xp = _ffn(xp, *ffn1)
        q_hm, q2_hm, kvc_rm, kvc, kvs, kvw, kvs_b, kvw_b, gates, u = _inproj(xp, *proj, tab_p, BF16)
        kc, vc = _compress_prompt(kvc_rm.reshape(b, n_chunk_p, CHUNK_WIDTH), pe, wk1, wv1, wk2, wv2)
        gates3 = gates.reshape(b, t, LANES)
        oc, sel = _cmp_attn(q_hm, kc, vc, gates3, agg_p, tq=ATTN_TQ, pos0=0, tiled=True,
                            n_slc=n_slc_p, sel_dtype=BF16)
        o_mix = _prompt_attn(q2_hm, kvs_b, kvw_b, sel, expand_p, oc, gates3)
        u3 = u.reshape(b, t, POOL_WIDTH)
        pool_out = _pool_prompt(u3, wp, scale)
        mix = (o_mix.reshape(b * t, NSA_WIDTH), pool_out.reshape(b * t, POOL_WIDTH), wo_nsa, wo_pool)
        xp = _ffn(xp, *ffn2, g_final=n_final if last else None, mix=mix)
        st_p[0].append(_rows_to_state(kvc))
        st_p[1].append(_rows_to_state(kvs))
        st_p[2].append(_rows_to_state(kvw[..., t - min(WINDOW, t):]))
        st_p[3].append(u3[:, t - POOL_STATE:])

        xs = _ffn(xs, *ffn1)
        q, _, _, kvc, kvs, kvw, _, _, gates, u = _inproj(xs, *proj, tab_s, F32)
        per_seq = lambda a: a.reshape(N_KV, HEAD_DIM, db, ds).transpose(2, 0, 1, 3)
        kvc_n, kvs_n, kvw_n = per_seq(kvc), per_seq(kvs), per_seq(kvw)
        ynew = _pad_axis(kvc_n.transpose(0, 1, 3, 2), 2, CMP_STRIDE).reshape(db, N_KV, CMP_FLAT)
        cache_c = _state_to_fm(cache_kv_cmp[l])
        kc, vc = _compress_sample(page_table, cache_c, _pad_axis(ynew, 1, SUBLANES), pe, wk1, wv1, wk2, wv2)
        q_s = _pad_axis(q.reshape(N_HEADS, db, ds, HEAD_DIM).transpose(1, 0, 2, 3), 2, Q_PAD)
        gates_s = _pad_axis(gates.reshape(db, ds, LANES), 1, Q_PAD)
        oc, sel = _cmp_attn(q_s, kc, vc, gates_s, agg_s, tq=Q_PAD, pos0=past, tiled=False,
                            n_slc=n_slc_s, sel_dtype=F32)
        o_cs = _sample_slc(page_table, _state_to_fm(cache_kv_slc[l]), q_s, sel,
                           _pad_axis(kvs_n, 3, LANES, front=True), expand_s, oc, gates_s, ds)
        o_mix, st_win_new = _sample_win(q_s, _state_to_fm(state_kv_win[l]),
                                        _pad_axis(kvw_n, 3, LANES, front=True), o_cs, gates_s, ds)
        u3 = u.reshape(db, ds, POOL_WIDTH)
        ext = jnp.concatenate([jnp.zeros((db, POOL_HALO - POOL_STATE, POOL_WIDTH), F32), state_pool[l], u3], axis=1)
        pool_out = _pool_sample(ext.transpose(1, 0, 2), wp, scale, past, ds)
        pool_out = pool_out.reshape(ds, db, POOL_WIDTH).transpose(1, 0, 2).reshape(db * ds, POOL_WIDTH)
        mix = (o_mix[:, :ds].reshape(db * ds, NSA_WIDTH), pool_out, wo_nsa, wo_pool)
        xs = _ffn(xs, *ffn2, g_final=n_final if last else None, mix=mix)
        st_s[0].append(_rows_to_state(kvc_n))
        st_s[1].append(_rows_to_state(kvs_n))
        st_s[2].append(_rows_to_state(st_win_new))
        st_s[3].append(jnp.concatenate([state_pool[l], u3], axis=1)[:, ds:])

    return (xp.reshape(b, t, d), xs.reshape(db, ds, d),
            jnp.stack(st_p[0]), jnp.stack(st_p[1]), jnp.stack(st_p[2]), jnp.stack(st_p[3]),
            jnp.stack(st_s[0]), jnp.stack(st_s[1]), jnp.stack(st_s[2]), jnp.stack(st_s[3]))
```

```python
import functools

import numpy as np
import jax
import jax.numpy as jnp
from jax import lax
from jax.experimental import pallas as pl
from jax.experimental.pallas import tpu as pltpu

F32 = jnp.float32
BF16 = jnp.bfloat16

HEAD_DIM = 64
N_HEADS = 8
N_KV_HEADS = 2
GROUP = N_HEADS // N_KV_HEADS
NSA_WIDTH = N_HEADS * HEAD_DIM
N_KV = 2 * N_KV_HEADS
KV_WIDTH = N_KV * HEAD_DIM
ROPE_DIM = HEAD_DIM // 4
ROPE_HALF = ROPE_DIM // 2
ROPE_THETA = 500000.0
CMP_BLOCK = 32
CMP_STRIDE = 16
CMP_HIDDEN = 256
CMP_FLAT = CMP_STRIDE * HEAD_DIM
SLC_BLOCK = 64
TOPK_BLOCKS = 16
WINDOW = 512
PAGE_SIZE = 128
POOL_WINDOWS = (2, 4, 8, 16)
POOL_GROUP_WIDTH = 128
POOL_WIDTH = POOL_GROUP_WIDTH * len(POOL_WINDOWS)
POOL_STATE = max(POOL_WINDOWS) - 1
POOL_HALO = 16
RMS_EPS = 1e-6
FORCE_SCORE = 1e4
NEG_INF = -1e30
ATTN_SCALE = HEAD_DIM ** -0.5
LOG2_E = float(np.log2(np.e))
N_GATES = 3 * N_HEADS
CHUNK_WIDTH = CMP_STRIDE * KV_WIDTH

LANES = 128
SUBLANES = 8
VMEM_LIMIT_BYTES = 56 * 1024 * 1024

_NT = (((1,), (1,)), ((), ()))


def _params(*sem):
    return pltpu.CompilerParams(dimension_semantics=sem, vmem_limit_bytes=VMEM_LIMIT_BYTES)


def _rmsnorm(x, g):
    return x * lax.rsqrt(jnp.mean(x * x, axis=-1, keepdims=True) + RMS_EPS) * g


def _const_spec(shape):
    n = len(shape)
    return pl.BlockSpec(shape, lambda *_: (0,) * n)


FFN_TM = 512
FFN_CHUNK = 256


def _ffn_kernel(x_ref, g_ref, wg_ref, wu_ref, wd_ref, gf_ref, *rest, final_norm, mix):
    if mix:
        mo_ref, mp_ref, wo_ref, wp_ref, o_ref, act_ref = rest
        x = (x_ref[...] + jnp.dot(mo_ref[...].astype(BF16), wo_ref[...], preferred_element_type=F32)
             + jnp.dot(mp_ref[...], wp_ref[...], preferred_element_type=F32))
    else:
        o_ref, act_ref = rest
        x = x_ref[...]
    h = _rmsnorm(x, g_ref[...]).astype(BF16)
    d_ff = wg_ref.shape[1]
    for c in range(d_ff // FFN_CHUNK):
        sl = slice(c * FFN_CHUNK, (c + 1) * FFN_CHUNK)
        a = jnp.dot(h, wg_ref[:, sl], preferred_element_type=F32)
        u = jnp.dot(h, wu_ref[:, sl], preferred_element_type=F32)
        act_ref[:, sl] = (a * jax.nn.sigmoid(a) * u).astype(BF16)
    y = x + 0.5 * jnp.dot(act_ref[...], wd_ref[...], preferred_element_type=F32)
    if final_norm:
        y = _rmsnorm(y, gf_ref[...])
    o_ref[...] = y


def _ffn(x, g, wg, wu, wd, g_final=None, mix=None):
    m, d = x.shape
    d_ff = wg.shape[1]
    tm = min(FFN_TM, m)
    final_norm = g_final is not None
    gf = g_final if final_norm else g
    row = lambda i: (i, 0)
    once = lambda shape: pl.BlockSpec(shape, lambda i: (0, 0), pipeline_mode=pl.Buffered(1))
    in_specs = [pl.BlockSpec((tm, d), row), _const_spec((1, d)),
                once((d, d_ff)), once((d, d_ff)), once((d_ff, d)), _const_spec((1, d))]
    args = [x, g.reshape(1, d), wg, wu, wd, gf.reshape(1, d)]
    if mix is not None:
        mo, mp, wo, wp = mix
        in_specs += [pl.BlockSpec((tm, mo.shape[1]), row), pl.BlockSpec((tm, mp.shape[1]), row),
                     once(wo.shape), once(wp.shape)]
        args += [mo, mp, wo, wp]
    return pl.pallas_call(
        functools.partial(_ffn_kernel, final_norm=final_norm, mix=mix is not None),
        grid=(m // tm,),
        in_specs=in_specs,
        out_specs=pl.BlockSpec((tm, d), row),
        out_shape=jax.ShapeDtypeStruct((m, d), F32),
        scratch_shapes=[pltpu.VMEM((tm, d_ff), BF16)],
        compiler_params=_params("arbitrary"),
        name="ffn",
    )(*args)


INPROJ_TM = 512
ATTN_TQ = 256


def _rope_tables(pos):
    inv = jnp.power(ROPE_THETA, -jnp.arange(ROPE_HALF, dtype=F32) / ROPE_HALF)
    ang = pos.astype(F32)[:, None] * inv[None, :]
    cos, sin = jnp.cos(ang), jnp.sin(ang)
    t = pos.shape[0]
    rest = HEAD_DIM - ROPE_DIM
    z_half = jnp.zeros((t, ROPE_HALF), F32)
    z_rest = jnp.zeros((t, rest), F32)
    cos_t = jnp.concatenate([cos, cos, jnp.ones((t, rest), F32)], axis=1)
    sin_a = jnp.concatenate([-sin, z_half, z_rest], axis=1)
    sin_b = jnp.concatenate([z_half, sin, z_rest], axis=1)
    rep = LANES // HEAD_DIM
    return tuple(jnp.tile(a, (1, rep)) for a in (cos_t, sin_a, sin_b)) + (cos.T, sin.T)


def _rope(z, cos_t, sin_a, sin_b):
    outs = []
    for c in range(z.shape[1] // LANES):
        zc = z[:, c * LANES:(c + 1) * LANES]
        outs.append(zc * cos_t + pltpu.roll(zc, LANES - ROPE_HALF, axis=1) * sin_a
                    + pltpu.roll(zc, ROPE_HALF, axis=1) * sin_b)
    return outs[0] if len(outs) == 1 else jnp.concatenate(outs, axis=1)


def _rope_fm(kt, cos, sin):
    x1, x2 = kt[0:ROPE_HALF], kt[ROPE_HALF:ROPE_DIM]
    return jnp.concatenate([x1 * cos - x2 * sin, x2 * cos + x1 * sin, kt[ROPE_DIM:]], axis=0)


def _inproj_kernel(x_ref, g_ref, wq_ref, wkv_t_ref, wkc_ref, wgt_ref, wu_ref,
                   cos_ref, sa_ref, sb_ref, cos_fm_ref, sin_fm_ref,
                   q_ref, q2_ref, kvc_rm_ref, kvc_ref, kvs_ref, kvw_ref, kvs_b_ref, kvw_b_ref, gate_ref, u_ref):
    h = _rmsnorm(x_ref[...], g_ref[...]).astype(BF16)
    tm = h.shape[0]
    tk = kvs_b_ref.shape[-1]
    cos_t, sin_a, sin_b = cos_ref[...], sa_ref[...], sb_ref[...]
    cos_fm, sin_fm = cos_fm_ref[...], sin_fm_ref[...]
    q = _rope(jnp.dot(h, wq_ref[...], preferred_element_type=F32), cos_t, sin_a, sin_b) * ATTN_SCALE
    q2 = q * LOG2_E
    for hd in range(N_HEADS):
        q_ref[hd] = q[:, hd * HEAD_DIM:(hd + 1) * HEAD_DIM].astype(q_ref.dtype)
        q2_ref[hd] = q2[:, hd * HEAD_DIM:(hd + 1) * HEAD_DIM].astype(BF16)
    k_width = N_KV_HEADS * HEAD_DIM
    kvc = jnp.dot(h, wkc_ref[...], preferred_element_type=F32)
    kvc_rm_ref[...] = jnp.concatenate([_rope(kvc[:, :k_width], cos_t, sin_a, sin_b), kvc[:, k_width:]], axis=1)
    kv_t = lax.dot_general(wkv_t_ref[...], h, _NT, preferred_element_type=F32)
    for i, (f_ref, b_ref) in enumerate(((kvc_ref, None), (kvs_ref, kvs_b_ref), (kvw_ref, kvw_b_ref))):
        for j in range(N_KV):
            r0 = i * KV_WIDTH + j * HEAD_DIM
            blk = kv_t[r0:r0 + HEAD_DIM]
            if j < N_KV_HEADS:
                blk = _rope_fm(blk, cos_fm, sin_fm)
            f_ref[j] = blk
            if b_ref is not None:
                for c in range(tm // tk):
                    b_ref[j, c] = blk[:, c * tk:(c + 1) * tk].astype(BF16)
    gate_ref[...] = jax.nn.sigmoid(jnp.dot(h, wgt_ref[...], preferred_element_type=F32))
    u_ref[...] = jnp.dot(h, wu_ref[...], preferred_element_type=F32)


def _inproj(x, g, wq, wkv_t, wkc, wgt, wu, tables, q_dtype):
    m, d = x.shape
    seq = tables[0].shape[0]
    nseq = m // seq
    tm = min(INPROJ_TM, seq)
    tk = min(ATTN_TQ, tm)
    n_tab = seq // tm
    row = lambda i: (i, 0)
    tab = lambda i: (i % n_tab, 0)
    tab_fm = lambda i: (0, i % n_tab)
    hm = lambda i: (i // n_tab, 0, i % n_tab, 0)
    fm = lambda i: (i // n_tab, 0, 0, i % n_tab)
    fmb = lambda i: (i // n_tab, 0, i % n_tab, 0, 0)
    fm_shape = jax.ShapeDtypeStruct((nseq, N_KV, HEAD_DIM, seq), F32)
    fmb_shape = jax.ShapeDtypeStruct((nseq, N_KV, seq // tk, HEAD_DIM, tk), BF16)
    fm_spec = pl.BlockSpec((None, N_KV, HEAD_DIM, tm), fm)
    fmb_spec = pl.BlockSpec((None, N_KV, tm // tk, HEAD_DIM, tk), fmb)
    return pl.pallas_call(
        _inproj_kernel,
        grid=(m // tm,),
        in_specs=[
            pl.BlockSpec((tm, d), row), _const_spec((1, d)),
            _const_spec(wq.shape), _const_spec(wkv_t.shape), _const_spec(wkc.shape),
            _const_spec(wgt.shape), _const_spec(wu.shape),
            pl.BlockSpec((tm, LANES), tab), pl.BlockSpec((tm, LANES), tab), pl.BlockSpec((tm, LANES), tab),
            pl.BlockSpec((ROPE_HALF, tm), tab_fm), pl.BlockSpec((ROPE_HALF, tm), tab_fm),
        ],
        out_specs=[
            pl.BlockSpec((None, N_HEADS, tm, HEAD_DIM), hm), pl.BlockSpec((None, N_HEADS, tm, HEAD_DIM), hm),
            pl.BlockSpec((tm, KV_WIDTH), row),
            fm_spec, fm_spec, fm_spec, fmb_spec, fmb_spec,
            pl.BlockSpec((tm, LANES), row), pl.BlockSpec((tm, POOL_WIDTH), row),
        ],
        out_shape=[
            jax.ShapeDtypeStruct((nseq, N_HEADS, seq, HEAD_DIM), q_dtype),
            jax.ShapeDtypeStruct((nseq, N_HEADS, seq, HEAD_DIM), BF16),
            jax.ShapeDtypeStruct((m, KV_WIDTH), F32),
            fm_shape, fm_shape, fm_shape, fmb_shape, fmb_shape,
            jax.ShapeDtypeStruct((m, LANES), F32), jax.ShapeDtypeStruct((m, POOL_WIDTH), F32),
        ],
        compiler_params=_params("arbitrary"),
        name="inproj",
    )(x, g.reshape(1, d), wq, wkv_t, wkc, wgt, wu, *tables)


CMP_ROWS = 128


def _gelu_tanh(x):
    return 0.5 * x * (1.0 + jnp.tanh(np.sqrt(2.0 / np.pi).astype(np.float32) * (x + 0.044715 * (x * x * x))))


def _interleave_heads(a0, a1):
    low = lax.broadcasted_iota(jnp.int32, a0.shape, 1) < HEAD_DIM
    return (jnp.where(low, a0, pltpu.roll(a1, HEAD_DIM, axis=1)),
            jnp.where(low, pltpu.roll(a0, HEAD_DIM, axis=1), a1))


def _pick_row(blk, r):
    row8 = lax.broadcasted_iota(jnp.int32, (SUBLANES, 1), 0)
    return jnp.sum(jnp.where(row8 == r, blk, 0.0), axis=0, keepdims=True)


def _compress_finish(kv, n, new8, pe_ref, w1_ref, w2_ref, out_ref, y_ref, has_new):
    y_ref[pl.ds(2 * n, 2 * SUBLANES), :] = jnp.concatenate([new8, pe_ref[kv]], axis=0).astype(BF16)
    p = jnp.dot(y_ref[...], w1_ref[...], preferred_element_type=F32)
    p_new, p_pe = p[2 * n:2 * n + SUBLANES], p[2 * n + SUBLANES:2 * n + 2 * SUBLANES]
    bias = _pick_row(p_pe[:, :CMP_HIDDEN], 0) + _pick_row(p_pe[:, CMP_HIDDEN:], 1)
    last = lax.broadcasted_iota(jnp.int32, (n, CMP_HIDDEN), 0) == n - 1
    for g in range(N_KV_HEADS):
        slot0 = p[g * n:(g + 1) * n, :CMP_HIDDEN]
        slot1 = pltpu.roll(p[g * n:(g + 1) * n, CMP_HIDDEN:], n - 1, axis=0)
        if has_new:
            slot1 = jnp.where(last, _pick_row(p_new[:, CMP_HIDDEN:], g), slot1)
        hid = _gelu_tanh(slot0 + slot1 + bias).astype(BF16)
        out_ref[g] = jnp.dot(hid, w2_ref[...], preferred_element_type=F32).astype(BF16)


def _compress_prompt_kernel(x_ref, pe_ref, wk1_ref, wv1_ref, wk2_ref, wv2_ref, kc_ref, vc_ref, y_ref):
    n = x_ref.shape[0]
    for kv, (w1_ref, w2_ref, out_ref) in enumerate(((wk1_ref, wk2_ref, kc_ref), (wv1_ref, wv2_ref, vc_ref))):
        def fill(r, carry):
            r0 = pl.multiple_of(r * CMP_ROWS, CMP_ROWS)
            for j in range(CMP_STRIDE // 2):
                c0 = (2 * j) * KV_WIDTH + kv * LANES
                c1 = (2 * j + 1) * KV_WIDTH + kv * LANES
                y0, y1 = _interleave_heads(x_ref[pl.ds(r0, CMP_ROWS), c0:c0 + LANES],
                                           x_ref[pl.ds(r0, CMP_ROWS), c1:c1 + LANES])
                y_ref[pl.ds(r0, CMP_ROWS), j * LANES:(j + 1) * LANES] = y0.astype(BF16)
                y_ref[pl.ds(n + r0, CMP_ROWS), j * LANES:(j + 1) * LANES] = y1.astype(BF16)
            return carry
        lax.fori_loop(0, n // CMP_ROWS, fill, 0)
        _compress_finish(kv, n, jnp.zeros((SUBLANES, CMP_FLAT), F32), pe_ref, w1_ref, w2_ref, out_ref, y_ref, False)


def _compress_prompt(x, pe, wk1, wv1, wk2, wv2):
    b, n, cw = x.shape
    out = jax.ShapeDtypeStruct((b, N_KV_HEADS, n, HEAD_DIM), BF16)
    out_spec = pl.BlockSpec((None, N_KV_HEADS, n, HEAD_DIM), lambda i: (i, 0, 0, 0))
    return pl.pallas_call(
        _compress_prompt_kernel,
        grid=(b,),
        in_specs=[pl.BlockSpec((None, n, cw), lambda i: (i, 0, 0)), _const_spec(pe.shape),
                  _const_spec(wk1.shape), _const_spec(wv1.shape), _const_spec(wk2.shape), _const_spec(wv2.shape)],
        out_specs=[out_spec, out_spec],
        out_shape=[out, out],
        scratch_shapes=[pltpu.VMEM((2 * n + 2 * SUBLANES, CMP_FLAT), BF16)],
        compiler_params=_params("arbitrary"),
        name="compress_prompt",
    )(x, pe, wk1, wv1, wk2, wv2)


def _paged_prologue(start_fetch, wait_fetch):
    b = pl.program_id(0)
    slot = lax.rem(b, 2)

    @pl.when(b == 0)
    def _():
        start_fetch(0, 0)

    @pl.when(b + 1 < pl.num_programs(0))
    def _():
        start_fetch(b + 1, 1 - slot)

    wait_fetch(slot)
    return slot


CMP_PAGE_GROUP = 8


def _dechunk_perm():
    perm = np.zeros((N_KV_HEADS * PAGE_SIZE, 2 * LANES), np.float32)
    cpp = PAGE_SIZE // CMP_STRIDE
    for g in range(N_KV_HEADS):
        for c in range(cpp):
            for j in range(CMP_STRIDE // 2):
                for par in range(2):
                    perm[g * PAGE_SIZE + CMP_STRIDE * c + 2 * j + par, par * LANES + g * HEAD_DIM + j * cpp + c] = 1.0
    return jnp.asarray(perm, BF16)


def _compress_sample_kernel(pt_ref, cache_hbm, ynew_ref, pe_ref, perm_ref, wk1_ref, wv1_ref, wk2_ref, wv2_ref,
                            kc_ref, vc_ref, buf, sem, y_ref):
    n_pages = pt_ref.shape[1]
    n = n_pages * (PAGE_SIZE // CMP_STRIDE)

    def copy(page, slot, p):
        return pltpu.make_async_copy(cache_hbm.at[page], buf.at[slot, p], sem.at[slot])

    def start_fetch(seq, slot):
        for p in range(n_pages):
            copy(pt_ref[seq, p], slot, p).start()

    def wait_fetch(slot):
        for p in range(n_pages):
            copy(0, slot, p).wait()

    slot = _paged_prologue(start_fetch, wait_fetch)
    ynew = ynew_ref[...]
    row8 = lax.broadcasted_iota(jnp.int32, (SUBLANES, 1), 0)
    pg = CMP_PAGE_GROUP
    cpp = PAGE_SIZE // CMP_STRIDE
    for kv, (w1_ref, w2_ref, out_ref) in enumerate(((wk1_ref, wk2_ref, kc_ref), (wv1_ref, wv2_ref, vc_ref))):
        for grp in range(n_pages // pg):
            lhs = jnp.concatenate(
                [jnp.concatenate([buf[slot, grp * pg + q, N_KV_HEADS * kv + g] for g in range(N_KV_HEADS)], axis=1)
                 for q in range(pg)], axis=0).astype(BF16)
            out = jnp.dot(lhs, perm_ref[...], preferred_element_type=F32)
            x = jnp.concatenate(
                [jnp.concatenate([out[q * HEAD_DIM:(q + 1) * HEAD_DIM, :LANES],
                                  out[q * HEAD_DIM:(q + 1) * HEAD_DIM, LANES:]], axis=0) for q in range(pg)], axis=1)
            xt = x.T
            r0 = grp * (pg * cpp)
            for g in range(N_KV_HEADS):
                for j in range(CMP_STRIDE // 2):
                    piece = jnp.concatenate(
                        [xt[q * LANES + g * HEAD_DIM + j * cpp:q * LANES + g * HEAD_DIM + (j + 1) * cpp]
                         for q in range(pg)], axis=0)
                    y_ref[kv, g * n + r0:g * n + r0 + pg * cpp, j * LANES:(j + 1) * LANES] = piece.astype(BF16)
        new8 = jnp.where(row8 == 0, _pick_row(ynew, N_KV_HEADS * kv),
                         jnp.where(row8 == 1, _pick_row(ynew, N_KV_HEADS * kv + 1), 0.0))
        _compress_finish(kv, n, new8, pe_ref, w1_ref, w2_ref, out_ref, y_ref.at[kv], True)


def _compress_sample(page_table, cache, ynew, pe, wk1, wv1, wk2, wv2):
    db, n_pages = page_table.shape
    n = n_pages * (PAGE_SIZE // CMP_STRIDE)
    out = jax.ShapeDtypeStruct((db, N_KV_HEADS, n, HEAD_DIM), BF16)
    out_spec = pl.BlockSpec((None, N_KV_HEADS, n, HEAD_DIM), lambda i, pt: (i, 0, 0, 0))
    const = lambda s: pl.BlockSpec(s, lambda i, pt: (0,) * len(s))
    perm = _dechunk_perm()
    return pl.pallas_call(
        _compress_sample_kernel,
        grid_spec=pltpu.PrefetchScalarGridSpec(
            num_scalar_prefetch=1,
            grid=(db,),
            in_specs=[pl.BlockSpec(memory_space=pl.ANY),
                      pl.BlockSpec((None, SUBLANES, CMP_FLAT), lambda i, pt: (i, 0, 0)),
                      const(pe.shape), const(perm.shape),
                      const(wk1.shape), const(wv1.shape), const(wk2.shape), const(wv2.shape)],
            out_specs=[out_spec, out_spec],
            scratch_shapes=[pltpu.VMEM((2, n_pages, N_KV, HEAD_DIM, PAGE_SIZE), F32),
                            pltpu.SemaphoreType.DMA((2,)),
                            pltpu.VMEM((2, 2 * n + 2 * SUBLANES, CMP_FLAT), BF16)],
        ),
        out_shape=[out, out],
        compiler_params=_params("arbitrary"),
        name="compress_sample",
    )(page_table, cache, ynew, pe, perm, wk1, wv1, wk2, wv2)


def _select_mask(p_slc, t_pos, j, n_slc):
    cur = t_pos // SLC_BLOCK
    forced = (j == 0) | (j == cur) | (j == cur - 1)
    real = j < n_slc
    score = jnp.where(forced, FORCE_SCORE, jnp.where(j <= cur, p_slc, NEG_INF))
    score = jnp.where(real, score, -jnp.inf)
    rank = jnp.zeros(score.shape, jnp.int32)
    for jp in range(n_slc):
        c = score[jp:jp + 1, :]
        rank = rank + ((c > score) | ((c == score) & (j > jp))).astype(jnp.int32)
    return jnp.where((rank < TOPK_BLOCKS) & real, 0.0, NEG_INF)


def _select_all_kernel(p_ref, o_ref, *, pos0, q_pad, n_slc):
    nsp, lanes = p_ref.shape
    n_rows = SUBLANES * (-(-n_slc // SUBLANES))
    t_pos = pos0 + lax.rem(lax.broadcasted_iota(jnp.int32, (n_rows, lanes), 1), q_pad)
    j = lax.broadcasted_iota(jnp.int32, (n_rows, lanes), 0)
    o_ref[0:n_rows, :] = _select_mask(p_ref[0:n_rows, :], t_pos, j, n_slc)
    if n_rows < nsp:
        o_ref[n_rows:, :] = jnp.zeros((nsp - n_rows, lanes), F32)


def _select_all(p_slc, *, pos0, n_slc):
    db, q_pad, width = p_slc.shape
    nsp = width // N_KV_HEADS
    p_t = p_slc.reshape(db, q_pad, N_KV_HEADS, nsp).transpose(3, 0, 2, 1).reshape(nsp, db * N_KV_HEADS * q_pad)
    mask_t = pl.pallas_call(
        functools.partial(_select_all_kernel, pos0=pos0, q_pad=q_pad, n_slc=n_slc),
        grid=(1,),
        in_specs=[_const_spec(p_t.shape)],
        out_specs=_const_spec(p_t.shape),
        out_shape=jax.ShapeDtypeStruct(p_t.shape, F32),
        compiler_params=_params("arbitrary"),
        name="select_all",
    )(p_t)
    return mask_t.reshape(nsp, db, N_KV_HEADS, q_pad).transpose(1, 3, 2, 0).reshape(db, q_pad, width)


CMP_ATTN_SEQS = 4


def _cmp_attn_kernel(q_ref, kc_ref, vc_ref, gate_ref, agg_ref, oc_ref, sel_ref, **static):
    results = [_cmp_attn_one(q_ref.at[i], kc_ref.at[i], vc_ref.at[i], gate_ref.at[i], agg_ref, **static)
               for i in range(q_ref.shape[0])]
    for i, (heads, masks) in enumerate(results):
        nsp = masks[0].shape[1]
        for g, mask in enumerate(masks):
            sel_ref[i, :, g * nsp:(g + 1) * nsp] = mask.astype(sel_ref.dtype)
        for hd, o in enumerate(heads):
            oc_ref[i, :, hd * HEAD_DIM:(hd + 1) * HEAD_DIM] = o


def _cmp_attn_one(q_ref, kc_ref, vc_ref, gate_ref, agg_ref, *, pos0, tiled, n_slc, blocks_major):
    tq = q_ref.shape[1]
    n = kc_ref.shape[1]
    nsp = agg_ref.shape[0] if blocks_major else agg_ref.shape[1]
    t0 = pos0 + (pl.program_id(1) * tq if tiled else 0)
    t_n = t0 + lax.broadcasted_iota(jnp.int32, (tq, n), 0)
    blk_end = lax.broadcasted_iota(jnp.int32, (tq, n), 1) * CMP_STRIDE + (CMP_BLOCK - 1)
    valid = (blk_end <= t_n)[None]
    if blocks_major:
        n_rows = SUBLANES * (-(-n_slc // SUBLANES))
        t_s = t0 + lax.broadcasted_iota(jnp.int32, (n_rows, tq), 1)
        j = lax.broadcasted_iota(jnp.int32, (n_rows, tq), 0)
    gates = gate_ref[...]
    heads, masks = [], []
    for g in range(N_KV_HEADS):
        qg = jnp.concatenate([q_ref[GROUP * g + r].astype(F32) for r in range(GROUP)], axis=0).astype(BF16)
        s = lax.dot_general(qg, kc_ref[g], _NT, preferred_element_type=F32).reshape(GROUP, tq, n)
        s = jnp.where(valid, s, NEG_INF)
        e = jnp.where(valid, jnp.exp(s - jnp.max(s, axis=-1, keepdims=True)), 0.0)
        p = e / jnp.maximum(jnp.sum(e, axis=-1, keepdims=True), 1e-30)
        o = jnp.dot(p.reshape(GROUP * tq, n).astype(BF16), vc_ref[g], preferred_element_type=F32)
        p_grp = jnp.sum(p, axis=0)
        p_hi = p_grp.astype(BF16)
        p_lo = (p_grp - p_hi.astype(F32)).astype(BF16)
        if blocks_major:
            p_slc = (lax.dot_general(agg_ref[...], p_hi, _NT, preferred_element_type=F32)
                     + lax.dot_general(agg_ref[...], p_lo, _NT, preferred_element_type=F32))
            mask = _select_mask(p_slc[:n_rows], t_s, j, n_slc)
            if n_rows < nsp:
                mask = jnp.concatenate([mask, jnp.zeros((nsp - n_rows, tq), F32)], axis=0)
            mask = mask.T
        else:
            mask = (jnp.dot(p_hi, agg_ref[...], preferred_element_type=F32)
                    + jnp.dot(p_lo, agg_ref[...], preferred_element_type=F32))
        masks.append(mask)
        for r in range(GROUP):
            hd = GROUP * g + r
            heads.append(gates[:, 3 * hd:3 * hd + 1] * o[r * tq:(r + 1) * tq])
    return heads, masks


def _agg_matrix(n_cmp_pad, n_cmp, n_slc, n_slc_pad):
    c0 = np.arange(n_cmp)[:, None] * CMP_STRIDE
    s0 = np.arange(n_slc)[None, :] * SLC_BLOCK
    overlap = np.clip(np.minimum(c0 + CMP_BLOCK, s0 + SLC_BLOCK) - np.maximum(c0, s0), 0, None)
    agg = np.zeros((n_cmp_pad, n_slc_pad), np.float32)
    agg[:n_cmp, :n_slc] = overlap / CMP_BLOCK
    return jnp.asarray(agg, BF16)


def _cmp_attn(q_hm, kc, vc, gates, agg, *, tq, pos0, tiled, n_slc, sel_dtype):
    b, _, t, _ = q_hm.shape
    n = kc.shape[2]
    nsp = agg.shape[1]
    blocks_major = tq % LANES == 0
    if blocks_major:
        agg = agg.T
    ns = CMP_ATTN_SEQS if (t == tq and b % CMP_ATTN_SEQS == 0) else 1
    return pl.pallas_call(
        functools.partial(_cmp_attn_kernel, pos0=pos0, tiled=tiled, n_slc=n_slc, blocks_major=blocks_major),
        grid=(b // ns, t // tq),
        in_specs=[
            pl.BlockSpec((ns, N_HEADS, tq, HEAD_DIM), lambda i, k: (i, 0, k, 0)),
            pl.BlockSpec((ns, N_KV_HEADS, n, HEAD_DIM), lambda i, k: (i, 0, 0, 0)),
            pl.BlockSpec((ns, N_KV_HEADS, n, HEAD_DIM), lambda i, k: (i, 0, 0, 0)),
            pl.BlockSpec((ns, tq, LANES), lambda i, k: (i, k, 0)),
            _const_spec(agg.shape),
        ],
        out_specs=[pl.BlockSpec((ns, tq, NSA_WIDTH), lambda i, k: (i, k, 0)),
                   pl.BlockSpec((ns, tq, N_KV_HEADS * nsp), lambda i, k: (i, k, 0))],
        out_shape=[jax.ShapeDtypeStruct((b, t, NSA_WIDTH), F32),
                   jax.ShapeDtypeStruct((b, t, N_KV_HEADS * nsp), sel_dtype)],
        compiler_params=_params("arbitrary", "arbitrary"),
        name="cmp_attn",
    )(q_hm, kc, vc, gates, agg)


def _expand_matrix(n_keys, n_blk_pad, tk):
    e = np.zeros((n_keys // tk, n_blk_pad, tk), np.float32)
    key = np.arange(n_keys)
    e[key // tk, key // SLC_BLOCK, key % tk] = 1.0
    return jnp.asarray(e, BF16)


def _col_blocks(x, op):
    out = x[:, :LANES]
    for c in range(1, x.shape[1] // LANES):
        out = op(out, x[:, c * LANES:(c + 1) * LANES])
    return out


def _prompt_attn_kernel(q_ref, kvs_ref, kvw_ref, sel_ref, e_ref, oc_ref, gate_ref, o_ref,
                        qx_ref, qg_ref, s_ref, m_ref, l_ref, acc_ref):
    tq = q_ref.shape[1]
    tk = kvs_ref.shape[-1]
    nsp = sel_ref.shape[1] // N_KV_HEADS
    rows = GROUP * tq
    spare = s_ref.shape[0] - 1
    qt = pl.program_id(1)
    n_win = WINDOW // tk
    row = lax.broadcasted_iota(jnp.int32, (tq, tk), 0)
    col = lax.broadcasted_iota(jnp.int32, (tq, tk), 1)
    causal = jnp.where(col <= row, 0.0, NEG_INF)
    win_lo = jnp.where(col >= row, 0.0, NEG_INF)
    gates = gate_ref[...]
    oc = oc_ref[...]
    k_pad = jnp.zeros((qx_ref.shape[1] - nsp - HEAD_DIM, tk), BF16)
    qx_ref[:, nsp + HEAD_DIM:] = jnp.zeros((rows, qx_ref.shape[1] - nsp - HEAD_DIM), BF16)

    def scores_slc(g, kt):
        rhs = jnp.concatenate([e_ref[kt], kvs_ref[g, kt], k_pad], axis=0)
        return jnp.dot(qx_ref[...], rhs, preferred_element_type=F32)

    def scores_win(g, kt):
        return jnp.dot(qg_ref[...], kvw_ref[g, kt], preferred_element_type=F32)

    def find_max(scores, g, kt, slot, bias):
        s = scores(g, kt)
        if bias is not None:
            s = (s.reshape(GROUP, tq, tk) + bias[None]).reshape(rows, tk)
        s_ref[slot] = s
        m_ref[...] = jnp.maximum(m_ref[...], _col_blocks(s, jnp.maximum))

    def accumulate(v_ref, g, kt, slot):
        p = jnp.exp2(s_ref[slot] - jnp.tile(m_ref[...], (1, tk // LANES)))
        l_ref[...] += _col_blocks(p, jnp.add)
        acc_ref[...] += lax.dot_general(p.astype(BF16), v_ref[N_KV_HEADS + g, kt], _NT, preferred_element_type=F32)

    def maybe(kt, bias):
        exists = kt >= 0
        gate = jnp.where(exists, 0.0, NEG_INF)
        return jnp.maximum(kt, 0), jnp.where(exists, kt, spare), (gate if bias is None else bias + gate)

    def softmax_v(scores, v_ref, g, loop_pairs, tail):
        def both_passes(fn_pair, fn_tail):
            def body(i, carry):
                fn_pair(2 * i)
                fn_pair(2 * i + 1)
                return carry
            lax.fori_loop(0, loop_pairs, body, 0)
            for kt, slot, bias in tail:
                fn_tail(kt, slot, bias)

        m_ref[...] = jnp.full(m_ref.shape, NEG_INF, F32)
        both_passes(lambda kt: find_max(scores, g, kt, kt, None),
                    lambda kt, slot, bias: find_max(scores, g, kt, slot, bias))
        m_ref[...] = jnp.broadcast_to(jnp.max(m_ref[...], axis=-1, keepdims=True), m_ref.shape)
        l_ref[...] = jnp.zeros(l_ref.shape, F32)
        acc_ref[...] = jnp.zeros(acc_ref.shape, F32)
        both_passes(lambda kt: accumulate(v_ref, g, kt, kt),
                    lambda kt, slot, bias: accumulate(v_ref, g, kt, slot))
        return acc_ref[...] / jnp.sum(l_ref[...], axis=-1, keepdims=True)

    for g in range(N_KV_HEADS):
        sel_g = sel_ref[:, g * nsp:(g + 1) * nsp]
        for r in range(GROUP):
            qx_ref[r * tq:(r + 1) * tq, :nsp] = sel_g
            qx_ref[r * tq:(r + 1) * tq, nsp:nsp + HEAD_DIM] = q_ref[GROUP * g + r]
            qg_ref[r * tq:(r + 1) * tq, :] = q_ref[GROUP * g + r]
        odd_kt = jnp.where(qt % 2 == 1, qt - 1, -1)
        o_s = softmax_v(scores_slc, kvs_ref, g, qt // 2, [maybe(odd_kt, None), (qt, qt, causal)])
        tail = [maybe(qt - n_win, win_lo)] + [maybe(qt - d, None) for d in range(n_win - 1, 0, -1)]
        o_w = softmax_v(scores_win, kvw_ref, g, 0, tail + [(qt, qt, causal)])

        for r in range(GROUP):
            hd = GROUP * g + r
            cols = slice(hd * HEAD_DIM, (hd + 1) * HEAD_DIM)
            head = slice(r * tq, (r + 1) * tq)
            o_ref[:, cols] = (oc[:, cols] + gates[:, 3 * hd + 1:3 * hd + 2] * o_s[head]
                              + gates[:, 3 * hd + 2:3 * hd + 3] * o_w[head]).astype(o_ref.dtype)


def _prompt_attn(q_hm, kvs_b, kvw_b, sel, expand, oc, gates):
    b, _, t, _ = q_hm.shape
    tq = kvs_b.shape[-1]
    rows = GROUP * tq
    nsp = sel.shape[2] // N_KV_HEADS
    k_ext = LANES * (-(-(nsp + HEAD_DIM) // LANES))
    tile3 = lambda i, k: (i, k, 0)
    kv_spec = pl.BlockSpec((None,) + kvs_b.shape[1:], lambda i, k: (i, 0, 0, 0, 0))
    return pl.pallas_call(
        _prompt_attn_kernel,
        grid=(b, t // tq),
        in_specs=[
            pl.BlockSpec((None, N_HEADS, tq, HEAD_DIM), lambda i, k: (i, 0, k, 0)),
            kv_spec, kv_spec,
            pl.BlockSpec((None, tq, sel.shape[2]), tile3),
            _const_spec(expand.shape),
            pl.BlockSpec((None, tq, NSA_WIDTH), tile3),
            pl.BlockSpec((None, tq, LANES), tile3),
        ],
        out_specs=pl.BlockSpec((None, tq, NSA_WIDTH), tile3),
        out_shape=jax.ShapeDtypeStruct((b, t, NSA_WIDTH), BF16),
        scratch_shapes=[pltpu.VMEM((rows, k_ext), BF16), pltpu.VMEM((rows, HEAD_DIM), BF16),
                        pltpu.VMEM((t // tq + 1, rows, tq), F32),
                        pltpu.VMEM((rows, LANES), F32), pltpu.VMEM((rows, LANES), F32),
                        pltpu.VMEM((rows, HEAD_DIM), F32)],
        compiler_params=_params("arbitrary", "arbitrary"),
        name="prompt_attn",
    )(q_hm, kvs_b, kvw_b, sel, expand, oc, gates)


Q_PAD = 8


def _group_q(q_ref, g):
    return jnp.concatenate([q_ref[GROUP * g + r] for r in range(GROUP)], axis=0).astype(BF16)


def _two_piece_attention(qg, k_old, v_old, bias_old, k_new, v_new, bias_new):
    def scores(k_t, bias):
        s = jnp.dot(qg, k_t.astype(BF16), preferred_element_type=F32)
        nk = s.shape[1]
        return (s.reshape(GROUP, Q_PAD, nk) + bias[None]).reshape(GROUP * Q_PAD, nk)
    s_old = scores(k_old, bias_old)
    s_new = scores(k_new, bias_new)
    m = jnp.maximum(jnp.max(s_old, axis=-1, keepdims=True), jnp.max(s_new, axis=-1, keepdims=True))
    p_old = jnp.exp(s_old - m)
    p_new = jnp.exp(s_new - m)
    den = jnp.sum(p_old, axis=-1, keepdims=True) + jnp.sum(p_new, axis=-1, keepdims=True)
    pv = (lax.dot_general(p_old.astype(BF16), v_old.astype(BF16), _NT, preferred_element_type=F32)
          + lax.dot_general(p_new.astype(BF16), v_new.astype(BF16), _NT, preferred_element_type=F32))
    return pv / den


def _add_gated(prev_ref, gate_ref, o_ref, per_g, branch):
    prev, gates = prev_ref[...], gate_ref[...]
    for g in range(N_KV_HEADS):
        for r in range(GROUP):
            hd = GROUP * g + r
            cols = slice(hd * HEAD_DIM, (hd + 1) * HEAD_DIM)
            o_ref[:, cols] = (prev[:, cols] + gates[:, 3 * hd + branch:3 * hd + branch + 1]
                              * per_g[g][r * Q_PAD:(r + 1) * Q_PAD])


def _new_rows_bias(dec_seq):
    q = lax.broadcasted_iota(jnp.int32, (Q_PAD, LANES), 0)
    i = lax.broadcasted_iota(jnp.int32, (Q_PAD, LANES), 1) - (LANES - dec_seq)
    return jnp.where((i >= 0) & (i <= q), 0.0, NEG_INF)


def _sample_slc_kernel(pt_ref, cache_hbm, q_ref, sel_ref, new_ref, e_ref, prev_ref, gate_ref, o_ref,
                       buf, sem, *, dec_seq):
    n_pages = pt_ref.shape[1]

    def copy(page, slot, p):
        return pltpu.make_async_copy(cache_hbm.at[page],
                                     buf.at[slot, :, :, pl.ds(p * PAGE_SIZE, PAGE_SIZE)], sem.at[slot])

    def start_fetch(seq, slot):
        for p in range(n_pages):
            copy(pt_ref[seq, p], slot, p).start()

    def wait_fetch(slot):
        for p in range(n_pages):
            copy(0, slot, p).wait()

    slot = _paged_prologue(start_fetch, wait_fetch)
    nsp = sel_ref.shape[1] // N_KV_HEADS
    n_past_blk = n_pages * (PAGE_SIZE // SLC_BLOCK)
    new_bias = _new_rows_bias(dec_seq)
    n_exp = e_ref.shape[0]
    sel = [sel_ref[:, g * nsp:(g + 1) * nsp] for g in range(N_KV_HEADS)]
    sel_past = jnp.concatenate([s[:, :n_exp] for s in sel], axis=0).astype(BF16)
    bias_past = jnp.dot(sel_past, e_ref[...], preferred_element_type=F32)
    per_g = []
    for g in range(N_KV_HEADS):
        bias_old = bias_past[g * Q_PAD:(g + 1) * Q_PAD]
        bias_new = new_bias + sel[g][:, n_past_blk:n_past_blk + 1]
        per_g.append(_two_piece_attention(_group_q(q_ref, g), buf[slot, g], buf[slot, N_KV_HEADS + g], bias_old,
                                          new_ref[g], new_ref[N_KV_HEADS + g], bias_new))
    _add_gated(prev_ref, gate_ref, o_ref, per_g, 1)


def _sample_slc(page_table, cache, q_s, sel, new_t, expand, prev, gates, dec_seq):
    db, n_pages = page_table.shape
    past = n_pages * PAGE_SIZE
    seq3 = lambda i, pt: (i, 0, 0)
    seq4 = lambda i, pt: (i, 0, 0, 0)
    return pl.pallas_call(
        functools.partial(_sample_slc_kernel, dec_seq=dec_seq),
        grid_spec=pltpu.PrefetchScalarGridSpec(
            num_scalar_prefetch=1,
            grid=(db,),
            in_specs=[pl.BlockSpec(memory_space=pl.ANY),
                      pl.BlockSpec((None, N_HEADS, Q_PAD, HEAD_DIM), seq4),
                      pl.BlockSpec((None, Q_PAD, sel.shape[2]), seq3),
                      pl.BlockSpec((None, N_KV, HEAD_DIM, LANES), seq4),
                      pl.BlockSpec(expand.shape, lambda i, pt: (0, 0)),
                      pl.BlockSpec((None, Q_PAD, NSA_WIDTH), seq3),
                      pl.BlockSpec((None, Q_PAD, LANES), seq3)],
            out_specs=pl.BlockSpec((None, Q_PAD, NSA_WIDTH), seq3),
            scratch_shapes=[pltpu.VMEM((2, N_KV, HEAD_DIM, past), F32), pltpu.SemaphoreType.DMA((2,))],
        ),
        out_shape=jax.ShapeDtypeStruct((db, Q_PAD, NSA_WIDTH), F32),
        compiler_params=_params("arbitrary"),
        name="sample_slc",
    )(page_table, cache, q_s, sel, new_t, expand, prev, gates)


WIN_SEQS = 4


def _sample_win_kernel(q_ref, st_ref, new_ref, prev_ref, gate_ref, o_ref, st_out_ref, *, dec_seq):
    wb = st_ref.shape[-1]
    q = lax.broadcasted_iota(jnp.int32, (Q_PAD, wb), 0)
    i = lax.broadcasted_iota(jnp.int32, (Q_PAD, wb), 1)
    bias_old = jnp.where(wb + q - i <= WINDOW, 0.0, NEG_INF)
    new_bias = _new_rows_bias(dec_seq)
    lane = lax.broadcasted_iota(jnp.int32, (HEAD_DIM, wb), 1)
    per_seq = [[_two_piece_attention(_group_q(q_ref.at[b], g), st_ref[b, g], st_ref[b, N_KV_HEADS + g], bias_old,
                                     new_ref[b, g], new_ref[b, N_KV_HEADS + g], new_bias)
                for g in range(N_KV_HEADS)] for b in range(q_ref.shape[0])]
    for b, per_g in enumerate(per_seq):
        _add_gated(prev_ref.at[b], gate_ref.at[b], o_ref.at[b], per_g, 2)
        for j in range(N_KV):
            shifted = pltpu.roll(st_ref[b, j], wb - dec_seq, axis=1)
            st_out_ref[b, j] = jnp.where(lane >= wb - dec_seq, jnp.tile(new_ref[b, j], (1, wb // LANES)), shifted)


def _sample_win(q_s, st_win, new_t, prev, gates, dec_seq):
    db, _, _, wb = st_win.shape
    ns = WIN_SEQS if db % WIN_SEQS == 0 else 1
    seq3 = lambda i: (i, 0, 0)
    seq4 = lambda i: (i, 0, 0, 0)
    return pl.pallas_call(
        functools.partial(_sample_win_kernel, dec_seq=dec_seq),
        grid=(db // ns,),
        in_specs=[pl.BlockSpec((ns, N_HEADS, Q_PAD, HEAD_DIM), seq4),
                  pl.BlockSpec((ns, N_KV, HEAD_DIM, wb), seq4),
                  pl.BlockSpec((ns, N_KV, HEAD_DIM, LANES), seq4),
                  pl.BlockSpec((ns, Q_PAD, NSA_WIDTH), seq3),
                  pl.BlockSpec((ns, Q_PAD, LANES), seq3)],
        out_specs=[pl.BlockSpec((ns, Q_PAD, NSA_WIDTH), seq3), pl.BlockSpec((ns, N_KV, HEAD_DIM, wb), seq4)],
        out_shape=[jax.ShapeDtypeStruct((db, Q_PAD, NSA_WIDTH), F32),
                   jax.ShapeDtypeStruct((db, N_KV, HEAD_DIM, wb), F32)],
        compiler_params=_params("arbitrary"),
        name="sample_win",
    )(q_s, st_win, new_t, prev, gates)


POOL_TM = 512


def _pool_project(d_groups, wp_ref, scale_ref, o_ref):
    for gi, d in enumerate(d_groups):
        cols = slice(gi * POOL_GROUP_WIDTH, (gi + 1) * POOL_GROUP_WIDTH)
        y = jnp.dot(d.astype(BF16), wp_ref[gi], preferred_element_type=F32)
        o_ref[:, cols] = (y * scale_ref[:, cols]).astype(o_ref.dtype)


def _pool_prompt_kernel(u_ref, wp_ref, scale_ref, o_ref, ext_ref):
    tm = o_ref.shape[0]
    t0 = pl.multiple_of(pl.program_id(1) * tm, tm)
    halo = u_ref[pl.ds(pl.multiple_of(jnp.maximum(t0 - POOL_HALO, 0), POOL_HALO), POOL_HALO), :]
    ext_ref[0:POOL_HALO, :] = jnp.where(t0 > 0, halo, 0.0)
    ext_ref[POOL_HALO:, :] = u_ref[pl.ds(t0, tm), :]
    pos = t0 + lax.broadcasted_iota(jnp.int32, (tm, POOL_GROUP_WIDTH), 0)
    d_groups = []
    for gi, w in enumerate(POOL_WINDOWS):
        cols = slice(gi * POOL_GROUP_WIDTH, (gi + 1) * POOL_GROUP_WIDTH)
        e = ext_ref[:, cols]
        acc = e
        span = 1
        while span < w:
            acc = acc + pltpu.roll(acc, span, axis=0)
            span *= 2
        cnt = jnp.minimum(w, pos + 1).astype(F32)
        d_groups.append(acc[POOL_HALO:] / cnt - e[POOL_HALO:])
    _pool_project(d_groups, wp_ref, scale_ref, o_ref)


def _pool_prompt(u, wp, scale):
    b, t, pw = u.shape
    tm = min(POOL_TM, t)
    return pl.pallas_call(
        _pool_prompt_kernel,
        grid=(b, t // tm),
        in_specs=[pl.BlockSpec((None, t, pw), lambda i, k: (i, 0, 0)), _const_spec(wp.shape), _const_spec(scale.shape)],
        out_specs=pl.BlockSpec((None, tm, pw), lambda i, k: (i, k, 0)),
        out_shape=jax.ShapeDtypeStruct((b, t, pw), BF16),
        scratch_shapes=[pltpu.VMEM((POOL_HALO + tm, pw), F32)],
        compiler_params=_params("arbitrary", "arbitrary"),
        name="pool_prompt",
    )(u, wp, scale)


def _pool_sample_kernel(ext_ref, wp_ref, scale_ref, o_ref, *, past_len, dec_seq):
    db = ext_ref.shape[1]
    for q in range(dec_seq):
        d_groups = []
        for gi, w in enumerate(POOL_WINDOWS):
            cols = slice(gi * POOL_GROUP_WIDTH, (gi + 1) * POOL_GROUP_WIDTH)
            row = POOL_HALO + q
            acc = ext_ref[row, :, cols]
            for i in range(1, w):
                acc = acc + ext_ref[row - i, :, cols]
            cnt = float(min(w, past_len + q + 1))
            d_groups.append(acc / cnt - ext_ref[row, :, cols])
        _pool_project(d_groups, wp_ref, scale_ref, o_ref.at[pl.ds(q * db, db)])


def _pool_sample(ext, wp, scale, past_len, dec_seq):
    rows, db, pw = ext.shape
    return pl.pallas_call(
        functools.partial(_pool_sample_kernel, past_len=past_len, dec_seq=dec_seq),
        grid=(1,),
        in_specs=[_const_spec(ext.shape), _const_spec(wp.shape), _const_spec(scale.shape)],
        out_specs=_const_spec((dec_seq * db, pw)),
        out_shape=jax.ShapeDtypeStruct((dec_seq * db, pw), BF16),
        compiler_params=_params("arbitrary"),
        name="pool_sample",
    )(ext, wp, scale)


def _cmp_weights(w1, w2, pe):
    n_slots = CMP_BLOCK // CMP_STRIDE
    w1s = w1.reshape(n_slots, CMP_FLAT, CMP_HIDDEN)
    w1cat = jnp.concatenate([w1s[h] for h in range(n_slots)], axis=1).astype(BF16)
    pe8 = jnp.pad(pe.reshape(n_slots, CMP_FLAT), ((0, SUBLANES - n_slots), (0, 0)))
    return w1cat, w2.astype(BF16), pe8


def _pad_axis(x, axis, size, front=False):
    pad = [(0, 0)] * x.ndim
    extra = size - x.shape[axis]
    pad[axis] = (extra, 0) if front else (0, extra)
    return jnp.pad(x, pad)


def _rows_to_state(x_fm):
    b, _, _, t = x_fm.shape
    return x_fm.reshape(b, 2, N_KV_HEADS, HEAD_DIM, t).transpose(0, 4, 1, 2, 3)


def _state_to_fm(x):
    b, r = x.shape[:2]
    return x.transpose(0, 2, 3, 4, 1).reshape(b, N_KV, HEAD_DIM, r)


def kernel(x_prompt, x_sample, cache_kv_cmp, cache_kv_slc, page_table, state_kv_win, state_pool, n_ffn1, w_ffn1_gate, w_ffn1_up, w_ffn1_down, n_mix, w_in, w_cmp_k1, w_cmp_k2, pe_cmp_k, w_cmp_v1, w_cmp_v2, pe_cmp_v, w_pool, pool_scale, w_out, n_ffn2, w_ffn2_gate, w_ffn2_up, w_ffn2_down, n_final):
    b, t, d = x_prompt.shape
    db, ds, _ = x_sample.shape
    depth = w_in.shape[0]
    n_pages = page_table.shape[1]
    past = n_pages * PAGE_SIZE
    wb = state_kv_win.shape[2]
    assert t % ATTN_TQ == 0 and t % INPROJ_TM == 0 and WINDOW % ATTN_TQ == 0 and t >= WINDOW
    assert ds <= Q_PAD and ds <= CMP_STRIDE and wb == WINDOW and past % SLC_BLOCK == 0
    assert (t // CMP_STRIDE) % CMP_ROWS == 0 and (db * ds) % SUBLANES == 0 and n_pages % CMP_PAGE_GROUP == 0

    xp = x_prompt.reshape(b * t, d)
    xs = x_sample.reshape(db * ds, d)
    pos_p = jnp.arange(t, dtype=jnp.int32)
    pos_s = past + jnp.arange(ds, dtype=jnp.int32)
    tab_p = _rope_tables(pos_p)
    tab_s = tuple(jnp.tile(a, (db, 1)) for a in _rope_tables(pos_s)[:3]) + tuple(
        jnp.tile(a, (1, db)) for a in _rope_tables(pos_s)[3:])

    n_chunk_p = t // CMP_STRIDE
    n_slc_p = -(-t // SLC_BLOCK)
    agg_p = _agg_matrix(n_chunk_p, n_chunk_p - 1, n_slc_p, LANES * (-(-n_slc_p // LANES)))
    expand_p = _expand_matrix(t, agg_p.shape[1], ATTN_TQ)
    n_chunk_s = past // CMP_STRIDE
    n_slc_s = -(-(past + ds) // SLC_BLOCK)
    agg_s = _agg_matrix(n_chunk_s, n_chunk_s, n_slc_s, LANES * (-(-n_slc_s // LANES)))
    expand_s = _expand_matrix(past, LANES * (-(-(past // SLC_BLOCK) // LANES)), past)[0]

    st_p = ([], [], [], [])
    st_s = ([], [], [], [])
    for l in range(depth):
        last = l == depth - 1
        ffn1 = (n_ffn1[l], w_ffn1_gate[l].astype(BF16), w_ffn1_up[l].astype(BF16), w_ffn1_down[l].astype(BF16))
        ffn2 = (n_ffn2[l], w_ffn2_gate[l].astype(BF16), w_ffn2_up[l].astype(BF16), w_ffn2_down[l].astype(BF16))
        w = w_in[l]
        o_kv, o_gate, o_pool = NSA_WIDTH, NSA_WIDTH + 3 * KV_WIDTH, NSA_WIDTH + 3 * KV_WIDTH + N_GATES
        wq = w[:, :o_kv].astype(BF16)
        wkv_t = w[:, o_kv:o_gate].T.astype(BF16)
        wkc = w[:, o_kv:o_kv + KV_WIDTH].astype(BF16)
        wgt = _pad_axis(w[:, o_gate:o_pool], 1, LANES).astype(BF16)
        wu = w[:, o_pool:].astype(BF16)
        proj = (n_mix[l], wq, wkv_t, wkc, wgt, wu)
        wk1, wk2, pek = _cmp_weights(w_cmp_k1[l], w_cmp_k2[l], pe_cmp_k[l])
        wv1, wv2, pev = _cmp_weights(w_cmp_v1[l], w_cmp_v2[l], pe_cmp_v[l])
        pe = jnp.stack([pek, pev])
        wp = w_pool[l].astype(BF16)
        scale = pool_scale[l].reshape(1, POOL_WIDTH)
        wo_nsa = w_out[l][:NSA_WIDTH].astype(BF16)
        wo_pool = w_out[l][NSA_WIDTH:].astype(BF16)

        xp = _ffn(xp, *ffn1)
        q_hm, q2_hm, kvc_rm, kvc, kvs, kvw, kvs_b, kvw_b, gates, u = _inproj(xp, *proj, tab_p, BF16)
        kc, vc = _compress_prompt(kvc_rm.reshape(b, n_chunk_p, CHUNK_WIDTH), pe, wk1, wv1, wk2, wv2)
        gates3 = gates.reshape(b, t, LANES)
        oc, sel = _cmp_attn(q_hm, kc, vc, gates3, agg_p, tq=ATTN_TQ, pos0=0, tiled=True,
                            n_slc=n_slc_p, sel_dtype=BF16)
        o_mix = _prompt_attn(q2_hm, kvs_b, kvw_b, sel, expand_p, oc, gates3)
        u3 = u.reshape(b, t, POOL_WIDTH)
        pool_out = _pool_prompt(u3, wp, scale)
        mix = (o_mix.reshape(b * t, NSA_WIDTH), pool_out.reshape(b * t, POOL_WIDTH), wo_nsa, wo_pool)
        xp = _ffn(xp, *ffn2, g_final=n_final if last else None, mix=mix)
        st_p[0].append(_rows_to_state(kvc))
        st_p[1].append(_rows_to_state(kvs))
        st_p[2].append(_rows_to_state(kvw[..., t - min(WINDOW, t):]))
        st_p[3].append(u3[:, t - POOL_STATE:])

        xs = _ffn(xs, *ffn1)
        q, _, _, kvc, kvs, kvw, _, _, gates, u = _inproj(xs, *proj, tab_s, F32)
        per_seq = lambda a: a.reshape(N_KV, HEAD_DIM, db, ds).transpose(2, 0, 1, 3)
        kvc_n, kvs_n, kvw_n = per_seq(kvc), per_seq(kvs), per_seq(kvw)
        ynew = _pad_axis(kvc_n.transpose(0, 1, 3, 2), 2, CMP_STRIDE).reshape(db, N_KV, CMP_FLAT)
        cache_c = _state_to_fm(cache_kv_cmp[l])
        kc, vc = _compress_sample(page_table, cache_c, _pad_axis(ynew, 1, SUBLANES), pe, wk1, wv1, wk2, wv2)
        q_s = _pad_axis(q.reshape(N_HEADS, db, ds, HEAD_DIM).transpose(1, 0, 2, 3), 2, Q_PAD)
        gates_s = _pad_axis(gates.reshape(db, ds, LANES), 1, Q_PAD)
        oc, p_slc = _cmp_attn(q_s, kc, vc, gates_s, agg_s, tq=Q_PAD, pos0=past, tiled=False,
                              n_slc=n_slc_s, sel_dtype=F32)
        sel = _select_all(p_slc, pos0=past, n_slc=n_slc_s)
        o_cs =_sample_slc(page_table, _state_to_fm(cache_kv_slc[l]), q_s, sel,
                           _pad_axis(kvs_n, 3, LANES, front=True), expand_s, oc, gates_s, ds)
        o_mix, st_win_new = _sample_win(q_s, _state_to_fm(state_kv_win[l]),
                                        _pad_axis(kvw_n, 3, LANES, front=True), o_cs, gates_s, ds)
        u3 = u.reshape(db, ds, POOL_WIDTH)
        ext = jnp.concatenate([jnp.zeros((db, POOL_HALO - POOL_STATE, POOL_WIDTH), F32), state_pool[l], u3], axis=1)
        pool_out = _pool_sample(ext.transpose(1, 0, 2), wp, scale, past, ds)
        pool_out = pool_out.reshape(ds, db, POOL_WIDTH).transpose(1, 0, 2).reshape(db * ds, POOL_WIDTH)
        mix = (o_mix[:, :ds].reshape(db * ds, NSA_WIDTH), pool_out, wo_nsa, wo_pool)
        xs = _ffn(xs, *ffn2, g_final=n_final if last else None, mix=mix)
        st_s[0].append(_rows_to_state(kvc_n))
        st_s[1].append(_rows_to_state(kvs_n))
        st_s[2].append(_rows_to_state(st_win_new))
        st_s[3].append(jnp.concatenate([state_pool[l], u3], axis=1)[:, ds:])

    return (xp.reshape(b, t, d), xs.reshape(db, ds, d),
            jnp.stack(st_p[0]), jnp.stack(st_p[1]), jnp.stack(st_p[2]), jnp.stack(st_p[3]),
            jnp.stack(st_s[0]), jnp.stack(st_s[1]), jnp.stack(st_s[2]), jnp.stack(st_s[3]))
```

```python
import functools

import numpy as np
import jax
import jax.numpy as jnp
from jax import lax
from jax.experimental import pallas as pl
from jax.experimental.pallas import tpu as pltpu

F32 = jnp.float32
BF16 = jnp.bfloat16

HEAD_DIM = 64
N_HEADS = 8
N_KV_HEADS = 2
GROUP = N_HEADS // N_KV_HEADS
NSA_WIDTH = N_HEADS * HEAD_DIM
N_KV = 2 * N_KV_HEADS
KV_WIDTH = N_KV * HEAD_DIM
ROPE_DIM = HEAD_DIM // 4
ROPE_HALF = ROPE_DIM // 2
ROPE_THETA = 500000.0
CMP_BLOCK = 32
CMP_STRIDE = 16
CMP_HIDDEN = 256
CMP_FLAT = CMP_STRIDE * HEAD_DIM
SLC_BLOCK = 64
TOPK_BLOCKS = 16
WINDOW = 512
PAGE_SIZE = 128
POOL_WINDOWS = (2, 4, 8, 16)
POOL_GROUP_WIDTH = 128
POOL_WIDTH = POOL_GROUP_WIDTH * len(POOL_WINDOWS)
POOL_STATE = max(POOL_WINDOWS) - 1
POOL_HALO = 16
RMS_EPS = 1e-6
FORCE_SCORE = 1e4
NEG_INF = -1e30
ATTN_SCALE = HEAD_DIM ** -0.5
LOG2_E = float(np.log2(np.e))
N_GATES = 3 * N_HEADS

LANES = 128
SUBLANES = 8
VMEM_LIMIT_BYTES = 56 * 1024 * 1024

_NT = (((1,), (1,)), ((), ()))


def _params(*sem):
    return pltpu.CompilerParams(dimension_semantics=sem, vmem_limit_bytes=VMEM_LIMIT_BYTES)


def _rmsnorm(x, g):
    return x * lax.rsqrt(jnp.mean(x * x, axis=-1, keepdims=True) + RMS_EPS) * g


def _const_spec(shape):
    n = len(shape)
    return pl.BlockSpec(shape, lambda *_: (0,) * n)


FFN_TM = 512
FFN_CHUNK = 256


def _ffn_kernel(x_ref, g_ref, wg_ref, wu_ref, wd_ref, gf_ref, *rest, final_norm, mix):
    if mix:
        mo_ref, mp_ref, wo_ref, wp_ref, o_ref, act_ref = rest
        x = (x_ref[...] + jnp.dot(mo_ref[...].astype(BF16), wo_ref[...], preferred_element_type=F32)
             + jnp.dot(mp_ref[...], wp_ref[...], preferred_element_type=F32))
    else:
        o_ref, act_ref = rest
        x = x_ref[...]
    h = _rmsnorm(x, g_ref[...]).astype(BF16)
    d_ff = wg_ref.shape[1]
    for c in range(d_ff // FFN_CHUNK):
        sl = slice(c * FFN_CHUNK, (c + 1) * FFN_CHUNK)
        a = jnp.dot(h, wg_ref[:, sl], preferred_element_type=F32)
        u = jnp.dot(h, wu_ref[:, sl], preferred_element_type=F32)
        act_ref[:, sl] = (a * jax.nn.sigmoid(a) * u).astype(BF16)
    y = x + 0.5 * jnp.dot(act_ref[...], wd_ref[...], preferred_element_type=F32)
    if final_norm:
        y = _rmsnorm(y, gf_ref[...])
    o_ref[...] = y


def _ffn(x, g, wg, wu, wd, g_final=None, mix=None):
    m, d = x.shape
    d_ff = wg.shape[1]
    tm = min(FFN_TM, m)
    final_norm = g_final is not None
    gf = g_final if final_norm else g
    row = lambda i: (i, 0)
    once = lambda shape: pl.BlockSpec(shape, lambda i: (0, 0), pipeline_mode=pl.Buffered(1))
    in_specs = [pl.BlockSpec((tm, d), row), _const_spec((1, d)),
                once((d, d_ff)), once((d, d_ff)), once((d_ff, d)), _const_spec((1, d))]
    args = [x, g.reshape(1, d), wg, wu, wd, gf.reshape(1, d)]
    if mix is not None:
        mo, mp, wo, wp = mix
        in_specs += [pl.BlockSpec((tm, mo.shape[1]), row), pl.BlockSpec((tm, mp.shape[1]), row),
                     once(wo.shape), once(wp.shape)]
        args += [mo, mp, wo, wp]
    return pl.pallas_call(
        functools.partial(_ffn_kernel, final_norm=final_norm, mix=mix is not None),
        grid=(m // tm,),
        in_specs=in_specs,
        out_specs=pl.BlockSpec((tm, d), row),
        out_shape=jax.ShapeDtypeStruct((m, d), F32),
        scratch_shapes=[pltpu.VMEM((tm, d_ff), BF16)],
        compiler_params=_params("arbitrary"),
        name="ffn",
    )(*args)


INPROJ_TM = 512
ATTN_TQ = 256


def _rope_tables(pos):
    inv = jnp.power(ROPE_THETA, -jnp.arange(ROPE_HALF, dtype=F32) / ROPE_HALF)
    ang = pos.astype(F32)[:, None] * inv[None, :]
    cos, sin = jnp.cos(ang), jnp.sin(ang)
    t = pos.shape[0]
    rest = HEAD_DIM - ROPE_DIM
    z_half = jnp.zeros((t, ROPE_HALF), F32)
    z_rest = jnp.zeros((t, rest), F32)
    cos_t = jnp.concatenate([cos, cos, jnp.ones((t, rest), F32)], axis=1)
    sin_a = jnp.concatenate([-sin, z_half, z_rest], axis=1)
    sin_b = jnp.concatenate([z_half, sin, z_rest], axis=1)
    rep = LANES // HEAD_DIM
    return tuple(jnp.tile(a, (1, rep)) for a in (cos_t, sin_a, sin_b)) + (cos.T, sin.T)


def _rope(z, cos_t, sin_a, sin_b):
    outs = []
    for c in range(z.shape[1] // LANES):
        zc = z[:, c * LANES:(c + 1) * LANES]
        outs.append(zc * cos_t + pltpu.roll(zc, LANES - ROPE_HALF, axis=1) * sin_a
                    + pltpu.roll(zc, ROPE_HALF, axis=1) * sin_b)
    return outs[0] if len(outs) == 1 else jnp.concatenate(outs, axis=1)


def _rope_fm(kt, cos, sin):
    x1, x2 = kt[0:ROPE_HALF], kt[ROPE_HALF:ROPE_DIM]
    return jnp.concatenate([x1 * cos - x2 * sin, x2 * cos + x1 * sin, kt[ROPE_DIM:]], axis=0)


def _inproj_kernel(x_ref, g_ref, wq_ref, wkv_t_ref, wgt_ref, wu_ref,
                   cos_ref, sa_ref, sb_ref, cos_fm_ref, sin_fm_ref, *rest, n_tab, pool):
    if pool:
        (wp_ref, scale_ref, q_ref, q2_ref, kvc_ref, kvs_ref, kvw_ref, kvs_b_ref, kvw_b_ref, gate_ref,
         pool_ref, u_tail_ref, carry_ref, ext_ref) = rest
    else:
        q_ref, q2_ref, kvc_ref, kvs_ref, kvw_ref, kvs_b_ref, kvw_b_ref, gate_ref, u_ref = rest
    h = _rmsnorm(x_ref[...], g_ref[...]).astype(BF16)
    tm = h.shape[0]
    tk = kvs_b_ref.shape[-1]
    cos_t, sin_a, sin_b = cos_ref[...], sa_ref[...], sb_ref[...]
    cos_fm, sin_fm = cos_fm_ref[...], sin_fm_ref[...]
    q = _rope(jnp.dot(h, wq_ref[...], preferred_element_type=F32), cos_t, sin_a, sin_b) * ATTN_SCALE
    q2 = q * LOG2_E
    for hd in range(N_HEADS):
        q_ref[hd] = q[:, hd * HEAD_DIM:(hd + 1) * HEAD_DIM].astype(q_ref.dtype)
        q2_ref[hd] = q2[:, hd * HEAD_DIM:(hd + 1) * HEAD_DIM].astype(BF16)
    kv_t = lax.dot_general(wkv_t_ref[...], h, _NT, preferred_element_type=F32)
    for i, (f_ref, b_ref) in enumerate(((kvc_ref, None), (kvs_ref, kvs_b_ref), (kvw_ref, kvw_b_ref))):
        for j in range(N_KV):
            r0 = i * KV_WIDTH + j * HEAD_DIM
            blk = kv_t[r0:r0 + HEAD_DIM]
            if j < N_KV_HEADS:
                blk = _rope_fm(blk, cos_fm, sin_fm)
            f_ref[j] = blk
            if b_ref is not None:
                for c in range(tm // tk):
                    b_ref[j, c] = blk[:, c * tk:(c + 1) * tk].astype(BF16)
    gate_ref[...] = jax.nn.sigmoid(jnp.dot(h, wgt_ref[...], preferred_element_type=F32))
    u = jnp.dot(h, wu_ref[...], preferred_element_type=F32)
    if pool:
        ti = lax.rem(pl.program_id(0), n_tab)
        ext_ref[0:POOL_HALO, :] = jnp.where(ti > 0, carry_ref[...], 0.0)
        ext_ref[POOL_HALO:, :] = u
        carry_ref[...] = u[tm - POOL_HALO:]
        u_tail_ref[...] = u[tm - POOL_HALO:]
        _pool_tile(ext_ref, ti * tm, wp_ref, scale_ref, pool_ref)
    else:
        u_ref[...] = u


def _inproj(x, g, wq, wkv_t, wgt, wu, tables, q_dtype, pool=None):
    m, d = x.shape
    seq = tables[0].shape[0]
    nseq = m // seq
    tm = min(INPROJ_TM, seq)
    tk = min(ATTN_TQ, tm)
    n_tab = seq // tm
    row = lambda i: (i, 0)
    tab = lambda i: (i % n_tab, 0)
    tab_fm = lambda i: (0, i % n_tab)
    hm = lambda i: (i // n_tab, 0, i % n_tab, 0)
    fm = lambda i: (i // n_tab, 0, 0, i % n_tab)
    fmb = lambda i: (i // n_tab, 0, i % n_tab, 0, 0)
    fm_shape = jax.ShapeDtypeStruct((nseq, N_KV, HEAD_DIM, seq), F32)
    fmb_shape = jax.ShapeDtypeStruct((nseq, N_KV, seq // tk, HEAD_DIM, tk), BF16)
    fm_spec = pl.BlockSpec((None, N_KV, HEAD_DIM, tm), fm)
    fmb_spec = pl.BlockSpec((None, N_KV, tm // tk, HEAD_DIM, tk), fmb)
    in_specs = [
        pl.BlockSpec((tm, d), row), _const_spec((1, d)),
        _const_spec(wq.shape), _const_spec(wkv_t.shape), _const_spec(wgt.shape), _const_spec(wu.shape),
        pl.BlockSpec((tm, LANES), tab), pl.BlockSpec((tm, LANES), tab), pl.BlockSpec((tm, LANES), tab),
        pl.BlockSpec((ROPE_HALF, tm), tab_fm), pl.BlockSpec((ROPE_HALF, tm), tab_fm),
    ]
    out_specs = [
        pl.BlockSpec((None, N_HEADS, tm, HEAD_DIM), hm), pl.BlockSpec((None, N_HEADS, tm, HEAD_DIM), hm),
        fm_spec, fm_spec, fm_spec, fmb_spec, fmb_spec,
        pl.BlockSpec((tm, LANES), row),
    ]
    out_shape = [
        jax.ShapeDtypeStruct((nseq, N_HEADS, seq, HEAD_DIM), q_dtype),
        jax.ShapeDtypeStruct((nseq, N_HEADS, seq, HEAD_DIM), BF16),
        fm_shape, fm_shape, fm_shape, fmb_shape, fmb_shape,
        jax.ShapeDtypeStruct((m, LANES), F32),
    ]
    args = [x, g.reshape(1, d), wq, wkv_t, wgt, wu, *tables]
    scratch = []
    if pool is not None:
        in_specs += [_const_spec(pool[0].shape), _const_spec(pool[1].shape)]
        args += list(pool)
        out_specs += [pl.BlockSpec((tm, POOL_WIDTH), row),
                      pl.BlockSpec((None, POOL_HALO, POOL_WIDTH), lambda i: (i // n_tab, 0, 0))]
        out_shape += [jax.ShapeDtypeStruct((m, POOL_WIDTH), BF16),
                      jax.ShapeDtypeStruct((nseq, POOL_HALO, POOL_WIDTH), F32)]
        scratch = [pltpu.VMEM((POOL_HALO, POOL_WIDTH), F32), pltpu.VMEM((POOL_HALO + tm, POOL_WIDTH), F32)]
    else:
        out_specs += [pl.BlockSpec((tm, POOL_WIDTH), row)]
        out_shape += [jax.ShapeDtypeStruct((m, POOL_WIDTH), F32)]
    return pl.pallas_call(
        functools.partial(_inproj_kernel, n_tab=n_tab, pool=pool is not None),
        grid=(m // tm,),
        in_specs=in_specs,
        out_specs=out_specs,
        out_shape=out_shape,
        scratch_shapes=scratch,
        compiler_params=_params("arbitrary"),
        name="inproj",
    )(*args)


def _gelu_tanh(x):
    return 0.5 * x * (1.0 + jnp.tanh(np.sqrt(2.0 / np.pi).astype(np.float32) * (x + 0.044715 * (x * x * x))))


def _pick_row(blk, r):
    row8 = lax.broadcasted_iota(jnp.int32, (SUBLANES, 1), 0)
    return jnp.sum(jnp.where(row8 == r, blk, 0.0), axis=0, keepdims=True)


def _compress_finish(kv, n, new8, pe_ref, w1_ref, w2_ref, out_ref, y_ref, has_new):
    y_ref[pl.ds(2 * n, 2 * SUBLANES), :] = jnp.concatenate([new8, pe_ref[kv]], axis=0).astype(BF16)
    p = jnp.dot(y_ref[...], w1_ref[...], preferred_element_type=F32)
    p_new, p_pe = p[2 * n:2 * n + SUBLANES], p[2 * n + SUBLANES:2 * n + 2 * SUBLANES]
    bias = _pick_row(p_pe[:, :CMP_HIDDEN], 0) + _pick_row(p_pe[:, CMP_HIDDEN:], 1)
    last = lax.broadcasted_iota(jnp.int32, (n, CMP_HIDDEN), 0) == n - 1
    for g in range(N_KV_HEADS):
        slot0 = p[g * n:(g + 1) * n, :CMP_HIDDEN]
        slot1 = pltpu.roll(p[g * n:(g + 1) * n, CMP_HIDDEN:], n - 1, axis=0)
        if has_new:
            slot1 = jnp.where(last, _pick_row(p_new[:, CMP_HIDDEN:], g), slot1)
        hid = _gelu_tanh(slot0 + slot1 + bias).astype(BF16)
        out_ref[g] = jnp.dot(hid, w2_ref[...], preferred_element_type=F32).astype(BF16)


CMP_PAGE_GROUP = 8


def _dechunk_perm():
    perm = np.zeros((N_KV_HEADS * PAGE_SIZE, 2 * LANES), np.float32)
    cpp = PAGE_SIZE // CMP_STRIDE
    for g in range(N_KV_HEADS):
        for c in range(cpp):
            for j in range(CMP_STRIDE // 2):
                for par in range(2):
                    perm[g * PAGE_SIZE + CMP_STRIDE * c + 2 * j + par, par * LANES + g * HEAD_DIM + j * cpp + c] = 1.0
    return jnp.asarray(perm, BF16)


def _regroup_pages(load_tile, n_pages, n, perm_ref, y_ref):
    pg = CMP_PAGE_GROUP
    cpp = PAGE_SIZE // CMP_STRIDE
    for grp in range(n_pages // pg):
        lhs = jnp.concatenate(
            [jnp.concatenate([load_tile(grp * pg + q, g) for g in range(N_KV_HEADS)], axis=1)
             for q in range(pg)], axis=0).astype(BF16)
        out = jnp.dot(lhs, perm_ref[...], preferred_element_type=F32)
        x = jnp.concatenate(
            [jnp.concatenate([out[q * HEAD_DIM:(q + 1) * HEAD_DIM, :LANES],
                              out[q * HEAD_DIM:(q + 1) * HEAD_DIM, LANES:]], axis=0) for q in range(pg)], axis=1)
        xt = x.T
        r0 = grp * (pg * cpp)
        for g in range(N_KV_HEADS):
            for j in range(CMP_STRIDE // 2):
                piece = jnp.concatenate(
                    [xt[q * LANES + g * HEAD_DIM + j * cpp:q * LANES + g * HEAD_DIM + (j + 1) * cpp]
                     for q in range(pg)], axis=0)
                y_ref[g * n + r0:g * n + r0 + pg * cpp, j * LANES:(j + 1) * LANES] = piece.astype(BF16)


def _compress_prompt_kernel(x_ref, pe_ref, perm_ref, wk1_ref, wv1_ref, wk2_ref, wv2_ref, kc_ref, vc_ref, y_ref):
    t = x_ref.shape[-1]
    n = t // CMP_STRIDE
    for kv, (w1_ref, w2_ref, out_ref) in enumerate(((wk1_ref, wk2_ref, kc_ref), (wv1_ref, wv2_ref, vc_ref))):
        _regroup_pages(lambda page, g: x_ref[N_KV_HEADS * kv + g, :, page * PAGE_SIZE:(page + 1) * PAGE_SIZE],
                       t // PAGE_SIZE, n, perm_ref, y_ref.at[kv])
        _compress_finish(kv, n, jnp.zeros((SUBLANES, CMP_FLAT), F32), pe_ref, w1_ref, w2_ref, out_ref,
                         y_ref.at[kv], False)


def _compress_prompt(x, pe, wk1, wv1, wk2, wv2):
    b, _, _, t = x.shape
    n = t // CMP_STRIDE
    perm = _dechunk_perm()
    out = jax.ShapeDtypeStruct((b, N_KV_HEADS, n, HEAD_DIM), BF16)
    out_spec = pl.BlockSpec((None, N_KV_HEADS, n, HEAD_DIM), lambda i: (i, 0, 0, 0))
    return pl.pallas_call(
        _compress_prompt_kernel,
        grid=(b,),
        in_specs=[pl.BlockSpec((None, N_KV, HEAD_DIM, t), lambda i: (i, 0, 0, 0)), _const_spec(pe.shape),
                  _const_spec(perm.shape),
                  _const_spec(wk1.shape), _const_spec(wv1.shape), _const_spec(wk2.shape), _const_spec(wv2.shape)],
        out_specs=[out_spec, out_spec],
        out_shape=[out, out],
        scratch_shapes=[pltpu.VMEM((2, 2 * n + 2 * SUBLANES, CMP_FLAT), BF16)],
        compiler_params=_params("arbitrary"),
        name="compress_prompt",
    )(x, pe, perm, wk1, wv1, wk2, wv2)


def _paged_prologue(start_fetch, wait_fetch):
    b = pl.program_id(0)
    slot = lax.rem(b, 2)

    @pl.when(b == 0)
    def _():
        start_fetch(0, 0)

    @pl.when(b + 1 < pl.num_programs(0))
    def _():
        start_fetch(b + 1, 1 - slot)

    wait_fetch(slot)
    return slot


def _compress_sample_kernel(pt_ref, cache_hbm, ynew_ref, pe_ref, perm_ref, wk1_ref, wv1_ref, wk2_ref, wv2_ref,
                            kc_ref, vc_ref, buf, sem, y_ref):
    n_pages = pt_ref.shape[1]
    n = n_pages * (PAGE_SIZE // CMP_STRIDE)

    def copy(page, slot, p):
        return pltpu.make_async_copy(cache_hbm.at[page], buf.at[slot, p], sem.at[slot])

    def start_fetch(seq, slot):
        for p in range(n_pages):
            copy(pt_ref[seq, p], slot, p).start()

    def wait_fetch(slot):
        for p in range(n_pages):
            copy(0, slot, p).wait()

    slot = _paged_prologue(start_fetch, wait_fetch)
    ynew = ynew_ref[...]
    row8 = lax.broadcasted_iota(jnp.int32, (SUBLANES, 1), 0)
    for kv, (w1_ref, w2_ref, out_ref) in enumerate(((wk1_ref, wk2_ref, kc_ref), (wv1_ref, wv2_ref, vc_ref))):
        _regroup_pages(lambda page, g: buf[slot, page, N_KV_HEADS * kv + g], n_pages, n, perm_ref, y_ref.at[kv])
        new8 = jnp.where(row8 == 0, _pick_row(ynew, N_KV_HEADS * kv),
                         jnp.where(row8 == 1, _pick_row(ynew, N_KV_HEADS * kv + 1), 0.0))
        _compress_finish(kv, n, new8, pe_ref, w1_ref, w2_ref, out_ref, y_ref.at[kv], True)


def _compress_sample(page_table, cache, ynew, pe, wk1, wv1, wk2, wv2):
    db, n_pages = page_table.shape
    n = n_pages * (PAGE_SIZE // CMP_STRIDE)
    out = jax.ShapeDtypeStruct((db, N_KV_HEADS, n, HEAD_DIM), BF16)
    out_spec = pl.BlockSpec((None, N_KV_HEADS, n, HEAD_DIM), lambda i, pt: (i, 0, 0, 0))
    const = lambda s: pl.BlockSpec(s, lambda i, pt: (0,) * len(s))
    perm = _dechunk_perm()
    return pl.pallas_call(
        _compress_sample_kernel,
        grid_spec=pltpu.PrefetchScalarGridSpec(
            num_scalar_prefetch=1,
            grid=(db,),
            in_specs=[pl.BlockSpec(memory_space=pl.ANY),
                      pl.BlockSpec((None, SUBLANES, CMP_FLAT), lambda i, pt: (i, 0, 0)),
                      const(pe.shape), const(perm.shape),
                      const(wk1.shape), const(wv1.shape), const(wk2.shape), const(wv2.shape)],
            out_specs=[out_spec, out_spec],
            scratch_shapes=[pltpu.VMEM((2, n_pages, N_KV, HEAD_DIM, PAGE_SIZE), F32),
                            pltpu.SemaphoreType.DMA((2,)),
                            pltpu.VMEM((2, 2 * n + 2 * SUBLANES, CMP_FLAT), BF16)],
        ),
        out_shape=[out, out],
        compiler_params=_params("arbitrary"),
        name="compress_sample",
    )(page_table, cache, ynew, pe, perm, wk1, wv1, wk2, wv2)


def _select_mask(p_slc, t_pos, j, n_slc):
    cur = t_pos // SLC_BLOCK
    forced = (j == 0) | (j == cur) | (j == cur - 1)
    real = j < n_slc
    score = jnp.where(forced, FORCE_SCORE, jnp.where(j <= cur, p_slc, NEG_INF))
    score = jnp.where(real, score, -jnp.inf)
    rank = jnp.zeros(score.shape, jnp.int32)
    for jp in range(n_slc):
        c = score[jp:jp + 1, :]
        rank = rank + ((c > score) | ((c == score) & (j > jp))).astype(jnp.int32)
    return jnp.where((rank < TOPK_BLOCKS) & real, 0.0, NEG_INF)


def _select_all_kernel(p_ref, o_ref, *, pos0, q_pad, n_slc):
    nsp, lanes = p_ref.shape
    n_rows = SUBLANES * (-(-n_slc // SUBLANES))
    t_pos = pos0 + lax.rem(lax.broadcasted_iota(jnp.int32, (n_rows, lanes), 1), q_pad)
    j = lax.broadcasted_iota(jnp.int32, (n_rows, lanes), 0)
    o_ref[0:n_rows, :] = _select_mask(p_ref[0:n_rows, :], t_pos, j, n_slc)
    if n_rows < nsp:
        o_ref[n_rows:, :] = jnp.zeros((nsp - n_rows, lanes), F32)


def _select_all(p_slc, *, pos0, n_slc):
    db, q_pad, width = p_slc.shape
    nsp = width // N_KV_HEADS
    p_t = p_slc.reshape(db, q_pad, N_KV_HEADS, nsp).transpose(3, 0, 2, 1).reshape(nsp, db * N_KV_HEADS * q_pad)
    mask_t = pl.pallas_call(
        functools.partial(_select_all_kernel, pos0=pos0, q_pad=q_pad, n_slc=n_slc),
        grid=(1,),
        in_specs=[_const_spec(p_t.shape)],
        out_specs=_const_spec(p_t.shape),
        out_shape=jax.ShapeDtypeStruct(p_t.shape, F32),
        compiler_params=_params("arbitrary"),
        name="select_all",
    )(p_t)
    return mask_t.reshape(nsp, db, N_KV_HEADS, q_pad).transpose(1, 3, 2, 0).reshape(db, q_pad, width)


CMP_ATTN_SEQS = 4


def _cmp_attn_kernel(q_ref, kc_ref, vc_ref, gate_ref, agg_ref, oc_ref, sel_ref, **static):
    results = [_cmp_attn_one(q_ref.at[i], kc_ref.at[i], vc_ref.at[i], gate_ref.at[i], agg_ref, **static)
               for i in range(q_ref.shape[0])]
    for i, (heads, masks) in enumerate(results):
        nsp = masks[0].shape[1]
        for g, mask in enumerate(masks):
            sel_ref[i, :, g * nsp:(g + 1) * nsp] = mask.astype(sel_ref.dtype)
        for hd, o in enumerate(heads):
            oc_ref[i, :, hd * HEAD_DIM:(hd + 1) * HEAD_DIM] = o


def _cmp_attn_one(q_ref, kc_ref, vc_ref, gate_ref, agg_ref, *, pos0, tiled, n_slc, blocks_major):
    tq = q_ref.shape[1]
    n = kc_ref.shape[1]
    nsp = agg_ref.shape[0] if blocks_major else agg_ref.shape[1]
    t0 = pos0 + (pl.program_id(1) * tq if tiled else 0)
    t_n = t0 + lax.broadcasted_iota(jnp.int32, (tq, n), 0)
    blk_end = lax.broadcasted_iota(jnp.int32, (tq, n), 1) * CMP_STRIDE + (CMP_BLOCK - 1)
    valid = (blk_end <= t_n)[None]
    if blocks_major:
        n_rows = SUBLANES * (-(-n_slc // SUBLANES))
        t_s = t0 + lax.broadcasted_iota(jnp.int32, (n_rows, tq), 1)
        j = lax.broadcasted_iota(jnp.int32, (n_rows, tq), 0)
        blk_end_t = lax.broadcasted_iota(jnp.int32, (n, GROUP * tq), 0) * CMP_STRIDE + (CMP_BLOCK - 1)
        valid_t = blk_end_t <= t0 + lax.rem(lax.broadcasted_iota(jnp.int32, (n, GROUP * tq), 1), tq)
    gates = gate_ref[...]
    heads, masks = [], []
    for g in range(N_KV_HEADS):
        qg = jnp.concatenate([q_ref[GROUP * g + r].astype(F32) for r in range(GROUP)], axis=0).astype(BF16)
        if blocks_major:
            s = lax.dot_general(kc_ref[g], qg, _NT, preferred_element_type=F32)
            s = jnp.where(valid_t, s, NEG_INF)
            e = jnp.where(valid_t, jnp.exp(s - jnp.max(s, axis=0, keepdims=True)), 0.0)
            p_t = e / jnp.maximum(jnp.sum(e, axis=0, keepdims=True), 1e-30)
            o = jnp.dot(p_t.T.astype(BF16), vc_ref[g], preferred_element_type=F32)
            p_grp = p_t[:, :tq]
            for r in range(1, GROUP):
                p_grp = p_grp + p_t[:, r * tq:(r + 1) * tq]
        else:
            s = lax.dot_general(qg, kc_ref[g], _NT, preferred_element_type=F32).reshape(GROUP, tq, n)
            s = jnp.where(valid, s, NEG_INF)
            e = jnp.where(valid, jnp.exp(s - jnp.max(s, axis=-1, keepdims=True)), 0.0)
            p = e / jnp.maximum(jnp.sum(e, axis=-1, keepdims=True), 1e-30)
            o = jnp.dot(p.reshape(GROUP * tq, n).astype(BF16), vc_ref[g], preferred_element_type=F32)
            p_grp = jnp.sum(p, axis=0)
        p_hi = p_grp.astype(BF16)
        p_lo = (p_grp - p_hi.astype(F32)).astype(BF16)
        if blocks_major:
            p_slc = (jnp.dot(agg_ref[...], p_hi, preferred_element_type=F32)
                     + jnp.dot(agg_ref[...], p_lo, preferred_element_type=F32))
            mask = _select_mask(p_slc[:n_rows], t_s, j, n_slc)
            if n_rows < nsp:
                mask = jnp.concatenate([mask, jnp.zeros((nsp - n_rows, tq), F32)], axis=0)
            mask = mask.T
        else:
            mask = (jnp.dot(p_hi, agg_ref[...], preferred_element_type=F32)
                    + jnp.dot(p_lo, agg_ref[...], preferred_element_type=F32))
        masks.append(mask)
        for r in range(GROUP):
            hd = GROUP * g + r
            heads.append(gates[:, 3 * hd:3 * hd + 1] * o[r * tq:(r + 1) * tq])
    return heads, masks


def _agg_matrix(n_cmp_pad, n_cmp, n_slc, n_slc_pad):
    c0 = np.arange(n_cmp)[:, None] * CMP_STRIDE
    s0 = np.arange(n_slc)[None, :] * SLC_BLOCK
    overlap = np.clip(np.minimum(c0 + CMP_BLOCK, s0 + SLC_BLOCK) - np.maximum(c0, s0), 0, None)
    agg = np.zeros((n_cmp_pad, n_slc_pad), np.float32)
    agg[:n_cmp, :n_slc] = overlap / CMP_BLOCK
    return jnp.asarray(agg, BF16)


def _cmp_attn(q_hm, kc, vc, gates, agg, *, tq, pos0, tiled, n_slc, sel_dtype):
    b, _, t, _ = q_hm.shape
    n = kc.shape[2]
    nsp = agg.shape[1]
    blocks_major = tq % LANES == 0
    if blocks_major:
        agg = agg.T
    ns = CMP_ATTN_SEQS if (t == tq and b % CMP_ATTN_SEQS == 0) else 1
    return pl.pallas_call(
        functools.partial(_cmp_attn_kernel, pos0=pos0, tiled=tiled, n_slc=n_slc, blocks_major=blocks_major),
        grid=(b // ns, t // tq),
        in_specs=[
            pl.BlockSpec((ns, N_HEADS, tq, HEAD_DIM), lambda i, k: (i, 0, k, 0)),
            pl.BlockSpec((ns, N_KV_HEADS, n, HEAD_DIM), lambda i, k: (i, 0, 0, 0)),
            pl.BlockSpec((ns, N_KV_HEADS, n, HEAD_DIM), lambda i, k: (i, 0, 0, 0)),
            pl.BlockSpec((ns, tq, LANES), lambda i, k: (i, k, 0)),
            _const_spec(agg.shape),
        ],
        out_specs=[pl.BlockSpec((ns, tq, NSA_WIDTH), lambda i, k: (i, k, 0)),
                   pl.BlockSpec((ns, tq, N_KV_HEADS * nsp), lambda i, k: (i, k, 0))],
        out_shape=[jax.ShapeDtypeStruct((b, t, NSA_WIDTH), F32),
                   jax.ShapeDtypeStruct((b, t, N_KV_HEADS * nsp), sel_dtype)],
        compiler_params=_params("arbitrary", "arbitrary"),
        name="cmp_attn",
    )(q_hm, kc, vc, gates, agg)


def _expand_matrix(n_keys, n_blk_pad, tk):
    e = np.zeros((n_keys // tk, n_blk_pad, tk), np.float32)
    key = np.arange(n_keys)
    e[key // tk, key // SLC_BLOCK, key % tk] = 1.0
    return jnp.asarray(e, BF16)


def _col_blocks(x, op):
    out = x[:, :LANES]
    for c in range(1, x.shape[1] // LANES):
        out = op(out, x[:, c * LANES:(c + 1) * LANES])
    return out


def _prompt_attn_kernel(q_ref, kvs_ref, kvw_ref, sel_ref, e_ref, oc_ref, gate_ref, o_ref,
                        qx_ref, qg_ref, s_ref, m_ref, l_ref, acc_ref):
    tq = q_ref.shape[1]
    tk = kvs_ref.shape[-1]
    nsp = sel_ref.shape[1] // N_KV_HEADS
    rows = GROUP * tq
    spare = s_ref.shape[0] - 1
    qt = pl.program_id(1)
    n_win = WINDOW // tk
    row = lax.broadcasted_iota(jnp.int32, (tq, tk), 0)
    col = lax.broadcasted_iota(jnp.int32, (tq, tk), 1)
    causal = jnp.where(col <= row, 0.0, NEG_INF)
    win_lo = jnp.where(col >= row, 0.0, NEG_INF)
    gates = gate_ref[...]
    oc = oc_ref[...]
    k_pad = jnp.zeros((qx_ref.shape[1] - nsp - HEAD_DIM, tk), BF16)
    qx_ref[:, nsp + HEAD_DIM:] = jnp.zeros((rows, qx_ref.shape[1] - nsp - HEAD_DIM), BF16)

    def scores_slc(g, kt):
        rhs = jnp.concatenate([e_ref[kt], kvs_ref[g, kt], k_pad], axis=0)
        return jnp.dot(qx_ref[...], rhs, preferred_element_type=F32)

    def scores_win(g, kt):
        return jnp.dot(qg_ref[...], kvw_ref[g, kt], preferred_element_type=F32)

    def find_max(scores, g, kt, slot, bias):
        s = scores(g, kt)
        if bias is not None:
            s = (s.reshape(GROUP, tq, tk) + bias[None]).reshape(rows, tk)
        s_ref[slot] = s
        m_ref[...] = jnp.maximum(m_ref[...], _col_blocks(s, jnp.maximum))

    def accumulate(v_ref, g, kt, slot):
        p = jnp.exp2(s_ref[slot] - jnp.tile(m_ref[...], (1, tk // LANES)))
        l_ref[...] += _col_blocks(p, jnp.add)
        acc_ref[...] += lax.dot_general(p.astype(BF16), v_ref[N_KV_HEADS + g, kt], _NT, preferred_element_type=F32)

    def maybe(kt, bias):
        exists = kt >= 0
        gate = jnp.where(exists, 0.0, NEG_INF)
        return jnp.maximum(kt, 0), jnp.where(exists, kt, spare), (gate if bias is None else bias + gate)

    def softmax_v(scores, v_ref, g, loop_pairs, tail):
        def both_passes(fn_pair, fn_tail):
            def body(i, carry):
                fn_pair(2 * i)
                fn_pair(2 * i + 1)
                return carry
            lax.fori_loop(0, loop_pairs, body, 0)
            for kt, slot, bias in tail:
                fn_tail(kt, slot, bias)

        m_ref[...] = jnp.full(m_ref.shape, NEG_INF, F32)
        both_passes(lambda kt: find_max(scores, g, kt, kt, None),
                    lambda kt, slot, bias: find_max(scores, g, kt, slot, bias))
        m_ref[...] = jnp.broadcast_to(jnp.max(m_ref[...], axis=-1, keepdims=True), m_ref.shape)
        l_ref[...] = jnp.zeros(l_ref.shape, F32)
        acc_ref[...] = jnp.zeros(acc_ref.shape, F32)
        both_passes(lambda kt: accumulate(v_ref, g, kt, kt),
                    lambda kt, slot, bias: accumulate(v_ref, g, kt, slot))
        return acc_ref[...] / jnp.sum(l_ref[...], axis=-1, keepdims=True)

    for g in range(N_KV_HEADS):
        sel_g = sel_ref[:, g * nsp:(g + 1) * nsp]
        for r in range(GROUP):
            qx_ref[r * tq:(r + 1) * tq, :nsp] = sel_g
            qx_ref[r * tq:(r + 1) * tq, nsp:nsp + HEAD_DIM] = q_ref[GROUP * g + r]
            qg_ref[r * tq:(r + 1) * tq, :] = q_ref[GROUP * g + r]
        odd_kt = jnp.where(qt % 2 == 1, qt - 1, -1)
        o_s = softmax_v(scores_slc, kvs_ref, g, qt // 2, [maybe(odd_kt, None), (qt, qt, causal)])
        tail = [maybe(qt - n_win, win_lo)] + [maybe(qt - d, None) for d in range(n_win - 1, 0, -1)]
        o_w = softmax_v(scores_win, kvw_ref, g, 0, tail + [(qt, qt, causal)])

        for r in range(GROUP):
            hd = GROUP * g + r
            cols = slice(hd * HEAD_DIM, (hd + 1) * HEAD_DIM)
            head = slice(r * tq, (r + 1) * tq)
            o_ref[:, cols] = (oc[:, cols] + gates[:, 3 * hd + 1:3 * hd + 2] * o_s[head]
                              + gates[:, 3 * hd + 2:3 * hd + 3] * o_w[head]).astype(o_ref.dtype)


def _prompt_attn(q_hm, kvs_b, kvw_b, sel, expand, oc, gates):
    b, _, t, _ = q_hm.shape
    tq = kvs_b.shape[-1]
    rows = GROUP * tq
    nsp = sel.shape[2] // N_KV_HEADS
    k_ext = LANES * (-(-(nsp + HEAD_DIM) // LANES))
    tile3 = lambda i, k: (i, k, 0)
    kv_spec = pl.BlockSpec((None,) + kvs_b.shape[1:], lambda i, k: (i, 0, 0, 0, 0))
    return pl.pallas_call(
        _prompt_attn_kernel,
        grid=(b, t // tq),
        in_specs=[
            pl.BlockSpec((None, N_HEADS, tq, HEAD_DIM), lambda i, k: (i, 0, k, 0)),
            kv_spec, kv_spec,
            pl.BlockSpec((None, tq, sel.shape[2]), tile3),
            _const_spec(expand.shape),
            pl.BlockSpec((None, tq, NSA_WIDTH), tile3),
            pl.BlockSpec((None, tq, LANES), tile3),
        ],
        out_specs=pl.BlockSpec((None, tq, NSA_WIDTH), tile3),
        out_shape=jax.ShapeDtypeStruct((b, t, NSA_WIDTH), BF16),
        scratch_shapes=[pltpu.VMEM((rows, k_ext), BF16), pltpu.VMEM((rows, HEAD_DIM), BF16),
                        pltpu.VMEM((t // tq + 1, rows, tq), F32),
                        pltpu.VMEM((rows, LANES), F32), pltpu.VMEM((rows, LANES), F32),
                        pltpu.VMEM((rows, HEAD_DIM), F32)],
        compiler_params=_params("arbitrary", "arbitrary"),
        name="prompt_attn",
    )(q_hm, kvs_b, kvw_b, sel, expand, oc, gates)


Q_PAD = 8


def _group_q(q_ref, g):
    return jnp.concatenate([q_ref[GROUP * g + r] for r in range(GROUP)], axis=0).astype(BF16)


def _two_piece_attention(qg, k_old, v_old, bias_old, k_new, v_new, bias_new):
    def scores(k_t, bias):
        s = jnp.dot(qg, k_t.astype(BF16), preferred_element_type=F32)
        nk = s.shape[1]
        return (s.reshape(GROUP, Q_PAD, nk) + bias[None]).reshape(GROUP * Q_PAD, nk)
    s_old = scores(k_old, bias_old)
    s_new = scores(k_new, bias_new)
    m = jnp.maximum(jnp.max(s_old, axis=-1, keepdims=True), jnp.max(s_new, axis=-1, keepdims=True))
    p_old = jnp.exp(s_old - m)
    p_new = jnp.exp(s_new - m)
    den = jnp.sum(p_old, axis=-1, keepdims=True) + jnp.sum(p_new, axis=-1, keepdims=True)
    pv = (lax.dot_general(p_old.astype(BF16), v_old.astype(BF16), _NT, preferred_element_type=F32)
          + lax.dot_general(p_new.astype(BF16), v_new.astype(BF16), _NT, preferred_element_type=F32))
    return pv / den


def _add_gated(prev_ref, gate_ref, o_ref, per_g, branch):
    prev, gates = prev_ref[...], gate_ref[...]
    for g in range(N_KV_HEADS):
        for r in range(GROUP):
            hd = GROUP * g + r
            cols = slice(hd * HEAD_DIM, (hd + 1) * HEAD_DIM)
            o_ref[:, cols] = (prev[:, cols] + gates[:, 3 * hd + branch:3 * hd + branch + 1]
                              * per_g[g][r * Q_PAD:(r + 1) * Q_PAD])


def _new_rows_bias(dec_seq):
    q = lax.broadcasted_iota(jnp.int32, (Q_PAD, LANES), 0)
    i = lax.broadcasted_iota(jnp.int32, (Q_PAD, LANES), 1) - (LANES - dec_seq)
    return jnp.where((i >= 0) & (i <= q), 0.0, NEG_INF)


def _sample_slc_kernel(pt_ref, cache_hbm, q_ref, sel_ref, new_ref, e_ref, prev_ref, gate_ref, o_ref,
                       buf, sem, *, dec_seq):
    n_pages = pt_ref.shape[1]

    def copy(page, slot, p):
        return pltpu.make_async_copy(cache_hbm.at[page],
                                     buf.at[slot, :, :, pl.ds(p * PAGE_SIZE, PAGE_SIZE)], sem.at[slot])

    def start_fetch(seq, slot):
        for p in range(n_pages):
            copy(pt_ref[seq, p], slot, p).start()

    def wait_fetch(slot):
        for p in range(n_pages):
            copy(0, slot, p).wait()

    slot = _paged_prologue(start_fetch, wait_fetch)
    nsp = sel_ref.shape[1] // N_KV_HEADS
    n_past_blk = n_pages * (PAGE_SIZE // SLC_BLOCK)
    new_bias = _new_rows_bias(dec_seq)
    n_exp = e_ref.shape[0]
    sel = [sel_ref[:, g * nsp:(g + 1) * nsp] for g in range(N_KV_HEADS)]
    sel_past = jnp.concatenate([s[:, :n_exp] for s in sel], axis=0).astype(BF16)
    bias_past = jnp.dot(sel_past, e_ref[...], preferred_element_type=F32)
    per_g = []
    for g in range(N_KV_HEADS):
        bias_old = bias_past[g * Q_PAD:(g + 1) * Q_PAD]
        bias_new = new_bias + sel[g][:, n_past_blk:n_past_blk + 1]
        per_g.append(_two_piece_attention(_group_q(q_ref, g), buf[slot, g], buf[slot, N_KV_HEADS + g], bias_old,
                                          new_ref[g], new_ref[N_KV_HEADS + g], bias_new))
    _add_gated(prev_ref, gate_ref, o_ref, per_g, 1)


def _sample_slc(page_table, cache, q_s, sel, new_t, expand, prev, gates, dec_seq):
    db, n_pages = page_table.shape
    past = n_pages * PAGE_SIZE
    seq3 = lambda i, pt: (i, 0, 0)
    seq4 = lambda i, pt: (i, 0, 0, 0)
    return pl.pallas_call(
        functools.partial(_sample_slc_kernel, dec_seq=dec_seq),
        grid_spec=pltpu.PrefetchScalarGridSpec(
            num_scalar_prefetch=1,
            grid=(db,),
            in_specs=[pl.BlockSpec(memory_space=pl.ANY),
                      pl.BlockSpec((None, N_HEADS, Q_PAD, HEAD_DIM), seq4),
                      pl.BlockSpec((None, Q_PAD, sel.shape[2]), seq3),
                      pl.BlockSpec((None, N_KV, HEAD_DIM, LANES), seq4),
                      pl.BlockSpec(expand.shape, lambda i, pt: (0, 0)),
                      pl.BlockSpec((None, Q_PAD, NSA_WIDTH), seq3),
                      pl.BlockSpec((None, Q_PAD, LANES), seq3)],
            out_specs=pl.BlockSpec((None, Q_PAD, NSA_WIDTH), seq3),
            scratch_shapes=[pltpu.VMEM((2, N_KV, HEAD_DIM, past), F32), pltpu.SemaphoreType.DMA((2,))],
        ),
        out_shape=jax.ShapeDtypeStruct((db, Q_PAD, NSA_WIDTH), F32),
        compiler_params=_params("arbitrary"),
        name="sample_slc",
    )(page_table, cache, q_s, sel, new_t, expand, prev, gates)


WIN_SEQS = 4


def _sample_win_kernel(q_ref, st_ref, new_ref, prev_ref, gate_ref, o_ref, st_out_ref, *, dec_seq):
    wb = st_ref.shape[-1]
    q = lax.broadcasted_iota(jnp.int32, (Q_PAD, wb), 0)
    i = lax.broadcasted_iota(jnp.int32, (Q_PAD, wb), 1)
    bias_old = jnp.where(wb + q - i <= WINDOW, 0.0, NEG_INF)
    new_bias = _new_rows_bias(dec_seq)
    lane = lax.broadcasted_iota(jnp.int32, (HEAD_DIM, wb), 1)
    per_seq = [[_two_piece_attention(_group_q(q_ref.at[b], g), st_ref[b, g], st_ref[b, N_KV_HEADS + g], bias_old,
                                     new_ref[b, g], new_ref[b, N_KV_HEADS + g], new_bias)
                for g in range(N_KV_HEADS)] for b in range(q_ref.shape[0])]
    for b, per_g in enumerate(per_seq):
        _add_gated(prev_ref.at[b], gate_ref.at[b], o_ref.at[b], per_g, 2)
        for j in range(N_KV):
            shifted = pltpu.roll(st_ref[b, j], wb - dec_seq, axis=1)
            st_out_ref[b, j] = jnp.where(lane >= wb - dec_seq, jnp.tile(new_ref[b, j], (1, wb // LANES)), shifted)


def _sample_win(q_s, st_win, new_t, prev, gates, dec_seq):
    db, _, _, wb = st_win.shape
    ns = WIN_SEQS if db % WIN_SEQS == 0 else 1
    seq3 = lambda i: (i, 0, 0)
    seq4 = lambda i: (i, 0, 0, 0)
    return pl.pallas_call(
        functools.partial(_sample_win_kernel, dec_seq=dec_seq),
        grid=(db // ns,),
        in_specs=[pl.BlockSpec((ns, N_HEADS, Q_PAD, HEAD_DIM), seq4),
                  pl.BlockSpec((ns, N_KV, HEAD_DIM, wb), seq4),
                  pl.BlockSpec((ns, N_KV, HEAD_DIM, LANES), seq4),
                  pl.BlockSpec((ns, Q_PAD, NSA_WIDTH), seq3),
                  pl.BlockSpec((ns, Q_PAD, LANES), seq3)],
        out_specs=[pl.BlockSpec((ns, Q_PAD, NSA_WIDTH), seq3), pl.BlockSpec((ns, N_KV, HEAD_DIM, wb), seq4)],
        out_shape=[jax.ShapeDtypeStruct((db, Q_PAD, NSA_WIDTH), F32),
                   jax.ShapeDtypeStruct((db, N_KV, HEAD_DIM, wb), F32)],
        compiler_params=_params("arbitrary"),
        name="sample_win",
    )(q_s, st_win, new_t, prev, gates)


def _pool_project(d_groups, wp_ref, scale_ref, o_ref):
    for gi, d in enumerate(d_groups):
        cols = slice(gi * POOL_GROUP_WIDTH, (gi + 1) * POOL_GROUP_WIDTH)
        y = jnp.dot(d.astype(BF16), wp_ref[gi], preferred_element_type=F32)
        o_ref[:, cols] = (y * scale_ref[:, cols]).astype(o_ref.dtype)


def _pool_tile(ext_ref, t0, wp_ref, scale_ref, o_ref):
    tm = o_ref.shape[0]
    pos = t0 + lax.broadcasted_iota(jnp.int32, (tm, POOL_GROUP_WIDTH), 0)
    d_groups = []
    for gi, w in enumerate(POOL_WINDOWS):
        cols = slice(gi * POOL_GROUP_WIDTH, (gi + 1) * POOL_GROUP_WIDTH)
        e = ext_ref[:, cols]
        acc = e
        span = 1
        while span < w:
            acc = acc + pltpu.roll(acc, span, axis=0)
            span *= 2
        cnt = jnp.minimum(w, pos + 1).astype(F32)
        d_groups.append(acc[POOL_HALO:] / cnt - e[POOL_HALO:])
    _pool_project(d_groups, wp_ref, scale_ref, o_ref)


def _pool_sample_kernel(ext_ref, wp_ref, scale_ref, o_ref, *, past_len, dec_seq):
    db = ext_ref.shape[1]
    for q in range(dec_seq):
        d_groups = []
        for gi, w in enumerate(POOL_WINDOWS):
            cols = slice(gi * POOL_GROUP_WIDTH, (gi + 1) * POOL_GROUP_WIDTH)
            row = POOL_HALO + q
            acc = ext_ref[row, :, cols]
            for i in range(1, w):
                acc = acc + ext_ref[row - i, :, cols]
            cnt = float(min(w, past_len + q + 1))
            d_groups.append(acc / cnt - ext_ref[row, :, cols])
        _pool_project(d_groups, wp_ref, scale_ref, o_ref.at[pl.ds(q * db, db)])


def _pool_sample(ext, wp, scale, past_len, dec_seq):
    rows, db, pw = ext.shape
    return pl.pallas_call(
        functools.partial(_pool_sample_kernel, past_len=past_len, dec_seq=dec_seq),
        grid=(1,),
        in_specs=[_const_spec(ext.shape), _const_spec(wp.shape), _const_spec(scale.shape)],
        out_specs=_const_spec((dec_seq * db, pw)),
        out_shape=jax.ShapeDtypeStruct((dec_seq * db, pw), BF16),
        compiler_params=_params("arbitrary"),
        name="pool_sample",
    )(ext, wp, scale)


def _cmp_weights(w1, w2, pe):
    n_slots = CMP_BLOCK // CMP_STRIDE
    w1s = w1.reshape(n_slots, CMP_FLAT, CMP_HIDDEN)
    w1cat = jnp.concatenate([w1s[h] for h in range(n_slots)], axis=1).astype(BF16)
    pe8 = jnp.pad(pe.reshape(n_slots, CMP_FLAT), ((0, SUBLANES - n_slots), (0, 0)))
    return w1cat, w2.astype(BF16), pe8


def _pad_axis(x, axis, size, front=False):
    pad = [(0, 0)] * x.ndim
    extra = size - x.shape[axis]
    pad[axis] = (extra, 0) if front else (0, extra)
    return jnp.pad(x, pad)


def _rows_to_state(x_fm):
    b, _, _, t = x_fm.shape
    return x_fm.reshape(b, 2, N_KV_HEADS, HEAD_DIM, t).transpose(0, 4, 1, 2, 3)


def _state_to_fm(x):
    b, r = x.shape[:2]
    return x.transpose(0, 2, 3, 4, 1).reshape(b, N_KV, HEAD_DIM, r)


def kernel(x_prompt, x_sample, cache_kv_cmp, cache_kv_slc, page_table, state_kv_win, state_pool, n_ffn1, w_ffn1_gate, w_ffn1_up, w_ffn1_down, n_mix, w_in, w_cmp_k1, w_cmp_k2, pe_cmp_k, w_cmp_v1, w_cmp_v2, pe_cmp_v, w_pool, pool_scale, w_out, n_ffn2, w_ffn2_gate, w_ffn2_up, w_ffn2_down, n_final):
    b, t, d = x_prompt.shape
    db, ds, _ = x_sample.shape
    depth = w_in.shape[0]
    n_pages = page_table.shape[1]
    past = n_pages * PAGE_SIZE
    wb = state_kv_win.shape[2]
    assert t % ATTN_TQ == 0 and t % INPROJ_TM == 0 and WINDOW % ATTN_TQ == 0 and t >= WINDOW
    assert ds <= Q_PAD and ds <= CMP_STRIDE and wb == WINDOW and past % SLC_BLOCK == 0
    assert (db * ds) % SUBLANES == 0 and n_pages % CMP_PAGE_GROUP == 0 and (t // PAGE_SIZE) % CMP_PAGE_GROUP == 0

    xp = x_prompt.reshape(b * t, d)
    xs = x_sample.reshape(db * ds, d)
    pos_p = jnp.arange(t, dtype=jnp.int32)
    pos_s = past + jnp.arange(ds, dtype=jnp.int32)
    tab_p = _rope_tables(pos_p)
    tab_s = tuple(jnp.tile(a, (db, 1)) for a in _rope_tables(pos_s)[:3]) + tuple(
        jnp.tile(a, (1, db)) for a in _rope_tables(pos_s)[3:])

    n_chunk_p = t // CMP_STRIDE
    n_slc_p = -(-t // SLC_BLOCK)
    agg_p = _agg_matrix(n_chunk_p, n_chunk_p - 1, n_slc_p, LANES * (-(-n_slc_p // LANES)))
    expand_p = _expand_matrix(t, agg_p.shape[1], ATTN_TQ)
    n_chunk_s = past // CMP_STRIDE
    n_slc_s = -(-(past + ds) // SLC_BLOCK)
    agg_s = _agg_matrix(n_chunk_s, n_chunk_s, n_slc_s, LANES * (-(-n_slc_s // LANES)))
    expand_s = _expand_matrix(past, LANES * (-(-(past // SLC_BLOCK) // LANES)), past)[0]

    st_p = ([], [], [], [])
    st_s = ([], [], [], [])
    for l in range(depth):
        last = l == depth - 1
        ffn1 = (n_ffn1[l], w_ffn1_gate[l].astype(BF16), w_ffn1_up[l].astype(BF16), w_ffn1_down[l].astype(BF16))
        ffn2 = (n_ffn2[l], w_ffn2_gate[l].astype(BF16), w_ffn2_up[l].astype(BF16), w_ffn2_down[l].astype(BF16))
        w = w_in[l]
        o_kv, o_gate, o_pool = NSA_WIDTH, NSA_WIDTH + 3 * KV_WIDTH, NSA_WIDTH + 3 * KV_WIDTH + N_GATES
        wq = w[:, :o_kv].astype(BF16)
        wkv_t = w[:, o_kv:o_gate].T.astype(BF16)
        wgt = _pad_axis(w[:, o_gate:o_pool], 1, LANES).astype(BF16)
        wu = w[:, o_pool:].astype(BF16)
        proj = (n_mix[l], wq, wkv_t, wgt, wu)
        wk1, wk2, pek = _cmp_weights(w_cmp_k1[l], w_cmp_k2[l], pe_cmp_k[l])
        wv1, wv2, pev = _cmp_weights(w_cmp_v1[l], w_cmp_v2[l], pe_cmp_v[l])
        pe = jnp.stack([pek, pev])
        wp = w_pool[l].astype(BF16)
        scale = pool_scale[l].reshape(1, POOL_WIDTH)
        wo_nsa = w_out[l][:NSA_WIDTH].astype(BF16)
        wo_pool = w_out[l][NSA_WIDTH:].astype(BF16)

        xp = _ffn(xp, *ffn1)
        q_hm, q2_hm, kvc, kvs, kvw, kvs_b, kvw_b, gates, pool_out, u_tail = _inproj(
            xp, *proj, tab_p, BF16, pool=(wp, scale))
        kc, vc = _compress_prompt(kvc, pe, wk1, wv1, wk2, wv2)
        gates3 = gates.reshape(b, t, LANES)
        oc, sel = _cmp_attn(q_hm, kc, vc, gates3, agg_p, tq=ATTN_TQ, pos0=0, tiled=True,
                            n_slc=n_slc_p, sel_dtype=BF16)
        o_mix = _prompt_attn(q2_hm, kvs_b, kvw_b, sel, expand_p, oc, gates3)
        mix = (o_mix.reshape(b * t, NSA_WIDTH), pool_out, wo_nsa, wo_pool)
        xp = _ffn(xp, *ffn2, g_final=n_final if last else None, mix=mix)
        st_p[0].append(_rows_to_state(kvc))
        st_p[1].append(_rows_to_state(kvs))
        st_p[2].append(_rows_to_state(kvw[..., t - min(WINDOW, t):]))
        st_p[3].append(u_tail[:, POOL_HALO - POOL_STATE:])

        xs = _ffn(xs, *ffn1)
        q, _, kvc, kvs, kvw, _, _, gates, u = _inproj(xs, *proj, tab_s, F32)
        per_seq = lambda a: a.reshape(N_KV, HEAD_DIM, db, ds).transpose(2, 0, 1, 3)
        kvc_n, kvs_n, kvw_n = per_seq(kvc), per_seq(kvs), per_seq(kvw)
        ynew = _pad_axis(kvc_n.transpose(0, 1, 3, 2), 2, CMP_STRIDE).reshape(db, N_KV, CMP_FLAT)
        cache_c = _state_to_fm(cache_kv_cmp[l])
        kc, vc = _compress_sample(page_table, cache_c, _pad_axis(ynew, 1, SUBLANES), pe, wk1, wv1, wk2, wv2)
        q_s = _pad_axis(q.reshape(N_HEADS, db, ds, HEAD_DIM).transpose(1, 0, 2, 3), 2, Q_PAD)
        gates_s = _pad_axis(gates.reshape(db, ds, LANES), 1, Q_PAD)
        oc, p_slc = _cmp_attn(q_s, kc, vc, gates_s, agg_s, tq=Q_PAD, pos0=past, tiled=False,
                              n_slc=n_slc_s, sel_dtype=F32)
        sel = _select_all(p_slc, pos0=past, n_slc=n_slc_s)
        o_cs =_sample_slc(page_table, _state_to_fm(cache_kv_slc[l]), q_s, sel,
                           _pad_axis(kvs_n, 3, LANES, front=True), expand_s, oc, gates_s, ds)
        o_mix, st_win_new = _sample_win(q_s, _state_to_fm(state_kv_win[l]),
                                        _pad_axis(kvw_n, 3, LANES, front=True), o_cs, gates_s, ds)
        u3 = u.reshape(db, ds, POOL_WIDTH)
        ext = jnp.concatenate([jnp.zeros((db, POOL_HALO - POOL_STATE, POOL_WIDTH), F32), state_pool[l], u3], axis=1)
        pool_out = _pool_sample(ext.transpose(1, 0, 2), wp, scale, past, ds)
        pool_out = pool_out.reshape(ds, db, POOL_WIDTH).transpose(1, 0, 2).reshape(db * ds, POOL_WIDTH)
        mix = (o_mix[:, :ds].reshape(db * ds, NSA_WIDTH), pool_out, wo_nsa, wo_pool)
        xs = _ffn(xs, *ffn2, g_final=n_final if last else None, mix=mix)
        st_s[0].append(_rows_to_state(kvc_n))
        st_s[1].append(_rows_to_state(kvs_n))
        st_s[2].append(_rows_to_state(st_win_new))
        st_s[3].append(jnp.concatenate([state_pool[l], u3], axis=1)[:, ds:])

    return (xp.reshape(b, t, d), xs.reshape(db, ds, d),
            jnp.stack(st_p[0]), jnp.stack(st_p[1]), jnp.stack(st_p[2]), jnp.stack(st_p[3]),
            jnp.stack(st_s[0]), jnp.stack(st_s[1]), jnp.stack(st_s[2]), jnp.stack(st_s[3]))
```

```python
import functools

import numpy as np
import jax
import jax.numpy as jnp
from jax import lax
from jax.experimental import pallas as pl
from jax.experimental.pallas import tpu as pltpu

F32 = jnp.float32
BF16 = jnp.bfloat16

HEAD_DIM = 64
N_HEADS = 8
N_KV_HEADS = 2
GROUP = N_HEADS // N_KV_HEADS
NSA_WIDTH = N_HEADS * HEAD_DIM
N_KV = 2 * N_KV_HEADS
KV_WIDTH = N_KV * HEAD_DIM
ROPE_DIM = HEAD_DIM // 4
ROPE_HALF = ROPE_DIM // 2
ROPE_THETA = 500000.0
CMP_BLOCK = 32
CMP_STRIDE = 16
CMP_HIDDEN = 256
CMP_FLAT = CMP_STRIDE * HEAD_DIM
SLC_BLOCK = 64
TOPK_BLOCKS = 16
WINDOW = 512
PAGE_SIZE = 128
POOL_WINDOWS = (2, 4, 8, 16)
POOL_GROUP_WIDTH = 128
POOL_WIDTH = POOL_GROUP_WIDTH * len(POOL_WINDOWS)
POOL_STATE = max(POOL_WINDOWS) - 1
POOL_HALO = 16
RMS_EPS = 1e-6
FORCE_SCORE = 1e4
NEG_INF = -1e30
ATTN_SCALE = HEAD_DIM ** -0.5
LOG2_E = float(np.log2(np.e))
N_GATES = 3 * N_HEADS

LANES = 128
SUBLANES = 8
VMEM_LIMIT_BYTES = 56 * 1024 * 1024

_NT = (((1,), (1,)), ((), ()))


def _params(*sem):
    return pltpu.CompilerParams(dimension_semantics=sem, vmem_limit_bytes=VMEM_LIMIT_BYTES)


def _rmsnorm(x, g):
    return x * lax.rsqrt(jnp.mean(x * x, axis=-1, keepdims=True) + RMS_EPS) * g


def _const_spec(shape):
    n = len(shape)
    return pl.BlockSpec(shape, lambda *_: (0,) * n)


FFN_TM = 512
FFN_CHUNK = 256


def _ffn_kernel(x_ref, g_ref, wg_ref, wu_ref, wd_ref, gf_ref, *rest, final_norm, mix):
    if mix:
        mo_ref, mp_ref, wo_ref, wp_ref, o_ref, act_ref = rest
        x = (x_ref[...] + jnp.dot(mo_ref[...].astype(BF16), wo_ref[...], preferred_element_type=F32)
             + jnp.dot(mp_ref[...], wp_ref[...], preferred_element_type=F32))
    else:
        o_ref, act_ref = rest
        x = x_ref[...]
    h = _rmsnorm(x, g_ref[...]).astype(BF16)
    d_ff = wg_ref.shape[1]
    for c in range(d_ff // FFN_CHUNK):
        sl = slice(c * FFN_CHUNK, (c + 1) * FFN_CHUNK)
        a = jnp.dot(h, wg_ref[:, sl], preferred_element_type=F32)
        u = jnp.dot(h, wu_ref[:, sl], preferred_element_type=F32)
        act_ref[:, sl] = (a * jax.nn.sigmoid(a) * u).astype(BF16)
    y = x + 0.5 * jnp.dot(act_ref[...], wd_ref[...], preferred_element_type=F32)
    if final_norm:
        y = _rmsnorm(y, gf_ref[...])
    o_ref[...] = y


def _ffn(x, g, wg, wu, wd, g_final=None, mix=None):
    m, d = x.shape
    d_ff = wg.shape[1]
    tm = min(FFN_TM, m)
    final_norm = g_final is not None
    gf = g_final if final_norm else g
    row = lambda i: (i, 0)
    once = lambda shape: pl.BlockSpec(shape, lambda i: (0, 0), pipeline_mode=pl.Buffered(1))
    in_specs = [pl.BlockSpec((tm, d), row), _const_spec((1, d)),
                once((d, d_ff)), once((d, d_ff)), once((d_ff, d)), _const_spec((1, d))]
    args = [x, g.reshape(1, d), wg, wu, wd, gf.reshape(1, d)]
    if mix is not None:
        mo, mp, wo, wp = mix
        in_specs += [pl.BlockSpec((tm, mo.shape[1]), row), pl.BlockSpec((tm, mp.shape[1]), row),
                     once(wo.shape), once(wp.shape)]
        args += [mo, mp, wo, wp]
    return pl.pallas_call(
        functools.partial(_ffn_kernel, final_norm=final_norm, mix=mix is not None),
        grid=(m // tm,),
        in_specs=in_specs,
        out_specs=pl.BlockSpec((tm, d), row),
        out_shape=jax.ShapeDtypeStruct((m, d), F32),
        scratch_shapes=[pltpu.VMEM((tm, d_ff), BF16)],
        compiler_params=_params("arbitrary"),
        name="ffn",
    )(*args)


INPROJ_TM = 512
ATTN_TQ = 256


def _rope_tables(pos):
    inv = jnp.power(ROPE_THETA, -jnp.arange(ROPE_HALF, dtype=F32) / ROPE_HALF)
    ang = pos.astype(F32)[:, None] * inv[None, :]
    cos, sin = jnp.cos(ang), jnp.sin(ang)
    t = pos.shape[0]
    rest = HEAD_DIM - ROPE_DIM
    z_half = jnp.zeros((t, ROPE_HALF), F32)
    z_rest = jnp.zeros((t, rest), F32)
    cos_t = jnp.concatenate([cos, cos, jnp.ones((t, rest), F32)], axis=1)
    sin_a = jnp.concatenate([-sin, z_half, z_rest], axis=1)
    sin_b = jnp.concatenate([z_half, sin, z_rest], axis=1)
    rep = LANES // HEAD_DIM
    return tuple(jnp.tile(a, (1, rep)) for a in (cos_t, sin_a, sin_b)) + (cos.T, sin.T)


def _rope(z, cos_t, sin_a, sin_b):
    outs = []
    for c in range(z.shape[1] // LANES):
        zc = z[:, c * LANES:(c + 1) * LANES]
        outs.append(zc * cos_t + pltpu.roll(zc, LANES - ROPE_HALF, axis=1) * sin_a
                    + pltpu.roll(zc, ROPE_HALF, axis=1) * sin_b)
    return outs[0] if len(outs) == 1 else jnp.concatenate(outs, axis=1)


def _rope_fm(kt, cos, sin):
    x1, x2 = kt[0:ROPE_HALF], kt[ROPE_HALF:ROPE_DIM]
    return jnp.concatenate([x1 * cos - x2 * sin, x2 * cos + x1 * sin, kt[ROPE_DIM:]], axis=0)


def _inproj_kernel(x_ref, g_ref, wq_ref, wkv_t_ref, wgt_ref, wu_ref,
                   cos_ref, sa_ref, sb_ref, cos_fm_ref, sin_fm_ref, *rest, n_tab, pool):
    if pool:
        (wp_ref, scale_ref, q_ref, q2_ref, kvc_ref, kvs_ref, kvw_ref, kvs_b_ref, kvw_b_ref, gate_ref,
         pool_ref, u_tail_ref, carry_ref, ext_ref) = rest
    else:
        q_ref, q2_ref, kvc_ref, kvs_ref, kvw_ref, kvs_b_ref, kvw_b_ref, gate_ref, u_ref = rest
    h = _rmsnorm(x_ref[...], g_ref[...]).astype(BF16)
    tm = h.shape[0]
    tk = kvs_b_ref.shape[-1]
    cos_t, sin_a, sin_b = cos_ref[...], sa_ref[...], sb_ref[...]
    cos_fm, sin_fm = cos_fm_ref[...], sin_fm_ref[...]
    q = _rope(jnp.dot(h, wq_ref[...], preferred_element_type=F32), cos_t, sin_a, sin_b) * ATTN_SCALE
    q2 = q * LOG2_E
    for hd in range(N_HEADS):
        q_ref[hd] = q[:, hd * HEAD_DIM:(hd + 1) * HEAD_DIM].astype(q_ref.dtype)
        q2_ref[hd] = q2[:, hd * HEAD_DIM:(hd + 1) * HEAD_DIM].astype(BF16)
    kv_t = lax.dot_general(wkv_t_ref[...], h, _NT, preferred_element_type=F32)
    for i, (f_ref, b_ref) in enumerate(((kvc_ref, None), (kvs_ref, kvs_b_ref), (kvw_ref, kvw_b_ref))):
        for j in range(N_KV):
            r0 = i * KV_WIDTH + j * HEAD_DIM
            blk = kv_t[r0:r0 + HEAD_DIM]
            if j < N_KV_HEADS:
                blk = _rope_fm(blk, cos_fm, sin_fm)
            f_ref[j] = blk
            if b_ref is not None:
                for c in range(tm // tk):
                    b_ref[j, c] = blk[:, c * tk:(c + 1) * tk].astype(BF16)
    gate_ref[...] = jax.nn.sigmoid(jnp.dot(h, wgt_ref[...], preferred_element_type=F32))
    u = jnp.dot(h, wu_ref[...], preferred_element_type=F32)
    if pool:
        ti = lax.rem(pl.program_id(0), n_tab)
        ext_ref[0:POOL_HALO, :] = jnp.where(ti > 0, carry_ref[...], 0.0)
        ext_ref[POOL_HALO:, :] = u
        carry_ref[...] = u[tm - POOL_HALO:]
        u_tail_ref[...] = u[tm - POOL_HALO:]
        _pool_tile(ext_ref, ti * tm, wp_ref, scale_ref, pool_ref)
    else:
        u_ref[...] = u


def _inproj(x, g, wq, wkv_t, wgt, wu, tables, q_dtype, pool=None):
    m, d = x.shape
    seq = tables[0].shape[0]
    nseq = m // seq
    tm = min(INPROJ_TM, seq)
    tk = min(ATTN_TQ, tm)
    n_tab = seq // tm
    row = lambda i: (i, 0)
    tab = lambda i: (i % n_tab, 0)
    tab_fm = lambda i: (0, i % n_tab)
    hm = lambda i: (i // n_tab, 0, i % n_tab, 0)
    fm = lambda i: (i // n_tab, 0, 0, i % n_tab)
    fmb = lambda i: (i // n_tab, 0, i % n_tab, 0, 0)
    fm_shape = jax.ShapeDtypeStruct((nseq, N_KV, HEAD_DIM, seq), F32)
    fmb_shape = jax.ShapeDtypeStruct((nseq, N_KV, seq // tk, HEAD_DIM, tk), BF16)
    fm_spec = pl.BlockSpec((None, N_KV, HEAD_DIM, tm), fm)
    fmb_spec = pl.BlockSpec((None, N_KV, tm // tk, HEAD_DIM, tk), fmb)
    in_specs = [
        pl.BlockSpec((tm, d), row), _const_spec((1, d)),
        _const_spec(wq.shape), _const_spec(wkv_t.shape), _const_spec(wgt.shape), _const_spec(wu.shape),
        pl.BlockSpec((tm, LANES), tab), pl.BlockSpec((tm, LANES), tab), pl.BlockSpec((tm, LANES), tab),
        pl.BlockSpec((ROPE_HALF, tm), tab_fm), pl.BlockSpec((ROPE_HALF, tm), tab_fm),
    ]
    out_specs = [
        pl.BlockSpec((None, N_HEADS, tm, HEAD_DIM), hm), pl.BlockSpec((None, N_HEADS, tm, HEAD_DIM), hm),
        fm_spec, fm_spec, fm_spec, fmb_spec, fmb_spec,
        pl.BlockSpec((tm, LANES), row),
    ]
    out_shape = [
        jax.ShapeDtypeStruct((nseq, N_HEADS, seq, HEAD_DIM), q_dtype),
        jax.ShapeDtypeStruct((nseq, N_HEADS, seq, HEAD_DIM), BF16),
        fm_shape, fm_shape, fm_shape, fmb_shape, fmb_shape,
        jax.ShapeDtypeStruct((m, LANES), F32),
    ]
    args = [x, g.reshape(1, d), wq, wkv_t, wgt, wu, *tables]
    scratch = []
    if pool is not None:
        in_specs += [_const_spec(pool[0].shape), _const_spec(pool[1].shape)]
        args += list(pool)
        out_specs += [pl.BlockSpec((tm, POOL_WIDTH), row),
                      pl.BlockSpec((None, POOL_HALO, POOL_WIDTH), lambda i: (i // n_tab, 0, 0))]
        out_shape += [jax.ShapeDtypeStruct((m, POOL_WIDTH), BF16),
                      jax.ShapeDtypeStruct((nseq, POOL_HALO, POOL_WIDTH), F32)]
        scratch = [pltpu.VMEM((POOL_HALO, POOL_WIDTH), F32), pltpu.VMEM((POOL_HALO + tm, POOL_WIDTH), F32)]
    else:
        out_specs += [pl.BlockSpec((tm, POOL_WIDTH), row)]
        out_shape += [jax.ShapeDtypeStruct((m, POOL_WIDTH), F32)]
    return pl.pallas_call(
        functools.partial(_inproj_kernel, n_tab=n_tab, pool=pool is not None),
        grid=(m // tm,),
        in_specs=in_specs,
        out_specs=out_specs,
        out_shape=out_shape,
        scratch_shapes=scratch,
        compiler_params=_params("arbitrary"),
        name="inproj",
    )(*args)


def _gelu_tanh(x):
    return 0.5 * x * (1.0 + jnp.tanh(np.sqrt(2.0 / np.pi).astype(np.float32) * (x + 0.044715 * (x * x * x))))


def _pick_row(blk, r):
    row8 = lax.broadcasted_iota(jnp.int32, (SUBLANES, 1), 0)
    return jnp.sum(jnp.where(row8 == r, blk, 0.0), axis=0, keepdims=True)


def _compress_finish(kv, n, new8, pe_ref, w1_ref, w2_ref, out_ref, y_ref, has_new):
    y_ref[pl.ds(2 * n, 2 * SUBLANES), :] = jnp.concatenate([new8, pe_ref[kv]], axis=0).astype(BF16)
    p = jnp.dot(y_ref[...], w1_ref[...], preferred_element_type=F32)
    p_new, p_pe = p[2 * n:2 * n + SUBLANES], p[2 * n + SUBLANES:2 * n + 2 * SUBLANES]
    bias = _pick_row(p_pe[:, :CMP_HIDDEN], 0) + _pick_row(p_pe[:, CMP_HIDDEN:], 1)
    last = lax.broadcasted_iota(jnp.int32, (n, CMP_HIDDEN), 0) == n - 1
    for g in range(N_KV_HEADS):
        slot0 = p[g * n:(g + 1) * n, :CMP_HIDDEN]
        slot1 = pltpu.roll(p[g * n:(g + 1) * n, CMP_HIDDEN:], n - 1, axis=0)
        if has_new:
            slot1 = jnp.where(last, _pick_row(p_new[:, CMP_HIDDEN:], g), slot1)
        hid = _gelu_tanh(slot0 + slot1 + bias).astype(BF16)
        out_ref[g] = jnp.dot(hid, w2_ref[...], preferred_element_type=F32).astype(BF16)


CMP_PAGE_GROUP = 8


def _dechunk_perm():
    perm = np.zeros((N_KV_HEADS * PAGE_SIZE, 2 * LANES), np.float32)
    cpp = PAGE_SIZE // CMP_STRIDE
    for g in range(N_KV_HEADS):
        for c in range(cpp):
            for j in range(CMP_STRIDE // 2):
                for par in range(2):
                    perm[g * PAGE_SIZE + CMP_STRIDE * c + 2 * j + par, par * LANES + g * HEAD_DIM + j * cpp + c] = 1.0
    return jnp.asarray(perm, BF16)


def _regroup_pages(load_tile, n_pages, n, perm_ref, y_ref):
    pg = CMP_PAGE_GROUP
    cpp = PAGE_SIZE // CMP_STRIDE
    for grp in range(n_pages // pg):
        lhs = jnp.concatenate(
            [jnp.concatenate([load_tile(grp * pg + q, g) for g in range(N_KV_HEADS)], axis=1)
             for q in range(pg)], axis=0).astype(BF16)
        out = jnp.dot(lhs, perm_ref[...], preferred_element_type=F32)
        x = jnp.concatenate(
            [jnp.concatenate([out[q * HEAD_DIM:(q + 1) * HEAD_DIM, :LANES],
                              out[q * HEAD_DIM:(q + 1) * HEAD_DIM, LANES:]], axis=0) for q in range(pg)], axis=1)
        xt = x.T
        r0 = grp * (pg * cpp)
        for g in range(N_KV_HEADS):
            for j in range(CMP_STRIDE // 2):
                piece = jnp.concatenate(
                    [xt[q * LANES + g * HEAD_DIM + j * cpp:q * LANES + g * HEAD_DIM + (j + 1) * cpp]
                     for q in range(pg)], axis=0)
                y_ref[g * n + r0:g * n + r0 + pg * cpp, j * LANES:(j + 1) * LANES] = piece.astype(BF16)


def _compress_prompt_kernel(x_ref, pe_ref, perm_ref, wk1_ref, wv1_ref, wk2_ref, wv2_ref, kc_ref, vc_ref, y_ref):
    t = x_ref.shape[-1]
    n = t // CMP_STRIDE
    for kv, (w1_ref, w2_ref, out_ref) in enumerate(((wk1_ref, wk2_ref, kc_ref), (wv1_ref, wv2_ref, vc_ref))):
        _regroup_pages(lambda page, g: x_ref[N_KV_HEADS * kv + g, :, page * PAGE_SIZE:(page + 1) * PAGE_SIZE],
                       t // PAGE_SIZE, n, perm_ref, y_ref.at[kv])
        _compress_finish(kv, n, jnp.zeros((SUBLANES, CMP_FLAT), F32), pe_ref, w1_ref, w2_ref, out_ref,
                         y_ref.at[kv], False)


def _compress_prompt(x, pe, wk1, wv1, wk2, wv2):
    b, _, _, t = x.shape
    n = t // CMP_STRIDE
    perm = _dechunk_perm()
    out = jax.ShapeDtypeStruct((b, N_KV_HEADS, n, HEAD_DIM), BF16)
    out_spec = pl.BlockSpec((None, N_KV_HEADS, n, HEAD_DIM), lambda i: (i, 0, 0, 0))
    return pl.pallas_call(
        _compress_prompt_kernel,
        grid=(b,),
        in_specs=[pl.BlockSpec((None, N_KV, HEAD_DIM, t), lambda i: (i, 0, 0, 0)), _const_spec(pe.shape),
                  _const_spec(perm.shape),
                  _const_spec(wk1.shape), _const_spec(wv1.shape), _const_spec(wk2.shape), _const_spec(wv2.shape)],
        out_specs=[out_spec, out_spec],
        out_shape=[out, out],
        scratch_shapes=[pltpu.VMEM((2, 2 * n + 2 * SUBLANES, CMP_FLAT), BF16)],
        compiler_params=_params("arbitrary"),
        name="compress_prompt",
    )(x, pe, perm, wk1, wv1, wk2, wv2)


def _paged_prologue(start_fetch, wait_fetch):
    b = pl.program_id(0)
    slot = lax.rem(b, 2)

    @pl.when(b == 0)
    def _():
        start_fetch(0, 0)

    @pl.when(b + 1 < pl.num_programs(0))
    def _():
        start_fetch(b + 1, 1 - slot)

    wait_fetch(slot)
    return slot


def _compress_sample_kernel(pt_ref, cache_hbm, ynew_ref, pe_ref, perm_ref, wk1_ref, wv1_ref, wk2_ref, wv2_ref,
                            kc_ref, vc_ref, buf, sem, y_ref):
    n_pages = pt_ref.shape[1]
    n = n_pages * (PAGE_SIZE // CMP_STRIDE)

    def copy(page, slot, p):
        return pltpu.make_async_copy(cache_hbm.at[page], buf.at[slot, p], sem.at[slot])

    def start_fetch(seq, slot):
        for p in range(n_pages):
            copy(pt_ref[seq, p], slot, p).start()

    def wait_fetch(slot):
        for p in range(n_pages):
            copy(0, slot, p).wait()

    slot = _paged_prologue(start_fetch, wait_fetch)
    ynew = ynew_ref[...]
    row8 = lax.broadcasted_iota(jnp.int32, (SUBLANES, 1), 0)
    for kv, (w1_ref, w2_ref, out_ref) in enumerate(((wk1_ref, wk2_ref, kc_ref), (wv1_ref, wv2_ref, vc_ref))):
        _regroup_pages(lambda page, g: buf[slot, page, N_KV_HEADS * kv + g], n_pages, n, perm_ref, y_ref.at[kv])
        new8 = jnp.where(row8 == 0, _pick_row(ynew, N_KV_HEADS * kv),
                         jnp.where(row8 == 1, _pick_row(ynew, N_KV_HEADS * kv + 1), 0.0))
        _compress_finish(kv, n, new8, pe_ref, w1_ref, w2_ref, out_ref, y_ref.at[kv], True)


def _compress_sample(page_table, cache, ynew, pe, wk1, wv1, wk2, wv2):
    db, n_pages = page_table.shape
    n = n_pages * (PAGE_SIZE // CMP_STRIDE)
    out = jax.ShapeDtypeStruct((db, N_KV_HEADS, n, HEAD_DIM), BF16)
    out_spec = pl.BlockSpec((None, N_KV_HEADS, n, HEAD_DIM), lambda i, pt: (i, 0, 0, 0))
    const = lambda s: pl.BlockSpec(s, lambda i, pt: (0,) * len(s))
    perm = _dechunk_perm()
    return pl.pallas_call(
        _compress_sample_kernel,
        grid_spec=pltpu.PrefetchScalarGridSpec(
            num_scalar_prefetch=1,
            grid=(db,),
            in_specs=[pl.BlockSpec(memory_space=pl.ANY),
                      pl.BlockSpec((None, SUBLANES, CMP_FLAT), lambda i, pt: (i, 0, 0)),
                      const(pe.shape), const(perm.shape),
                      const(wk1.shape), const(wv1.shape), const(wk2.shape), const(wv2.shape)],
            out_specs=[out_spec, out_spec],
            scratch_shapes=[pltpu.VMEM((2, n_pages, N_KV, HEAD_DIM, PAGE_SIZE), F32),
                            pltpu.SemaphoreType.DMA((2,)),
                            pltpu.VMEM((2, 2 * n + 2 * SUBLANES, CMP_FLAT), BF16)],
        ),
        out_shape=[out, out],
        compiler_params=_params("arbitrary"),
        name="compress_sample",
    )(page_table, cache, ynew, pe, perm, wk1, wv1, wk2, wv2)


def _select_mask(p_slc, t_pos, j, n_slc):
    cur = t_pos // SLC_BLOCK
    forced = (j == 0) | (j == cur) | (j == cur - 1)
    real = j < n_slc
    score = jnp.where(forced, FORCE_SCORE, jnp.where(j <= cur, p_slc, NEG_INF))
    score = jnp.where(real, score, -jnp.inf)
    rank = jnp.zeros(score.shape, jnp.int32)
    for jp in range(n_slc):
        c = score[jp:jp + 1, :]
        rank = rank + ((c > score) | ((c == score) & (j > jp))).astype(jnp.int32)
    return jnp.where((rank < TOPK_BLOCKS) & real, 0.0, NEG_INF)


def _select_all_kernel(p_ref, o_ref, *, pos0, q_pad, n_slc):
    nsp, lanes = p_ref.shape
    n_rows = SUBLANES * (-(-n_slc // SUBLANES))
    t_pos = pos0 + lax.rem(lax.broadcasted_iota(jnp.int32, (n_rows, lanes), 1), q_pad)
    j = lax.broadcasted_iota(jnp.int32, (n_rows, lanes), 0)
    o_ref[0:n_rows, :] = _select_mask(p_ref[0:n_rows, :], t_pos, j, n_slc)
    if n_rows < nsp:
        o_ref[n_rows:, :] = jnp.zeros((nsp - n_rows, lanes), F32)


def _select_all(p_slc, *, pos0, n_slc):
    db, q_pad, width = p_slc.shape
    nsp = width // N_KV_HEADS
    p_t = p_slc.reshape(db, q_pad, N_KV_HEADS, nsp).transpose(3, 0, 2, 1).reshape(nsp, db * N_KV_HEADS * q_pad)
    mask_t = pl.pallas_call(
        functools.partial(_select_all_kernel, pos0=pos0, q_pad=q_pad, n_slc=n_slc),
        grid=(1,),
        in_specs=[_const_spec(p_t.shape)],
        out_specs=_const_spec(p_t.shape),
        out_shape=jax.ShapeDtypeStruct(p_t.shape, F32),
        compiler_params=_params("arbitrary"),
        name="select_all",
    )(p_t)
    return mask_t.reshape(nsp, db, N_KV_HEADS, q_pad).transpose(1, 3, 2, 0).reshape(db, q_pad, width)


CMP_ATTN_SEQS = 4


def _cmp_attn_kernel(q_ref, kc_ref, vc_ref, gate_ref, agg_ref, oc_ref, sel_ref, **static):
    results = [_cmp_attn_one(q_ref.at[i], kc_ref.at[i], vc_ref.at[i], gate_ref.at[i], agg_ref, **static)
               for i in range(q_ref.shape[0])]
    for i, (heads, masks) in enumerate(results):
        nsp = masks[0].shape[1]
        for g, mask in enumerate(masks):
            sel_ref[i, :, g * nsp:(g + 1) * nsp] = mask.astype(sel_ref.dtype)
        for hd, o in enumerate(heads):
            oc_ref[i, :, hd * HEAD_DIM:(hd + 1) * HEAD_DIM] = o


def _cmp_attn_one(q_ref, kc_ref, vc_ref, gate_ref, agg_ref, *, pos0, tiled, n_slc, blocks_major):
    tq = q_ref.shape[1]
    n = kc_ref.shape[1]
    nsp = agg_ref.shape[0] if blocks_major else agg_ref.shape[1]
    t0 = pos0 + (pl.program_id(1) * tq if tiled else 0)
    t_n = t0 + lax.broadcasted_iota(jnp.int32, (tq, n), 0)
    blk_end = lax.broadcasted_iota(jnp.int32, (tq, n), 1) * CMP_STRIDE + (CMP_BLOCK - 1)
    valid = (blk_end <= t_n)[None]
    if blocks_major:
        n_rows = SUBLANES * (-(-n_slc // SUBLANES))
        t_s = t0 + lax.broadcasted_iota(jnp.int32, (n_rows, tq), 1)
        j = lax.broadcasted_iota(jnp.int32, (n_rows, tq), 0)
        blk_end_t = lax.broadcasted_iota(jnp.int32, (n, GROUP * tq), 0) * CMP_STRIDE + (CMP_BLOCK - 1)
        valid_t = blk_end_t <= t0 + lax.rem(lax.broadcasted_iota(jnp.int32, (n, GROUP * tq), 1), tq)
    gates = gate_ref[...]
    heads, masks = [], []
    for g in range(N_KV_HEADS):
        qg = jnp.concatenate([q_ref[GROUP * g + r].astype(F32) for r in range(GROUP)], axis=0).astype(BF16)
        if blocks_major:
            s = lax.dot_general(kc_ref[g], qg, _NT, preferred_element_type=F32)
            s = jnp.where(valid_t, s, NEG_INF)
            e = jnp.where(valid_t, jnp.exp(s - jnp.max(s, axis=0, keepdims=True)), 0.0)
            p_t = e / jnp.maximum(jnp.sum(e, axis=0, keepdims=True), 1e-30)
            o = jnp.dot(p_t.T.astype(BF16), vc_ref[g], preferred_element_type=F32)
            p_grp = p_t[:, :tq]
            for r in range(1, GROUP):
                p_grp = p_grp + p_t[:, r * tq:(r + 1) * tq]
        else:
            s = lax.dot_general(qg, kc_ref[g], _NT, preferred_element_type=F32).reshape(GROUP, tq, n)
            s = jnp.where(valid, s, NEG_INF)
            e = jnp.where(valid, jnp.exp(s - jnp.max(s, axis=-1, keepdims=True)), 0.0)
            p = e / jnp.maximum(jnp.sum(e, axis=-1, keepdims=True), 1e-30)
            o = jnp.dot(p.reshape(GROUP * tq, n).astype(BF16), vc_ref[g], preferred_element_type=F32)
            p_grp = jnp.sum(p, axis=0)
        p_hi = p_grp.astype(BF16)
        p_lo = (p_grp - p_hi.astype(F32)).astype(BF16)
        if blocks_major:
            p_slc = (jnp.dot(agg_ref[...], p_hi, preferred_element_type=F32)
                     + jnp.dot(agg_ref[...], p_lo, preferred_element_type=F32))
            mask = _select_mask(p_slc[:n_rows], t_s, j, n_slc)
            if n_rows < nsp:
                mask = jnp.concatenate([mask, jnp.zeros((nsp - n_rows, tq), F32)], axis=0)
            mask = mask.T
        else:
            mask = (jnp.dot(p_hi, agg_ref[...], preferred_element_type=F32)
                    + jnp.dot(p_lo, agg_ref[...], preferred_element_type=F32))
        masks.append(mask)
        for r in range(GROUP):
            hd = GROUP * g + r
            heads.append(gates[:, 3 * hd:3 * hd + 1] * o[r * tq:(r + 1) * tq])
    return heads, masks


def _agg_matrix(n_cmp_pad, n_cmp, n_slc, n_slc_pad):
    c0 = np.arange(n_cmp)[:, None] * CMP_STRIDE
    s0 = np.arange(n_slc)[None, :] * SLC_BLOCK
    overlap = np.clip(np.minimum(c0 + CMP_BLOCK, s0 + SLC_BLOCK) - np.maximum(c0, s0), 0, None)
    agg = np.zeros((n_cmp_pad, n_slc_pad), np.float32)
    agg[:n_cmp, :n_slc] = overlap / CMP_BLOCK
    return jnp.asarray(agg, BF16)


def _cmp_attn(q_hm, kc, vc, gates, agg, *, tq, pos0, tiled, n_slc, sel_dtype):
    b, _, t, _ = q_hm.shape
    n = kc.shape[2]
    nsp = agg.shape[1]
    blocks_major = tq % LANES == 0
    if blocks_major:
        agg = agg.T
    ns = CMP_ATTN_SEQS if (t == tq and b % CMP_ATTN_SEQS == 0) else 1
    return pl.pallas_call(
        functools.partial(_cmp_attn_kernel, pos0=pos0, tiled=tiled, n_slc=n_slc, blocks_major=blocks_major),
        grid=(b // ns, t // tq),
        in_specs=[
            pl.BlockSpec((ns, N_HEADS, tq, HEAD_DIM), lambda i, k: (i, 0, k, 0)),
            pl.BlockSpec((ns, N_KV_HEADS, n, HEAD_DIM), lambda i, k: (i, 0, 0, 0)),
            pl.BlockSpec((ns, N_KV_HEADS, n, HEAD_DIM), lambda i, k: (i, 0, 0, 0)),
            pl.BlockSpec((ns, tq, LANES), lambda i, k: (i, k, 0)),
            _const_spec(agg.shape),
        ],
        out_specs=[pl.BlockSpec((ns, tq, NSA_WIDTH), lambda i, k: (i, k, 0)),
                   pl.BlockSpec((ns, tq, N_KV_HEADS * nsp), lambda i, k: (i, k, 0))],
        out_shape=[jax.ShapeDtypeStruct((b, t, NSA_WIDTH), F32),
                   jax.ShapeDtypeStruct((b, t, N_KV_HEADS * nsp), sel_dtype)],
        compiler_params=_params("arbitrary", "arbitrary"),
        name="cmp_attn",
    )(q_hm, kc, vc, gates, agg)


def _expand_matrix(n_keys, n_blk_pad, tk):
    e = np.zeros((n_keys // tk, n_blk_pad, tk), np.float32)
    key = np.arange(n_keys)
    e[key // tk, key // SLC_BLOCK, key % tk] = 1.0
    return jnp.asarray(e, BF16)


def _col_blocks(x, op):
    out = x[:, :LANES]
    for c in range(1, x.shape[1] // LANES):
        out = op(out, x[:, c * LANES:(c + 1) * LANES])
    return out


def _prompt_attn_kernel(q_ref, kvs_ref, kvw_ref, sel_ref, e_ref, oc_ref, gate_ref, o_ref,
                        qx_ref, qg_ref, s_ref, m_ref, l_ref, acc_ref):
    tq = q_ref.shape[1]
    tk = kvs_ref.shape[-1]
    nsp = sel_ref.shape[1] // N_KV_HEADS
    rows = GROUP * tq
    spare = s_ref.shape[1] - 1
    qt = pl.program_id(1)
    n_win = WINDOW // tk
    row = lax.broadcasted_iota(jnp.int32, (tq, tk), 0)
    col = lax.broadcasted_iota(jnp.int32, (tq, tk), 1)
    causal = jnp.where(col <= row, 0.0, NEG_INF)
    win_lo = jnp.where(col >= row, 0.0, NEG_INF)
    gates = gate_ref[...]
    oc = oc_ref[...]
    k_pad = jnp.zeros((qx_ref.shape[2] - nsp - HEAD_DIM, tk), BF16)
    heads = range(N_KV_HEADS)

    def scores_slc(g, kt):
        rhs = jnp.concatenate([e_ref[kt], kvs_ref[g, kt], k_pad], axis=0)
        return jnp.dot(qx_ref[g], rhs, preferred_element_type=F32)

    def scores_win(g, kt):
        return jnp.dot(qg_ref[g], kvw_ref[g, kt], preferred_element_type=F32)

    def find_max(scores, g, kt, slot, bias):
        s = scores(g, kt)
        if bias is not None:
            s = (s.reshape(GROUP, tq, tk) + bias[None]).reshape(rows, tk)
        s_ref[g, slot] = s
        m_ref[g] = jnp.maximum(m_ref[g], _col_blocks(s, jnp.maximum))

    def accumulate(v_ref, g, kt, slot):
        p = jnp.exp2(s_ref[g, slot] - jnp.tile(m_ref[g], (1, tk // LANES)))
        l_ref[g] += _col_blocks(p, jnp.add)
        acc_ref[g] += lax.dot_general(p.astype(BF16), v_ref[N_KV_HEADS + g, kt], _NT, preferred_element_type=F32)

    def maybe(kt, bias):
        exists = kt >= 0
        gate = jnp.where(exists, 0.0, NEG_INF)
        return jnp.maximum(kt, 0), jnp.where(exists, kt, spare), (gate if bias is None else bias + gate)

    def softmax_v(scores, v_ref, loop_pairs, tail):
        def both_passes(fn_pair, fn_tail):
            def body(i, carry):
                for g in heads:
                    fn_pair(g, 2 * i)
                    fn_pair(g, 2 * i + 1)
                return carry
            lax.fori_loop(0, loop_pairs, body, 0)
            for g in heads:
                for kt, slot, bias in tail:
                    fn_tail(g, kt, slot, bias)

        m_ref[...] = jnp.full(m_ref.shape, NEG_INF, F32)
        both_passes(lambda g, kt: find_max(scores, g, kt, kt, None),
                    lambda g, kt, slot, bias: find_max(scores, g, kt, slot, bias))
        for g in heads:
            m_ref[g] = jnp.broadcast_to(jnp.max(m_ref[g], axis=-1, keepdims=True), (rows, LANES))
        l_ref[...] = jnp.zeros(l_ref.shape, F32)
        acc_ref[...] = jnp.zeros(acc_ref.shape, F32)
        both_passes(lambda g, kt: accumulate(v_ref, g, kt, kt),
                    lambda g, kt, slot, bias: accumulate(v_ref, g, kt, slot))
        return [acc_ref[g] / jnp.sum(l_ref[g], axis=-1, keepdims=True) for g in heads]

    for g in heads:
        sel_g = sel_ref[:, g * nsp:(g + 1) * nsp]
        qx_ref[g, :, nsp + HEAD_DIM:] = jnp.zeros((rows, qx_ref.shape[2] - nsp - HEAD_DIM), BF16)
        for r in range(GROUP):
            qx_ref[g, r * tq:(r + 1) * tq, :nsp] = sel_g
            qx_ref[g, r * tq:(r + 1) * tq, nsp:nsp + HEAD_DIM] = q_ref[GROUP * g + r]
            qg_ref[g, r * tq:(r + 1) * tq, :] = q_ref[GROUP * g + r]
    odd_kt = jnp.where(qt % 2 == 1, qt - 1, -1)
    o_s = softmax_v(scores_slc, kvs_ref, qt // 2, [maybe(odd_kt, None), (qt, qt, causal)])
    tail = [maybe(qt - n_win, win_lo)] + [maybe(qt - d, None) for d in range(n_win - 1, 0, -1)]
    o_w = softmax_v(scores_win, kvw_ref, 0, tail + [(qt, qt, causal)])
    for g in heads:
        for r in range(GROUP):
            hd = GROUP * g + r
            cols = slice(hd * HEAD_DIM, (hd + 1) * HEAD_DIM)
            head = slice(r * tq, (r + 1) * tq)
            o_ref[:, cols] = (oc[:, cols] + gates[:, 3 * hd + 1:3 * hd + 2] * o_s[g][head]
                              + gates[:, 3 * hd + 2:3 * hd + 3] * o_w[g][head]).astype(o_ref.dtype)


def _prompt_attn(q_hm, kvs_b, kvw_b, sel, expand, oc, gates):
    b, _, t, _ = q_hm.shape
    tq = kvs_b.shape[-1]
    rows = GROUP * tq
    nsp = sel.shape[2] // N_KV_HEADS
    k_ext = LANES * (-(-(nsp + HEAD_DIM) // LANES))
    tile3 = lambda i, k: (i, k, 0)
    kv_spec = pl.BlockSpec((None,) + kvs_b.shape[1:], lambda i, k: (i, 0, 0, 0, 0))
    return pl.pallas_call(
        _prompt_attn_kernel,
        grid=(b, t // tq),
        in_specs=[
            pl.BlockSpec((None, N_HEADS, tq, HEAD_DIM), lambda i, k: (i, 0, k, 0)),
            kv_spec, kv_spec,
            pl.BlockSpec((None, tq, sel.shape[2]), tile3),
            _const_spec(expand.shape),
            pl.BlockSpec((None, tq, NSA_WIDTH), tile3),
            pl.BlockSpec((None, tq, LANES), tile3),
        ],
        out_specs=pl.BlockSpec((None, tq, NSA_WIDTH), tile3),
        out_shape=jax.ShapeDtypeStruct((b, t, NSA_WIDTH), BF16),
        scratch_shapes=[pltpu.VMEM((N_KV_HEADS, rows, k_ext), BF16), pltpu.VMEM((N_KV_HEADS, rows, HEAD_DIM), BF16),
                        pltpu.VMEM((N_KV_HEADS, t // tq + 1, rows, tq), F32),
                        pltpu.VMEM((N_KV_HEADS, rows, LANES), F32), pltpu.VMEM((N_KV_HEADS, rows, LANES), F32),
                        pltpu.VMEM((N_KV_HEADS, rows, HEAD_DIM), F32)],
        compiler_params=_params("arbitrary", "arbitrary"),
        name="prompt_attn",
    )(q_hm, kvs_b, kvw_b, sel, expand, oc, gates)


Q_PAD = 8


def _group_q(q_ref, g):
    return jnp.concatenate([q_ref[GROUP * g + r] for r in range(GROUP)], axis=0).astype(BF16)


def _two_piece_attention(qg, k_old, v_old, bias_old, k_new, v_new, bias_new):
    def scores(k_t, bias):
        s = jnp.dot(qg, k_t.astype(BF16), preferred_element_type=F32)
        nk = s.shape[1]
        return (s.reshape(GROUP, Q_PAD, nk) + bias[None]).reshape(GROUP * Q_PAD, nk)
    s_old = scores(k_old, bias_old)
    s_new = scores(k_new, bias_new)
    m = jnp.maximum(jnp.max(s_old, axis=-1, keepdims=True), jnp.max(s_new, axis=-1, keepdims=True))
    p_old = jnp.exp(s_old - m)
    p_new = jnp.exp(s_new - m)
    den = jnp.sum(p_old, axis=-1, keepdims=True) + jnp.sum(p_new, axis=-1, keepdims=True)
    pv = (lax.dot_general(p_old.astype(BF16), v_old.astype(BF16), _NT, preferred_element_type=F32)
          + lax.dot_general(p_new.astype(BF16), v_new.astype(BF16), _NT, preferred_element_type=F32))
    return pv / den


def _add_gated(prev_ref, gate_ref, o_ref, per_g, branch):
    prev, gates = prev_ref[...], gate_ref[...]
    for g in range(N_KV_HEADS):
        for r in range(GROUP):
            hd = GROUP * g + r
            cols = slice(hd * HEAD_DIM, (hd + 1) * HEAD_DIM)
            o_ref[:, cols] = (prev[:, cols] + gates[:, 3 * hd + branch:3 * hd + branch + 1]
                              * per_g[g][r * Q_PAD:(r + 1) * Q_PAD])


def _new_rows_bias(dec_seq):
    q = lax.broadcasted_iota(jnp.int32, (Q_PAD, LANES), 0)
    i = lax.broadcasted_iota(jnp.int32, (Q_PAD, LANES), 1) - (LANES - dec_seq)
    return jnp.where((i >= 0) & (i <= q), 0.0, NEG_INF)


def _sample_slc_kernel(pt_ref, cache_hbm, q_ref, sel_ref, new_ref, e_ref, prev_ref, gate_ref, o_ref,
                       buf, sem, *, dec_seq):
    n_pages = pt_ref.shape[1]

    def copy(page, slot, p):
        return pltpu.make_async_copy(cache_hbm.at[page],
                                     buf.at[slot, :, :, pl.ds(p * PAGE_SIZE, PAGE_SIZE)], sem.at[slot])

    def start_fetch(seq, slot):
        for p in range(n_pages):
            copy(pt_ref[seq, p], slot, p).start()

    def wait_fetch(slot):
        for p in range(n_pages):
            copy(0, slot, p).wait()

    slot = _paged_prologue(start_fetch, wait_fetch)
    nsp = sel_ref.shape[1] // N_KV_HEADS
    n_past_blk = n_pages * (PAGE_SIZE // SLC_BLOCK)
    new_bias = _new_rows_bias(dec_seq)
    n_exp = e_ref.shape[0]
    sel = [sel_ref[:, g * nsp:(g + 1) * nsp] for g in range(N_KV_HEADS)]
    sel_past = jnp.concatenate([s[:, :n_exp] for s in sel], axis=0).astype(BF16)
    bias_past = jnp.dot(sel_past, e_ref[...], preferred_element_type=F32)
    per_g = []
    for g in range(N_KV_HEADS):
        bias_old = bias_past[g * Q_PAD:(g + 1) * Q_PAD]
        bias_new = new_bias + sel[g][:, n_past_blk:n_past_blk + 1]
        per_g.append(_two_piece_attention(_group_q(q_ref, g), buf[slot, g], buf[slot, N_KV_HEADS + g], bias_old,
                                          new_ref[g], new_ref[N_KV_HEADS + g], bias_new))
    _add_gated(prev_ref, gate_ref, o_ref, per_g, 1)


def _sample_slc(page_table, cache, q_s, sel, new_t, expand, prev, gates, dec_seq):
    db, n_pages = page_table.shape
    past = n_pages * PAGE_SIZE
    seq3 = lambda i, pt: (i, 0, 0)
    seq4 = lambda i, pt: (i, 0, 0, 0)
    return pl.pallas_call(
        functools.partial(_sample_slc_kernel, dec_seq=dec_seq),
        grid_spec=pltpu.PrefetchScalarGridSpec(
            num_scalar_prefetch=1,
            grid=(db,),
            in_specs=[pl.BlockSpec(memory_space=pl.ANY),
                      pl.BlockSpec((None, N_HEADS, Q_PAD, HEAD_DIM), seq4),
                      pl.BlockSpec((None, Q_PAD, sel.shape[2]), seq3),
                      pl.BlockSpec((None, N_KV, HEAD_DIM, LANES), seq4),
                      pl.BlockSpec(expand.shape, lambda i, pt: (0, 0)),
                      pl.BlockSpec((None, Q_PAD, NSA_WIDTH), seq3),
                      pl.BlockSpec((None, Q_PAD, LANES), seq3)],
            out_specs=pl.BlockSpec((None, Q_PAD, NSA_WIDTH), seq3),
            scratch_shapes=[pltpu.VMEM((2, N_KV, HEAD_DIM, past), F32), pltpu.SemaphoreType.DMA((2,))],
        ),
        out_shape=jax.ShapeDtypeStruct((db, Q_PAD, NSA_WIDTH), F32),
        compiler_params=_params("arbitrary"),
        name="sample_slc",
    )(page_table, cache, q_s, sel, new_t, expand, prev, gates)


WIN_SEQS = 4


def _sample_win_kernel(q_ref, st_ref, new_ref, prev_ref, gate_ref, o_ref, st_out_ref, *, dec_seq):
    wb = st_ref.shape[-1]
    q = lax.broadcasted_iota(jnp.int32, (Q_PAD, wb), 0)
    i = lax.broadcasted_iota(jnp.int32, (Q_PAD, wb), 1)
    bias_old = jnp.where(wb + q - i <= WINDOW, 0.0, NEG_INF)
    new_bias = _new_rows_bias(dec_seq)
    lane = lax.broadcasted_iota(jnp.int32, (HEAD_DIM, wb), 1)
    per_seq = [[_two_piece_attention(_group_q(q_ref.at[b], g), st_ref[b, g], st_ref[b, N_KV_HEADS + g], bias_old,
                                     new_ref[b, g], new_ref[b, N_KV_HEADS + g], new_bias)
                for g in range(N_KV_HEADS)] for b in range(q_ref.shape[0])]
    for b, per_g in enumerate(per_seq):
        _add_gated(prev_ref.at[b], gate_ref.at[b], o_ref.at[b], per_g, 2)
        for j in range(N_KV):
            shifted = pltpu.roll(st_ref[b, j], wb - dec_seq, axis=1)
            st_out_ref[b, j] = jnp.where(lane >= wb - dec_seq, jnp.tile(new_ref[b, j], (1, wb // LANES)), shifted)


def _sample_win(q_s, st_win, new_t, prev, gates, dec_seq):
    db, _, _, wb = st_win.shape
    ns = WIN_SEQS if db % WIN_SEQS == 0 else 1
    seq3 = lambda i: (i, 0, 0)
    seq4 = lambda i: (i, 0, 0, 0)
    return pl.pallas_call(
        functools.partial(_sample_win_kernel, dec_seq=dec_seq),
        grid=(db // ns,),
        in_specs=[pl.BlockSpec((ns, N_HEADS, Q_PAD, HEAD_DIM), seq4),
                  pl.BlockSpec((ns, N_KV, HEAD_DIM, wb), seq4),
                  pl.BlockSpec((ns, N_KV, HEAD_DIM, LANES), seq4),
                  pl.BlockSpec((ns, Q_PAD, NSA_WIDTH), seq3),
                  pl.BlockSpec((ns, Q_PAD, LANES), seq3)],
        out_specs=[pl.BlockSpec((ns, Q_PAD, NSA_WIDTH), seq3), pl.BlockSpec((ns, N_KV, HEAD_DIM, wb), seq4)],
        out_shape=[jax.ShapeDtypeStruct((db, Q_PAD, NSA_WIDTH), F32),
                   jax.ShapeDtypeStruct((db, N_KV, HEAD_DIM, wb), F32)],
        compiler_params=_params("arbitrary"),
        name="sample_win",
    )(q_s, st_win, new_t, prev, gates)


def _pool_project(d_groups, wp_ref, scale_ref, o_ref):
    for gi, d in enumerate(d_groups):
        cols = slice(gi * POOL_GROUP_WIDTH, (gi + 1) * POOL_GROUP_WIDTH)
        y = jnp.dot(d.astype(BF16), wp_ref[gi], preferred_element_type=F32)
        o_ref[:, cols] = (y * scale_ref[:, cols]).astype(o_ref.dtype)


def _pool_tile(ext_ref, t0, wp_ref, scale_ref, o_ref):
    tm = o_ref.shape[0]
    pos = t0 + lax.broadcasted_iota(jnp.int32, (tm, POOL_GROUP_WIDTH), 0)
    d_groups = []
    for gi, w in enumerate(POOL_WINDOWS):
        cols = slice(gi * POOL_GROUP_WIDTH, (gi + 1) * POOL_GROUP_WIDTH)
        e = ext_ref[:, cols]
        acc = e
        span = 1
        while span < w:
            acc = acc + pltpu.roll(acc, span, axis=0)
            span *= 2
        cnt = jnp.minimum(w, pos + 1).astype(F32)
        d_groups.append(acc[POOL_HALO:] / cnt - e[POOL_HALO:])
    _pool_project(d_groups, wp_ref, scale_ref, o_ref)


def _pool_sample_kernel(ext_ref, wp_ref, scale_ref, o_ref, *, past_len, dec_seq):
    db = ext_ref.shape[1]
    for q in range(dec_seq):
        d_groups = []
        for gi, w in enumerate(POOL_WINDOWS):
            cols = slice(gi * POOL_GROUP_WIDTH, (gi + 1) * POOL_GROUP_WIDTH)
            row = POOL_HALO + q
            acc = ext_ref[row, :, cols]
            for i in range(1, w):
                acc = acc + ext_ref[row - i, :, cols]
            cnt = float(min(w, past_len + q + 1))
            d_groups.append(acc / cnt - ext_ref[row, :, cols])
        _pool_project(d_groups, wp_ref, scale_ref, o_ref.at[pl.ds(q * db, db)])


def _pool_sample(ext, wp, scale, past_len, dec_seq):
    rows, db, pw = ext.shape
    return pl.pallas_call(
        functools.partial(_pool_sample_kernel, past_len=past_len, dec_seq=dec_seq),
        grid=(1,),
        in_specs=[_const_spec(ext.shape), _const_spec(wp.shape), _const_spec(scale.shape)],
        out_specs=_const_spec((dec_seq * db, pw)),
        out_shape=jax.ShapeDtypeStruct((dec_seq * db, pw), BF16),
        compiler_params=_params("arbitrary"),
        name="pool_sample",
    )(ext, wp, scale)


def _cmp_weights(w1, w2, pe):
    n_slots = CMP_BLOCK // CMP_STRIDE
    w1s = w1.reshape(n_slots, CMP_FLAT, CMP_HIDDEN)
    w1cat = jnp.concatenate([w1s[h] for h in range(n_slots)], axis=1).astype(BF16)
    pe8 = jnp.pad(pe.reshape(n_slots, CMP_FLAT), ((0, SUBLANES - n_slots), (0, 0)))
    return w1cat, w2.astype(BF16), pe8


def _pad_axis(x, axis, size, front=False):
    pad = [(0, 0)] * x.ndim
    extra = size - x.shape[axis]
    pad[axis] = (extra, 0) if front else (0, extra)
    return jnp.pad(x, pad)


def _rows_to_state(x_fm):
    b, _, _, t = x_fm.shape
    return x_fm.reshape(b, 2, N_KV_HEADS, HEAD_DIM, t).transpose(0, 4, 1, 2, 3)


def _state_to_fm(x):
    b, r = x.shape[:2]
    return x.transpose(0, 2, 3, 4, 1).reshape(b, N_KV, HEAD_DIM, r)


def kernel(x_prompt, x_sample, cache_kv_cmp, cache_kv_slc, page_table, state_kv_win, state_pool, n_ffn1, w_ffn1_gate, w_ffn1_up, w_ffn1_down, n_mix, w_in, w_cmp_k1, w_cmp_k2, pe_cmp_k, w_cmp_v1, w_cmp_v2, pe_cmp_v, w_pool, pool_scale, w_out, n_ffn2, w_ffn2_gate, w_ffn2_up, w_ffn2_down, n_final):
    b, t, d = x_prompt.shape
    db, ds, _ = x_sample.shape
    depth = w_in.shape[0]
    n_pages = page_table.shape[1]
    past = n_pages * PAGE_SIZE
    wb = state_kv_win.shape[2]
    assert t % ATTN_TQ == 0 and t % INPROJ_TM == 0 and WINDOW % ATTN_TQ == 0 and t >= WINDOW
    assert ds <= Q_PAD and ds <= CMP_STRIDE and wb == WINDOW and past % SLC_BLOCK == 0
    assert (db * ds) % SUBLANES == 0 and n_pages % CMP_PAGE_GROUP == 0 and (t // PAGE_SIZE) % CMP_PAGE_GROUP == 0

    xp = x_prompt.reshape(b * t, d)
    xs = x_sample.reshape(db * ds, d)
    pos_p = jnp.arange(t, dtype=jnp.int32)
    pos_s = past + jnp.arange(ds, dtype=jnp.int32)
    tab_p = _rope_tables(pos_p)
    tab_s = tuple(jnp.tile(a, (db, 1)) for a in _rope_tables(pos_s)[:3]) + tuple(
        jnp.tile(a, (1, db)) for a in _rope_tables(pos_s)[3:])

    n_chunk_p = t // CMP_STRIDE
    n_slc_p = -(-t // SLC_BLOCK)
    agg_p = _agg_matrix(n_chunk_p, n_chunk_p - 1, n_slc_p, LANES * (-(-n_slc_p // LANES)))
    expand_p = _expand_matrix(t, agg_p.shape[1], ATTN_TQ)
    n_chunk_s = past // CMP_STRIDE
    n_slc_s = -(-(past + ds) // SLC_BLOCK)
    agg_s = _agg_matrix(n_chunk_s, n_chunk_s, n_slc_s, LANES * (-(-n_slc_s // LANES)))
    expand_s = _expand_matrix(past, LANES * (-(-(past // SLC_BLOCK) // LANES)), past)[0]

    st_p = ([], [], [], [])
    st_s = ([], [], [], [])
    for l in range(depth):
        last = l == depth - 1
        ffn1 = (n_ffn1[l], w_ffn1_gate[l].astype(BF16), w_ffn1_up[l].astype(BF16), w_ffn1_down[l].astype(BF16))
        ffn2 = (n_ffn2[l], w_ffn2_gate[l].astype(BF16), w_ffn2_up[l].astype(BF16), w_ffn2_down[l].astype(BF16))
        w = w_in[l]
        o_kv, o_gate, o_pool = NSA_WIDTH, NSA_WIDTH + 3 * KV_WIDTH, NSA_WIDTH + 3 * KV_WIDTH + N_GATES
        wq = w[:, :o_kv].astype(BF16)
        wkv_t = w[:, o_kv:o_gate].T.astype(BF16)
        wgt = _pad_axis(w[:, o_gate:o_pool], 1, LANES).astype(BF16)
        wu = w[:, o_pool:].astype(BF16)
        proj = (n_mix[l], wq, wkv_t, wgt, wu)
        wk1, wk2, pek = _cmp_weights(w_cmp_k1[l], w_cmp_k2[l], pe_cmp_k[l])
        wv1, wv2, pev = _cmp_weights(w_cmp_v1[l], w_cmp_v2[l], pe_cmp_v[l])
        pe = jnp.stack([pek, pev])
        wp = w_pool[l].astype(BF16)
        scale = pool_scale[l].reshape(1, POOL_WIDTH)
        wo_nsa = w_out[l][:NSA_WIDTH].astype(BF16)
        wo_pool = w_out[l][NSA_WIDTH:].astype(BF16)

        xp = _ffn(xp, *ffn1)
        q_hm, q2_hm, kvc, kvs, kvw, kvs_b, kvw_b, gates, pool_out, u_tail = _inproj(
            xp, *proj, tab_p, BF16, pool=(wp, scale))
        kc, vc = _compress_prompt(kvc, pe, wk1, wv1, wk2, wv2)
        gates3 = gates.reshape(b, t, LANES)
        oc, sel = _cmp_attn(q_hm, kc, vc, gates3, agg_p, tq=ATTN_TQ, pos0=0, tiled=True,
                            n_slc=n_slc_p, sel_dtype=BF16)
        o_mix = _prompt_attn(q2_hm, kvs_b, kvw_b, sel, expand_p, oc, gates3)
        mix = (o_mix.reshape(b * t, NSA_WIDTH), pool_out, wo_nsa, wo_pool)
        xp = _ffn(xp, *ffn2, g_final=n_final if last else None, mix=mix)
        st_p[0].append(_rows_to_state(kvc))
        st_p[1].append(_rows_to_state(kvs))
        st_p[2].append(_rows_to_state(kvw[..., t - min(WINDOW, t):]))
        st_p[3].append(u_tail[:, POOL_HALO - POOL_STATE:])

        xs = _ffn(xs, *ffn1)
        q, _, kvc, kvs, kvw, _, _, gates, u = _inproj(xs, *proj, tab_s, F32)
        per_seq = lambda a: a.reshape(N_KV, HEAD_DIM, db, ds).transpose(2, 0, 1, 3)
        kvc_n, kvs_n, kvw_n = per_seq(kvc), per_seq(kvs), per_seq(kvw)
        ynew = _pad_axis(kvc_n.transpose(0, 1, 3, 2), 2, CMP_STRIDE).reshape(db, N_KV, CMP_FLAT)
        cache_c = _state_to_fm(cache_kv_cmp[l])
        kc, vc = _compress_sample(page_table, cache_c, _pad_axis(ynew, 1, SUBLANES), pe, wk1, wv1, wk2, wv2)
        q_s = _pad_axis(q.reshape(N_HEADS, db, ds, HEAD_DIM).transpose(1, 0, 2, 3), 2, Q_PAD)
        gates_s = _pad_axis(gates.reshape(db, ds, LANES), 1, Q_PAD)
        oc, p_slc = _cmp_attn(q_s, kc, vc, gates_s, agg_s, tq=Q_PAD, pos0=past, tiled=False,
                              n_slc=n_slc_s, sel_dtype=F32)
        sel = _select_all(p_slc, pos0=past, n_slc=n_slc_s)
        o_cs =_sample_slc(page_table, _state_to_fm(cache_kv_slc[l]), q_s, sel,
                           _pad_axis(kvs_n, 3, LANES, front=True), expand_s, oc, gates_s, ds)
        o_mix, st_win_new = _sample_win(q_s, _state_to_fm(state_kv_win[l]),
                                        _pad_axis(kvw_n, 3, LANES, front=True), o_cs, gates_s, ds)
        u3 = u.reshape(db, ds, POOL_WIDTH)
        ext = jnp.concatenate([jnp.zeros((db, POOL_HALO - POOL_STATE, POOL_WIDTH), F32), state_pool[l], u3], axis=1)
        pool_out = _pool_sample(ext.transpose(1, 0, 2), wp, scale, past, ds)
        pool_out = pool_out.reshape(ds, db, POOL_WIDTH).transpose(1, 0, 2).reshape(db * ds, POOL_WIDTH)
        mix = (o_mix[:, :ds].reshape(db * ds, NSA_WIDTH), pool_out, wo_nsa, wo_pool)
        xs = _ffn(xs, *ffn2, g_final=n_final if last else None, mix=mix)
        st_s[0].append(_rows_to_state(kvc_n))
        st_s[1].append(_rows_to_state(kvs_n))
        st_s[2].append(_rows_to_state(st_win_new))
        st_s[3].append(jnp.concatenate([state_pool[l], u3], axis=1)[:, ds:])

    return (xp.reshape(b, t, d), xs.reshape(db, ds, d),
            jnp.stack(st_p[0]), jnp.stack(st_p[1]), jnp.stack(st_p[2]), jnp.stack(st_p[3]),
            jnp.stack(st_s[0]), jnp.stack(st_s[1]), jnp.stack(st_s[2]), jnp.stack(st_s[3]))
```

```python
import functools

import numpy as np
import jax
import jax.numpy as jnp
from jax import lax
from jax.experimental import pallas as pl
from jax.experimental.pallas import tpu as pltpu

F32 = jnp.float32
BF16 = jnp.bfloat16

HEAD_DIM = 64
N_HEADS = 8
N_KV_HEADS = 2
GROUP = N_HEADS // N_KV_HEADS
NSA_WIDTH = N_HEADS * HEAD_DIM
N_KV = 2 * N_KV_HEADS
KV_WIDTH = N_KV * HEAD_DIM
ROPE_DIM = HEAD_DIM // 4
ROPE_HALF = ROPE_DIM // 2
ROPE_THETA = 500000.0
CMP_BLOCK = 32
CMP_STRIDE = 16
CMP_HIDDEN = 256
CMP_FLAT = CMP_STRIDE * HEAD_DIM
SLC_BLOCK = 64
TOPK_BLOCKS = 16
WINDOW = 512
PAGE_SIZE = 128
POOL_WINDOWS = (2, 4, 8, 16)
POOL_GROUP_WIDTH = 128
POOL_WIDTH = POOL_GROUP_WIDTH * len(POOL_WINDOWS)
POOL_STATE = max(POOL_WINDOWS) - 1
POOL_HALO = 16
RMS_EPS = 1e-6
FORCE_SCORE = 1e4
NEG_INF = -1e30
ATTN_SCALE = HEAD_DIM ** -0.5
LOG2_E = float(np.log2(np.e))
N_GATES = 3 * N_HEADS

LANES = 128
SUBLANES = 8
VMEM_LIMIT_BYTES = 56 * 1024 * 1024

_NT = (((1,), (1,)), ((), ()))


def _params(*sem):
    return pltpu.CompilerParams(dimension_semantics=sem, vmem_limit_bytes=VMEM_LIMIT_BYTES)


def _rmsnorm(x, g):
    return x * lax.rsqrt(jnp.mean(x * x, axis=-1, keepdims=True) + RMS_EPS) * g


def _const_spec(shape):
    n = len(shape)
    return pl.BlockSpec(shape, lambda *_: (0,) * n)


FFN_TM = 512
FFN_CHUNK = 256


def _ffn_kernel(x_ref, g_ref, wg_ref, wu_ref, wd_ref, gf_ref, *rest, final_norm, mix):
    if mix:
        mo_ref, mp_ref, wo_ref, wp_ref, o_ref, act_ref = rest
        x = (x_ref[...] + jnp.dot(mo_ref[...].astype(BF16), wo_ref[...], preferred_element_type=F32)
             + jnp.dot(mp_ref[...], wp_ref[...], preferred_element_type=F32))
    else:
        o_ref, act_ref = rest
        x = x_ref[...]
    h = _rmsnorm(x, g_ref[...]).astype(BF16)
    d_ff = wg_ref.shape[1]
    for c in range(d_ff // FFN_CHUNK):
        sl = slice(c * FFN_CHUNK, (c + 1) * FFN_CHUNK)
        a = jnp.dot(h, wg_ref[:, sl], preferred_element_type=F32)
        u = jnp.dot(h, wu_ref[:, sl], preferred_element_type=F32)
        act_ref[:, sl] = (a * jax.nn.sigmoid(a) * u).astype(BF16)
    y = x + 0.5 * jnp.dot(act_ref[...], wd_ref[...], preferred_element_type=F32)
    if final_norm:
        y = _rmsnorm(y, gf_ref[...])
    o_ref[...] = y


def _ffn(x, g, wg, wu, wd, g_final=None, mix=None):
    m, d = x.shape
    d_ff = wg.shape[1]
    tm = min(FFN_TM, m)
    final_norm = g_final is not None
    gf = g_final if final_norm else g
    row = lambda i: (i, 0)
    once = lambda shape: pl.BlockSpec(shape, lambda i: (0, 0), pipeline_mode=pl.Buffered(1))
    in_specs = [pl.BlockSpec((tm, d), row), _const_spec((1, d)),
                once((d, d_ff)), once((d, d_ff)), once((d_ff, d)), _const_spec((1, d))]
    args = [x, g.reshape(1, d), wg, wu, wd, gf.reshape(1, d)]
    if mix is not None:
        mo, mp, wo, wp = mix
        in_specs += [pl.BlockSpec((tm, mo.shape[1]), row), pl.BlockSpec((tm, mp.shape[1]), row),
                     once(wo.shape), once(wp.shape)]
        args += [mo, mp, wo, wp]
    return pl.pallas_call(
        functools.partial(_ffn_kernel, final_norm=final_norm, mix=mix is not None),
        grid=(m // tm,),
        in_specs=in_specs,
        out_specs=pl.BlockSpec((tm, d), row),
        out_shape=jax.ShapeDtypeStruct((m, d), F32),
        scratch_shapes=[pltpu.VMEM((tm, d_ff), BF16)],
        compiler_params=_params("arbitrary"),
        name="ffn",
    )(*args)


INPROJ_TM = 512
ATTN_TQ = 256


def _rope_tables(pos):
    inv = jnp.power(ROPE_THETA, -jnp.arange(ROPE_HALF, dtype=F32) / ROPE_HALF)
    ang = pos.astype(F32)[:, None] * inv[None, :]
    cos, sin = jnp.cos(ang), jnp.sin(ang)
    t = pos.shape[0]
    rest = HEAD_DIM - ROPE_DIM
    z_half = jnp.zeros((t, ROPE_HALF), F32)
    z_rest = jnp.zeros((t, rest), F32)
    cos_t = jnp.concatenate([cos, cos, jnp.ones((t, rest), F32)], axis=1)
    sin_a = jnp.concatenate([-sin, z_half, z_rest], axis=1)
    sin_b = jnp.concatenate([z_half, sin, z_rest], axis=1)
    rep = LANES // HEAD_DIM
    return tuple(jnp.tile(a, (1, rep)) for a in (cos_t, sin_a, sin_b)) + (cos.T, sin.T)


def _rope(z, cos_t, sin_a, sin_b):
    outs = []
    for c in range(z.shape[1] // LANES):
        zc = z[:, c * LANES:(c + 1) * LANES]
        outs.append(zc * cos_t + pltpu.roll(zc, LANES - ROPE_HALF, axis=1) * sin_a
                    + pltpu.roll(zc, ROPE_HALF, axis=1) * sin_b)
    return outs[0] if len(outs) == 1 else jnp.concatenate(outs, axis=1)


def _rope_fm(kt, cos, sin):
    x1, x2 = kt[0:ROPE_HALF], kt[ROPE_HALF:ROPE_DIM]
    return jnp.concatenate([x1 * cos - x2 * sin, x2 * cos + x1 * sin, kt[ROPE_DIM:]], axis=0)


def _inproj_kernel(x_ref, g_ref, wq_ref, wkv_t_ref, wgt_ref, wu_ref,
                   cos_ref, sa_ref, sb_ref, cos_fm_ref, sin_fm_ref, *rest, n_tab, pool):
    if pool:
        (wp_ref, scale_ref, q_ref, q2_ref, kvc_ref, kvs_ref, kvw_ref, kvs_b_ref, kvw_b_ref, gate_ref,
         pool_ref, u_tail_ref, carry_ref, ext_ref) = rest
    else:
        q_ref, q2_ref, kvc_ref, kvs_ref, kvw_ref, kvs_b_ref, kvw_b_ref, gate_ref, u_ref = rest
    h = _rmsnorm(x_ref[...], g_ref[...]).astype(BF16)
    tm = h.shape[0]
    tk = kvs_b_ref.shape[-1]
    cos_t, sin_a, sin_b = cos_ref[...], sa_ref[...], sb_ref[...]
    cos_fm, sin_fm = cos_fm_ref[...], sin_fm_ref[...]
    q = _rope(jnp.dot(h, wq_ref[...], preferred_element_type=F32), cos_t, sin_a, sin_b) * ATTN_SCALE
    q2 = q * LOG2_E
    for hd in range(N_HEADS):
        q_ref[hd] = q[:, hd * HEAD_DIM:(hd + 1) * HEAD_DIM].astype(q_ref.dtype)
        q2_ref[hd] = q2[:, hd * HEAD_DIM:(hd + 1) * HEAD_DIM].astype(BF16)
    kv_t = lax.dot_general(wkv_t_ref[...], h, _NT, preferred_element_type=F32)
    for i, (f_ref, b_ref) in enumerate(((kvc_ref, None), (kvs_ref, kvs_b_ref), (kvw_ref, kvw_b_ref))):
        for j in range(N_KV):
            r0 = i * KV_WIDTH + j * HEAD_DIM
            blk = kv_t[r0:r0 + HEAD_DIM]
            if j < N_KV_HEADS:
                blk = _rope_fm(blk, cos_fm, sin_fm)
            f_ref[j] = blk
            if b_ref is not None:
                for c in range(tm // tk):
                    b_ref[j, c] = blk[:, c * tk:(c + 1) * tk].astype(BF16)
    gate_ref[...] = jax.nn.sigmoid(jnp.dot(h, wgt_ref[...], preferred_element_type=F32))
    u = jnp.dot(h, wu_ref[...], preferred_element_type=F32)
    if pool:
        ti = lax.rem(pl.program_id(0), n_tab)
        ext_ref[0:POOL_HALO, :] = jnp.where(ti > 0, carry_ref[...], 0.0)
        ext_ref[POOL_HALO:, :] = u
        carry_ref[...] = u[tm - POOL_HALO:]
        u_tail_ref[...] = u[tm - POOL_HALO:]
        _pool_tile(ext_ref, ti * tm, wp_ref, scale_ref, pool_ref)
    else:
        u_ref[...] = u


def _inproj(x, g, wq, wkv_t, wgt, wu, tables, q_dtype, pool=None):
    m, d = x.shape
    seq = tables[0].shape[0]
    nseq = m // seq
    tm = min(INPROJ_TM, seq)
    tk = min(ATTN_TQ, tm)
    n_tab = seq // tm
    row = lambda i: (i, 0)
    tab = lambda i: (i % n_tab, 0)
    tab_fm = lambda i: (0, i % n_tab)
    hm = lambda i: (i // n_tab, 0, i % n_tab, 0)
    fm = lambda i: (i // n_tab, 0, 0, i % n_tab)
    fmb = lambda i: (i // n_tab, 0, i % n_tab, 0, 0)
    fm_shape = jax.ShapeDtypeStruct((nseq, N_KV, HEAD_DIM, seq), F32)
    fmb_shape = jax.ShapeDtypeStruct((nseq, N_KV, seq // tk, HEAD_DIM, tk), BF16)
    fm_spec = pl.BlockSpec((None, N_KV, HEAD_DIM, tm), fm)
    fmb_spec = pl.BlockSpec((None, N_KV, tm // tk, HEAD_DIM, tk), fmb)
    in_specs = [
        pl.BlockSpec((tm, d), row), _const_spec((1, d)),
        _const_spec(wq.shape), _const_spec(wkv_t.shape), _const_spec(wgt.shape), _const_spec(wu.shape),
        pl.BlockSpec((tm, LANES), tab), pl.BlockSpec((tm, LANES), tab), pl.BlockSpec((tm, LANES), tab),
        pl.BlockSpec((ROPE_HALF, tm), tab_fm), pl.BlockSpec((ROPE_HALF, tm), tab_fm),
    ]
    out_specs = [
        pl.BlockSpec((None, N_HEADS, tm, HEAD_DIM), hm), pl.BlockSpec((None, N_HEADS, tm, HEAD_DIM), hm),
        fm_spec, fm_spec, fm_spec, fmb_spec, fmb_spec,
        pl.BlockSpec((tm, LANES), row),
    ]
    out_shape = [
        jax.ShapeDtypeStruct((nseq, N_HEADS, seq, HEAD_DIM), q_dtype),
        jax.ShapeDtypeStruct((nseq, N_HEADS, seq, HEAD_DIM), BF16),
        fm_shape, fm_shape, fm_shape, fmb_shape, fmb_shape,
        jax.ShapeDtypeStruct((m, LANES), F32),
    ]
    args = [x, g.reshape(1, d), wq, wkv_t, wgt, wu, *tables]
    scratch = []
    if pool is not None:
        in_specs += [_const_spec(pool[0].shape), _const_spec(pool[1].shape)]
        args += list(pool)
        out_specs += [pl.BlockSpec((tm, POOL_WIDTH), row),
                      pl.BlockSpec((None, POOL_HALO, POOL_WIDTH), lambda i: (i // n_tab, 0, 0))]
        out_shape += [jax.ShapeDtypeStruct((m, POOL_WIDTH), BF16),
                      jax.ShapeDtypeStruct((nseq, POOL_HALO, POOL_WIDTH), F32)]
        scratch = [pltpu.VMEM((POOL_HALO, POOL_WIDTH), F32), pltpu.VMEM((POOL_HALO + tm, POOL_WIDTH), F32)]
    else:
        out_specs += [pl.BlockSpec((tm, POOL_WIDTH), row)]
        out_shape += [jax.ShapeDtypeStruct((m, POOL_WIDTH), F32)]
    return pl.pallas_call(
        functools.partial(_inproj_kernel, n_tab=n_tab, pool=pool is not None),
        grid=(m // tm,),
        in_specs=in_specs,
        out_specs=out_specs,
        out_shape=out_shape,
        scratch_shapes=scratch,
        compiler_params=_params("arbitrary"),
        name="inproj",
    )(*args)


def _gelu_tanh(x):
    return 0.5 * x * (1.0 + jnp.tanh(np.sqrt(2.0 / np.pi).astype(np.float32) * (x + 0.044715 * (x * x * x))))


def _pick_row(blk, r):
    row8 = lax.broadcasted_iota(jnp.int32, (SUBLANES, 1), 0)
    return jnp.sum(jnp.where(row8 == r, blk, 0.0), axis=0, keepdims=True)


def _compress_finish(kv, n, new8, pe_ref, w1_ref, w2_ref, out_ref, y_ref, has_new):
    y_ref[pl.ds(2 * n, 2 * SUBLANES), :] = jnp.concatenate([new8, pe_ref[kv]], axis=0).astype(BF16)
    p = jnp.dot(y_ref[...], w1_ref[...], preferred_element_type=F32)
    p_new, p_pe = p[2 * n:2 * n + SUBLANES], p[2 * n + SUBLANES:2 * n + 2 * SUBLANES]
    bias = _pick_row(p_pe[:, :CMP_HIDDEN], 0) + _pick_row(p_pe[:, CMP_HIDDEN:], 1)
    last = lax.broadcasted_iota(jnp.int32, (n, CMP_HIDDEN), 0) == n - 1
    for g in range(N_KV_HEADS):
        slot0 = p[g * n:(g + 1) * n, :CMP_HIDDEN]
        slot1 = pltpu.roll(p[g * n:(g + 1) * n, CMP_HIDDEN:], n - 1, axis=0)
        if has_new:
            slot1 = jnp.where(last, _pick_row(p_new[:, CMP_HIDDEN:], g), slot1)
        hid = _gelu_tanh(slot0 + slot1 + bias).astype(BF16)
        out_ref[g] = jnp.dot(hid, w2_ref[...], preferred_element_type=F32).astype(BF16)


CMP_PAGE_GROUP = 8


def _dechunk_perm():
    perm = np.zeros((N_KV_HEADS * PAGE_SIZE, 2 * LANES), np.float32)
    cpp = PAGE_SIZE // CMP_STRIDE
    for g in range(N_KV_HEADS):
        for c in range(cpp):
            for j in range(CMP_STRIDE // 2):
                for par in range(2):
                    perm[g * PAGE_SIZE + CMP_STRIDE * c + 2 * j + par, par * LANES + g * HEAD_DIM + j * cpp + c] = 1.0
    return jnp.asarray(perm, BF16)


def _regroup_pages(load_tile, n_pages, n, perm_ref, y_ref):
    pg = CMP_PAGE_GROUP
    cpp = PAGE_SIZE // CMP_STRIDE
    for grp in range(n_pages // pg):
        lhs = jnp.concatenate(
            [jnp.concatenate([load_tile(grp * pg + q, g) for g in range(N_KV_HEADS)], axis=1)
             for q in range(pg)], axis=0).astype(BF16)
        out = jnp.dot(lhs, perm_ref[...], preferred_element_type=F32)
        x = jnp.concatenate(
            [jnp.concatenate([out[q * HEAD_DIM:(q + 1) * HEAD_DIM, :LANES],
                              out[q * HEAD_DIM:(q + 1) * HEAD_DIM, LANES:]], axis=0) for q in range(pg)], axis=1)
        xt = x.T
        r0 = grp * (pg * cpp)
        for g in range(N_KV_HEADS):
            for j in range(CMP_STRIDE // 2):
                piece = jnp.concatenate(
                    [xt[q * LANES + g * HEAD_DIM + j * cpp:q * LANES + g * HEAD_DIM + (j + 1) * cpp]
                     for q in range(pg)], axis=0)
                y_ref[g * n + r0:g * n + r0 + pg * cpp, j * LANES:(j + 1) * LANES] = piece.astype(BF16)


def _compress_prompt_kernel(x_ref, pe_ref, perm_ref, wk1_ref, wv1_ref, wk2_ref, wv2_ref, kc_ref, vc_ref, y_ref):
    t = x_ref.shape[-1]
    n = t // CMP_STRIDE
    for kv, (w1_ref, w2_ref, out_ref) in enumerate(((wk1_ref, wk2_ref, kc_ref), (wv1_ref, wv2_ref, vc_ref))):
        _regroup_pages(lambda page, g: x_ref[N_KV_HEADS * kv + g, :, page * PAGE_SIZE:(page + 1) * PAGE_SIZE],
                       t // PAGE_SIZE, n, perm_ref, y_ref.at[kv])
        _compress_finish(kv, n, jnp.zeros((SUBLANES, CMP_FLAT), F32), pe_ref, w1_ref, w2_ref, out_ref,
                         y_ref.at[kv], False)


def _compress_prompt(x, pe, wk1, wv1, wk2, wv2):
    b, _, _, t = x.shape
    n = t // CMP_STRIDE
    perm = _dechunk_perm()
    out = jax.ShapeDtypeStruct((b, N_KV_HEADS, n, HEAD_DIM), BF16)
    out_spec = pl.BlockSpec((None, N_KV_HEADS, n, HEAD_DIM), lambda i: (i, 0, 0, 0))
    return pl.pallas_call(
        _compress_prompt_kernel,
        grid=(b,),
        in_specs=[pl.BlockSpec((None, N_KV, HEAD_DIM, t), lambda i: (i, 0, 0, 0)), _const_spec(pe.shape),
                  _const_spec(perm.shape),
                  _const_spec(wk1.shape), _const_spec(wv1.shape), _const_spec(wk2.shape), _const_spec(wv2.shape)],
        out_specs=[out_spec, out_spec],
        out_shape=[out, out],
        scratch_shapes=[pltpu.VMEM((2, 2 * n + 2 * SUBLANES, CMP_FLAT), BF16)],
        compiler_params=_params("arbitrary"),
        name="compress_prompt",
    )(x, pe, perm, wk1, wv1, wk2, wv2)


def _paged_prologue(start_fetch, wait_fetch):
    b = pl.program_id(0)
    slot = lax.rem(b, 2)

    @pl.when(b == 0)
    def _():
        start_fetch(0, 0)

    @pl.when(b + 1 < pl.num_programs(0))
    def _():
        start_fetch(b + 1, 1 - slot)

    wait_fetch(slot)
    return slot


def _compress_sample_kernel(pt_ref, cache_hbm, ynew_ref, pe_ref, perm_ref, wk1_ref, wv1_ref, wk2_ref, wv2_ref,
                            kc_ref, vc_ref, buf, sem, y_ref):
    n_pages = pt_ref.shape[1]
    n = n_pages * (PAGE_SIZE // CMP_STRIDE)

    def copy(page, slot, p):
        return pltpu.make_async_copy(cache_hbm.at[page], buf.at[slot, p], sem.at[slot])

    def start_fetch(seq, slot):
        for p in range(n_pages):
            copy(pt_ref[seq, p], slot, p).start()

    def wait_fetch(slot):
        for p in range(n_pages):
            copy(0, slot, p).wait()

    slot = _paged_prologue(start_fetch, wait_fetch)
    ynew = ynew_ref[...]
    row8 = lax.broadcasted_iota(jnp.int32, (SUBLANES, 1), 0)
    for kv in range(2):
        _regroup_pages(lambda page, g, kv=kv: buf[slot, page, N_KV_HEADS * kv + g], n_pages, n, perm_ref,
                       y_ref.at[kv])
    for kv, (w1_ref, w2_ref, out_ref) in enumerate(((wk1_ref, wk2_ref, kc_ref), (wv1_ref, wv2_ref, vc_ref))):
        new8 = jnp.where(row8 == 0, _pick_row(ynew, N_KV_HEADS * kv),
                         jnp.where(row8 == 1, _pick_row(ynew, N_KV_HEADS * kv + 1), 0.0))
        _compress_finish(kv, n, new8, pe_ref, w1_ref, w2_ref, out_ref, y_ref.at[kv], True)


def _compress_sample(page_table, cache, ynew, pe, wk1, wv1, wk2, wv2):
    db, n_pages = page_table.shape
    n = n_pages * (PAGE_SIZE // CMP_STRIDE)
    out = jax.ShapeDtypeStruct((db, N_KV_HEADS, n, HEAD_DIM), BF16)
    out_spec = pl.BlockSpec((None, N_KV_HEADS, n, HEAD_DIM), lambda i, pt: (i, 0, 0, 0))
    const = lambda s: pl.BlockSpec(s, lambda i, pt: (0,) * len(s))
    perm = _dechunk_perm()
    return pl.pallas_call(
        _compress_sample_kernel,
        grid_spec=pltpu.PrefetchScalarGridSpec(
            num_scalar_prefetch=1,
            grid=(db,),
            in_specs=[pl.BlockSpec(memory_space=pl.ANY),
                      pl.BlockSpec((None, SUBLANES, CMP_FLAT), lambda i, pt: (i, 0, 0)),
                      const(pe.shape), const(perm.shape),
                      const(wk1.shape), const(wv1.shape), const(wk2.shape), const(wv2.shape)],
            out_specs=[out_spec, out_spec],
            scratch_shapes=[pltpu.VMEM((2, n_pages, N_KV, HEAD_DIM, PAGE_SIZE), F32),
                            pltpu.SemaphoreType.DMA((2,)),
                            pltpu.VMEM((2, 2 * n + 2 * SUBLANES, CMP_FLAT), BF16)],
        ),
        out_shape=[out, out],
        compiler_params=_params("arbitrary"),
        name="compress_sample",
    )(page_table, cache, ynew, pe, perm, wk1, wv1, wk2, wv2)


def _select_mask(p_slc, t_pos, j, n_slc):
    cur = t_pos // SLC_BLOCK
    forced = (j == 0) | (j == cur) | (j == cur - 1)
    real = j < n_slc
    score = jnp.where(forced, FORCE_SCORE, jnp.where(j <= cur, p_slc, NEG_INF))
    score = jnp.where(real, score, -jnp.inf)
    rank = jnp.zeros(score.shape, jnp.int32)
    for jp in range(n_slc):
        c = score[jp:jp + 1, :]
        rank = rank + ((c > score) | ((c == score) & (j > jp))).astype(jnp.int32)
    return jnp.where((rank < TOPK_BLOCKS) & real, 0.0, NEG_INF)


def _select_all_kernel(p_ref, o_ref, *, pos0, q_pad, n_slc):
    nsp, lanes = p_ref.shape
    n_rows = SUBLANES * (-(-n_slc // SUBLANES))
    t_pos = pos0 + lax.rem(lax.broadcasted_iota(jnp.int32, (n_rows, lanes), 1), q_pad)
    j = lax.broadcasted_iota(jnp.int32, (n_rows, lanes), 0)
    o_ref[0:n_rows, :] = _select_mask(p_ref[0:n_rows, :], t_pos, j, n_slc)
    if n_rows < nsp:
        o_ref[n_rows:, :] = jnp.zeros((nsp - n_rows, lanes), F32)


def _select_all(p_slc, *, pos0, n_slc):
    db, q_pad, width = p_slc.shape
    nsp = width // N_KV_HEADS
    p_t = p_slc.reshape(db, q_pad, N_KV_HEADS, nsp).transpose(3, 0, 2, 1).reshape(nsp, db * N_KV_HEADS * q_pad)
    mask_t = pl.pallas_call(
        functools.partial(_select_all_kernel, pos0=pos0, q_pad=q_pad, n_slc=n_slc),
        grid=(1,),
        in_specs=[_const_spec(p_t.shape)],
        out_specs=_const_spec(p_t.shape),
        out_shape=jax.ShapeDtypeStruct(p_t.shape, F32),
        compiler_params=_params("arbitrary"),
        name="select_all",
    )(p_t)
    return mask_t.reshape(nsp, db, N_KV_HEADS, q_pad).transpose(1, 3, 2, 0).reshape(db, q_pad, width)


CMP_ATTN_TQ = 512
CMP_ATTN_SEQS = 4


def _cmp_attn_kernel(q_ref, kc_ref, vc_ref, gate_ref, agg_ref, oc_ref, sel_ref, **static):
    results = [_cmp_attn_one(q_ref.at[i], kc_ref.at[i], vc_ref.at[i], gate_ref.at[i], agg_ref, **static)
               for i in range(q_ref.shape[0])]
    for i, (heads, masks) in enumerate(results):
        nsp = masks[0].shape[1]
        for g, mask in enumerate(masks):
            sel_ref[i, :, g * nsp:(g + 1) * nsp] = mask.astype(sel_ref.dtype)
        for hd, o in enumerate(heads):
            oc_ref[i, :, hd * HEAD_DIM:(hd + 1) * HEAD_DIM] = o


def _cmp_attn_one(q_ref, kc_ref, vc_ref, gate_ref, agg_ref, *, pos0, tiled, n_slc, blocks_major):
    tq = q_ref.shape[1]
    n = kc_ref.shape[1]
    nsp = agg_ref.shape[0] if blocks_major else agg_ref.shape[1]
    t0 = pos0 + (pl.program_id(1) * tq if tiled else 0)
    t_n = t0 + lax.broadcasted_iota(jnp.int32, (tq, n), 0)
    blk_end = lax.broadcasted_iota(jnp.int32, (tq, n), 1) * CMP_STRIDE + (CMP_BLOCK - 1)
    valid = (blk_end <= t_n)[None]
    if blocks_major:
        n_rows = SUBLANES * (-(-n_slc // SUBLANES))
        t_s = t0 + lax.broadcasted_iota(jnp.int32, (n_rows, tq), 1)
        j = lax.broadcasted_iota(jnp.int32, (n_rows, tq), 0)
        blk_end_t = lax.broadcasted_iota(jnp.int32, (n, GROUP * tq), 0) * CMP_STRIDE + (CMP_BLOCK - 1)
        valid_t = blk_end_t <= t0 + lax.rem(lax.broadcasted_iota(jnp.int32, (n, GROUP * tq), 1), tq)
    gates = gate_ref[...]
    heads, masks = [], []
    for g in range(N_KV_HEADS):
        qg = jnp.concatenate([q_ref[GROUP * g + r].astype(F32) for r in range(GROUP)], axis=0).astype(BF16)
        if blocks_major:
            s = lax.dot_general(kc_ref[g], qg, _NT, preferred_element_type=F32)
            s = jnp.where(valid_t, s, NEG_INF)
            e = jnp.where(valid_t, jnp.exp(s - jnp.max(s, axis=0, keepdims=True)), 0.0)
            p_t = e / jnp.maximum(jnp.sum(e, axis=0, keepdims=True), 1e-30)
            o = jnp.dot(p_t.T.astype(BF16), vc_ref[g], preferred_element_type=F32)
            p_grp = p_t[:, :tq]
            for r in range(1, GROUP):
                p_grp = p_grp + p_t[:, r * tq:(r + 1) * tq]
        else:
            s = lax.dot_general(qg, kc_ref[g], _NT, preferred_element_type=F32).reshape(GROUP, tq, n)
            s = jnp.where(valid, s, NEG_INF)
            e = jnp.where(valid, jnp.exp(s - jnp.max(s, axis=-1, keepdims=True)), 0.0)
            p = e / jnp.maximum(jnp.sum(e, axis=-1, keepdims=True), 1e-30)
            o = jnp.dot(p.reshape(GROUP * tq, n).astype(BF16), vc_ref[g], preferred_element_type=F32)
            p_grp = jnp.sum(p, axis=0)
        p_hi = p_grp.astype(BF16)
        p_lo = (p_grp - p_hi.astype(F32)).astype(BF16)
        if blocks_major:
            p_slc = (jnp.dot(agg_ref[...], p_hi, preferred_element_type=F32)
                     + jnp.dot(agg_ref[...], p_lo, preferred_element_type=F32))
            mask = _select_mask(p_slc[:n_rows], t_s, j, n_slc)
            if n_rows < nsp:
                mask = jnp.concatenate([mask, jnp.zeros((nsp - n_rows, tq), F32)], axis=0)
            mask = mask.T
        else:
            mask = (jnp.dot(p_hi, agg_ref[...], preferred_element_type=F32)
                    + jnp.dot(p_lo, agg_ref[...], preferred_element_type=F32))
        masks.append(mask)
        for r in range(GROUP):
            hd = GROUP * g + r
            heads.append(gates[:, 3 * hd:3 * hd + 1] * o[r * tq:(r + 1) * tq])
    return heads, masks


def _agg_matrix(n_cmp_pad, n_cmp, n_slc, n_slc_pad):
    c0 = np.arange(n_cmp)[:, None] * CMP_STRIDE
    s0 = np.arange(n_slc)[None, :] * SLC_BLOCK
    overlap = np.clip(np.minimum(c0 + CMP_BLOCK, s0 + SLC_BLOCK) - np.maximum(c0, s0), 0, None)
    agg = np.zeros((n_cmp_pad, n_slc_pad), np.float32)
    agg[:n_cmp, :n_slc] = overlap / CMP_BLOCK
    return jnp.asarray(agg, BF16)


def _cmp_attn(q_hm, kc, vc, gates, agg, *, tq, pos0, tiled, n_slc, sel_dtype):
    b, _, t, _ = q_hm.shape
    n = kc.shape[2]
    nsp = agg.shape[1]
    blocks_major = tq % LANES == 0
    if blocks_major:
        agg = agg.T
    ns = CMP_ATTN_SEQS if (t == tq and b % CMP_ATTN_SEQS == 0) else 1
    return pl.pallas_call(
        functools.partial(_cmp_attn_kernel, pos0=pos0, tiled=tiled, n_slc=n_slc, blocks_major=blocks_major),
        grid=(b // ns, t // tq),
        in_specs=[
            pl.BlockSpec((ns, N_HEADS, tq, HEAD_DIM), lambda i, k: (i, 0, k, 0)),
            pl.BlockSpec((ns, N_KV_HEADS, n, HEAD_DIM), lambda i, k: (i, 0, 0, 0)),
            pl.BlockSpec((ns, N_KV_HEADS, n, HEAD_DIM), lambda i, k: (i, 0, 0, 0)),
            pl.BlockSpec((ns, tq, LANES), lambda i, k: (i, k, 0)),
            _const_spec(agg.shape),
        ],
        out_specs=[pl.BlockSpec((ns, tq, NSA_WIDTH), lambda i, k: (i, k, 0)),
                   pl.BlockSpec((ns, tq, N_KV_HEADS * nsp), lambda i, k: (i, k, 0))],
        out_shape=[jax.ShapeDtypeStruct((b, t, NSA_WIDTH), F32),
                   jax.ShapeDtypeStruct((b, t, N_KV_HEADS * nsp), sel_dtype)],
        compiler_params=_params("arbitrary", "arbitrary"),
        name="cmp_attn",
    )(q_hm, kc, vc, gates, agg)


def _expand_matrix(n_keys, n_blk_pad, tk):
    e = np.zeros((n_keys // tk, n_blk_pad, tk), np.float32)
    key = np.arange(n_keys)
    e[key // tk, key // SLC_BLOCK, key % tk] = 1.0
    return jnp.asarray(e, BF16)


def _col_blocks(x, op):
    out = x[:, :LANES]
    for c in range(1, x.shape[1] // LANES):
        out = op(out, x[:, c * LANES:(c + 1) * LANES])
    return out


def _prompt_attn_kernel(q_ref, kvs_ref, kvw_ref, sel_ref, e_ref, oc_ref, gate_ref, o_ref,
                        qx_ref, qg_ref, s_ref, m_ref, l_ref, acc_ref):
    tq = q_ref.shape[1]
    tk = kvs_ref.shape[-1]
    nsp = sel_ref.shape[1] // N_KV_HEADS
    rows = GROUP * tq
    spare = s_ref.shape[1] - 1
    qt = pl.program_id(1)
    n_win = WINDOW // tk
    row = lax.broadcasted_iota(jnp.int32, (tq, tk), 0)
    col = lax.broadcasted_iota(jnp.int32, (tq, tk), 1)
    causal = jnp.where(col <= row, 0.0, NEG_INF)
    win_lo = jnp.where(col >= row, 0.0, NEG_INF)
    gates = gate_ref[...]
    oc = oc_ref[...]
    k_pad = jnp.zeros((qx_ref.shape[2] - nsp - HEAD_DIM, tk), BF16)
    heads = range(N_KV_HEADS)

    def scores_slc(g, kt):
        rhs = jnp.concatenate([e_ref[kt], kvs_ref[g, kt], k_pad], axis=0)
        return jnp.dot(qx_ref[g], rhs, preferred_element_type=F32)

    def scores_win(g, kt):
        return jnp.dot(qg_ref[g], kvw_ref[g, kt], preferred_element_type=F32)

    def find_max(scores, g, kt, slot, bias):
        s = scores(g, kt)
        if bias is not None:
            s = (s.reshape(GROUP, tq, tk) + bias[None]).reshape(rows, tk)
        s_ref[g, slot] = s
        m_ref[g] = jnp.maximum(m_ref[g], _col_blocks(s, jnp.maximum))

    def accumulate(v_ref, g, kt, slot):
        p = jnp.exp2(s_ref[g, slot] - jnp.tile(m_ref[g], (1, tk // LANES)))
        l_ref[g] += _col_blocks(p, jnp.add)
        acc_ref[g] += lax.dot_general(p.astype(BF16), v_ref[N_KV_HEADS + g, kt], _NT, preferred_element_type=F32)

    def maybe(kt, bias):
        exists = kt >= 0
        gate = jnp.where(exists, 0.0, NEG_INF)
        return jnp.maximum(kt, 0), jnp.where(exists, kt, spare), (gate if bias is None else bias + gate)

    def softmax_v(scores, v_ref, loop_pairs, tail):
        def both_passes(fn_pair, fn_tail):
            def body(i, carry):
                for g in heads:
                    fn_pair(g, 2 * i)
                    fn_pair(g, 2 * i + 1)
                return carry
            lax.fori_loop(0, loop_pairs, body, 0)
            for g in heads:
                for kt, slot, bias in tail:
                    fn_tail(g, kt, slot, bias)

        m_ref[...] = jnp.full(m_ref.shape, NEG_INF, F32)
        both_passes(lambda g, kt: find_max(scores, g, kt, kt, None),
                    lambda g, kt, slot, bias: find_max(scores, g, kt, slot, bias))
        for g in heads:
            m_ref[g] = jnp.broadcast_to(jnp.max(m_ref[g], axis=-1, keepdims=True), (rows, LANES))
        l_ref[...] = jnp.zeros(l_ref.shape, F32)
        acc_ref[...] = jnp.zeros(acc_ref.shape, F32)
        both_passes(lambda g, kt: accumulate(v_ref, g, kt, kt),
                    lambda g, kt, slot, bias: accumulate(v_ref, g, kt, slot))
        return [acc_ref[g] / jnp.sum(l_ref[g], axis=-1, keepdims=True) for g in heads]

    for g in heads:
        sel_g = sel_ref[:, g * nsp:(g + 1) * nsp]
        qx_ref[g, :, nsp + HEAD_DIM:] = jnp.zeros((rows, qx_ref.shape[2] - nsp - HEAD_DIM), BF16)
        for r in range(GROUP):
            qx_ref[g, r * tq:(r + 1) * tq, :nsp] = sel_g
            qx_ref[g, r * tq:(r + 1) * tq, nsp:nsp + HEAD_DIM] = q_ref[GROUP * g + r]
            qg_ref[g, r * tq:(r + 1) * tq, :] = q_ref[GROUP * g + r]
    odd_kt = jnp.where(qt % 2 == 1, qt - 1, -1)
    o_s = softmax_v(scores_slc, kvs_ref, qt // 2, [maybe(odd_kt, None), (qt, qt, causal)])
    tail = [maybe(qt - n_win, win_lo)] + [maybe(qt - d, None) for d in range(n_win - 1, 0, -1)]
    o_w = softmax_v(scores_win, kvw_ref, 0, tail + [(qt, qt, causal)])
    for g in heads:
        for r in range(GROUP):
            hd = GROUP * g + r
            cols = slice(hd * HEAD_DIM, (hd + 1) * HEAD_DIM)
            head = slice(r * tq, (r + 1) * tq)
            o_ref[:, cols] = (oc[:, cols] + gates[:, 3 * hd + 1:3 * hd + 2] * o_s[g][head]
                              + gates[:, 3 * hd + 2:3 * hd + 3] * o_w[g][head]).astype(o_ref.dtype)


def _prompt_attn(q_hm, kvs_b, kvw_b, sel, expand, oc, gates):
    b, _, t, _ = q_hm.shape
    tq = kvs_b.shape[-1]
    rows = GROUP * tq
    nsp = sel.shape[2] // N_KV_HEADS
    k_ext = LANES * (-(-(nsp + HEAD_DIM) // LANES))
    tile3 = lambda i, k: (i, k, 0)
    kv_spec = pl.BlockSpec((None,) + kvs_b.shape[1:], lambda i, k: (i, 0, 0, 0, 0))
    return pl.pallas_call(
        _prompt_attn_kernel,
        grid=(b, t // tq),
        in_specs=[
            pl.BlockSpec((None, N_HEADS, tq, HEAD_DIM), lambda i, k: (i, 0, k, 0)),
            kv_spec, kv_spec,
            pl.BlockSpec((None, tq, sel.shape[2]), tile3),
            _const_spec(expand.shape),
            pl.BlockSpec((None, tq, NSA_WIDTH), tile3),
            pl.BlockSpec((None, tq, LANES), tile3),
        ],
        out_specs=pl.BlockSpec((None, tq, NSA_WIDTH), tile3),
        out_shape=jax.ShapeDtypeStruct((b, t, NSA_WIDTH), BF16),
        scratch_shapes=[pltpu.VMEM((N_KV_HEADS, rows, k_ext), BF16), pltpu.VMEM((N_KV_HEADS, rows, HEAD_DIM), BF16),
                        pltpu.VMEM((N_KV_HEADS, t // tq + 1, rows, tq), F32),
                        pltpu.VMEM((N_KV_HEADS, rows, LANES), F32), pltpu.VMEM((N_KV_HEADS, rows, LANES), F32),
                        pltpu.VMEM((N_KV_HEADS, rows, HEAD_DIM), F32)],
        compiler_params=_params("arbitrary", "arbitrary"),
        name="prompt_attn",
    )(q_hm, kvs_b, kvw_b, sel, expand, oc, gates)


Q_PAD = 8


def _group_q(q_ref, g):
    return jnp.concatenate([q_ref[GROUP * g + r] for r in range(GROUP)], axis=0).astype(BF16)


def _two_piece_attention(qg, k_old, v_old, bias_old, k_new, v_new, bias_new):
    def scores(k_t, bias):
        s = jnp.dot(qg, k_t.astype(BF16), preferred_element_type=F32)
        nk = s.shape[1]
        return (s.reshape(GROUP, Q_PAD, nk) + bias[None]).reshape(GROUP * Q_PAD, nk)
    s_old = scores(k_old, bias_old)
    s_new = scores(k_new, bias_new)
    m = jnp.maximum(jnp.max(s_old, axis=-1, keepdims=True), jnp.max(s_new, axis=-1, keepdims=True))
    p_old = jnp.exp(s_old - m)
    p_new = jnp.exp(s_new - m)
    den = jnp.sum(p_old, axis=-1, keepdims=True) + jnp.sum(p_new, axis=-1, keepdims=True)
    pv = (lax.dot_general(p_old.astype(BF16), v_old.astype(BF16), _NT, preferred_element_type=F32)
          + lax.dot_general(p_new.astype(BF16), v_new.astype(BF16), _NT, preferred_element_type=F32))
    return pv / den


def _add_gated(prev_ref, gate_ref, o_ref, per_g, branch):
    prev, gates = prev_ref[...], gate_ref[...]
    for g in range(N_KV_HEADS):
        for r in range(GROUP):
            hd = GROUP * g + r
            cols = slice(hd * HEAD_DIM, (hd + 1) * HEAD_DIM)
            o_ref[:, cols] = (prev[:, cols] + gates[:, 3 * hd + branch:3 * hd + branch + 1]
                              * per_g[g][r * Q_PAD:(r + 1) * Q_PAD])


def _new_rows_bias(dec_seq):
    q = lax.broadcasted_iota(jnp.int32, (Q_PAD, LANES), 0)
    i = lax.broadcasted_iota(jnp.int32, (Q_PAD, LANES), 1) - (LANES - dec_seq)
    return jnp.where((i >= 0) & (i <= q), 0.0, NEG_INF)


def _sample_slc_kernel(pt_ref, cache_hbm, q_ref, sel_ref, new_ref, e_ref, prev_ref, gate_ref, o_ref,
                       buf, sem, *, dec_seq):
    n_pages = pt_ref.shape[1]

    def copy(page, slot, p):
        return pltpu.make_async_copy(cache_hbm.at[page],
                                     buf.at[slot, :, :, pl.ds(p * PAGE_SIZE, PAGE_SIZE)], sem.at[slot])

    def start_fetch(seq, slot):
        for p in range(n_pages):
            copy(pt_ref[seq, p], slot, p).start()

    def wait_fetch(slot):
        for p in range(n_pages):
            copy(0, slot, p).wait()

    slot = _paged_prologue(start_fetch, wait_fetch)
    nsp = sel_ref.shape[1] // N_KV_HEADS
    n_past_blk = n_pages * (PAGE_SIZE // SLC_BLOCK)
    new_bias = _new_rows_bias(dec_seq)
    n_exp = e_ref.shape[0]
    sel = [sel_ref[:, g * nsp:(g + 1) * nsp] for g in range(N_KV_HEADS)]
    sel_past = jnp.concatenate([s[:, :n_exp] for s in sel], axis=0).astype(BF16)
    bias_past = jnp.dot(sel_past, e_ref[...], preferred_element_type=F32)
    per_g = []
    for g in range(N_KV_HEADS):
        bias_old = bias_past[g * Q_PAD:(g + 1) * Q_PAD]
        bias_new = new_bias + sel[g][:, n_past_blk:n_past_blk + 1]
        per_g.append(_two_piece_attention(_group_q(q_ref, g), buf[slot, g], buf[slot, N_KV_HEADS + g], bias_old,
                                          new_ref[g], new_ref[N_KV_HEADS + g], bias_new))
    _add_gated(prev_ref, gate_ref, o_ref, per_g, 1)


def _sample_slc(page_table, cache, q_s, sel, new_t, expand, prev, gates, dec_seq):
    db, n_pages = page_table.shape
    past = n_pages * PAGE_SIZE
    seq3 = lambda i, pt: (i, 0, 0)
    seq4 = lambda i, pt: (i, 0, 0, 0)
    return pl.pallas_call(
        functools.partial(_sample_slc_kernel, dec_seq=dec_seq),
        grid_spec=pltpu.PrefetchScalarGridSpec(
            num_scalar_prefetch=1,
            grid=(db,),
            in_specs=[pl.BlockSpec(memory_space=pl.ANY),
                      pl.BlockSpec((None, N_HEADS, Q_PAD, HEAD_DIM), seq4),
                      pl.BlockSpec((None, Q_PAD, sel.shape[2]), seq3),
                      pl.BlockSpec((None, N_KV, HEAD_DIM, LANES), seq4),
                      pl.BlockSpec(expand.shape, lambda i, pt: (0, 0)),
                      pl.BlockSpec((None, Q_PAD, NSA_WIDTH), seq3),
                      pl.BlockSpec((None, Q_PAD, LANES), seq3)],
            out_specs=pl.BlockSpec((None, Q_PAD, NSA_WIDTH), seq3),
            scratch_shapes=[pltpu.VMEM((2, N_KV, HEAD_DIM, past), F32), pltpu.SemaphoreType.DMA((2,))],
        ),
        out_shape=jax.ShapeDtypeStruct((db, Q_PAD, NSA_WIDTH), F32),
        compiler_params=_params("arbitrary"),
        name="sample_slc",
    )(page_table, cache, q_s, sel, new_t, expand, prev, gates)


WIN_SEQS = 4


def _sample_win_kernel(q_ref, st_ref, new_ref, prev_ref, gate_ref, o_ref, st_out_ref, *, dec_seq):
    wb = st_ref.shape[-1]
    q = lax.broadcasted_iota(jnp.int32, (Q_PAD, wb), 0)
    i = lax.broadcasted_iota(jnp.int32, (Q_PAD, wb), 1)
    bias_old = jnp.where(wb + q - i <= WINDOW, 0.0, NEG_INF)
    new_bias = _new_rows_bias(dec_seq)
    lane = lax.broadcasted_iota(jnp.int32, (HEAD_DIM, wb), 1)
    per_seq = [[_two_piece_attention(_group_q(q_ref.at[b], g), st_ref[b, g], st_ref[b, N_KV_HEADS + g], bias_old,
                                     new_ref[b, g], new_ref[b, N_KV_HEADS + g], new_bias)
                for g in range(N_KV_HEADS)] for b in range(q_ref.shape[0])]
    for b, per_g in enumerate(per_seq):
        _add_gated(prev_ref.at[b], gate_ref.at[b], o_ref.at[b], per_g, 2)
        for j in range(N_KV):
            shifted = pltpu.roll(st_ref[b, j], wb - dec_seq, axis=1)
            st_out_ref[b, j] = jnp.where(lane >= wb - dec_seq, jnp.tile(new_ref[b, j], (1, wb // LANES)), shifted)


def _sample_win(q_s, st_win, new_t, prev, gates, dec_seq):
    db, _, _, wb = st_win.shape
    ns = WIN_SEQS if db % WIN_SEQS == 0 else 1
    seq3 = lambda i: (i, 0, 0)
    seq4 = lambda i: (i, 0, 0, 0)
    return pl.pallas_call(
        functools.partial(_sample_win_kernel, dec_seq=dec_seq),
        grid=(db // ns,),
        in_specs=[pl.BlockSpec((ns, N_HEADS, Q_PAD, HEAD_DIM), seq4),
                  pl.BlockSpec((ns, N_KV, HEAD_DIM, wb), seq4),
                  pl.BlockSpec((ns, N_KV, HEAD_DIM, LANES), seq4),
                  pl.BlockSpec((ns, Q_PAD, NSA_WIDTH), seq3),
                  pl.BlockSpec((ns, Q_PAD, LANES), seq3)],
        out_specs=[pl.BlockSpec((ns, Q_PAD, NSA_WIDTH), seq3), pl.BlockSpec((ns, N_KV, HEAD_DIM, wb), seq4)],
        out_shape=[jax.ShapeDtypeStruct((db, Q_PAD, NSA_WIDTH), F32),
                   jax.ShapeDtypeStruct((db, N_KV, HEAD_DIM, wb), F32)],
        compiler_params=_params("arbitrary"),
        name="sample_win",
    )(q_s, st_win, new_t, prev, gates)


def _pool_project(d_groups, wp_ref, scale_ref, o_ref):
    for gi, d in enumerate(d_groups):
        cols = slice(gi * POOL_GROUP_WIDTH, (gi + 1) * POOL_GROUP_WIDTH)
        y = jnp.dot(d.astype(BF16), wp_ref[gi], preferred_element_type=F32)
        o_ref[:, cols] = (y * scale_ref[:, cols]).astype(o_ref.dtype)


def _pool_tile(ext_ref, t0, wp_ref, scale_ref, o_ref):
    tm = o_ref.shape[0]
    pos = t0 + lax.broadcasted_iota(jnp.int32, (tm, POOL_GROUP_WIDTH), 0)
    d_groups = []
    for gi, w in enumerate(POOL_WINDOWS):
        cols = slice(gi * POOL_GROUP_WIDTH, (gi + 1) * POOL_GROUP_WIDTH)
        e = ext_ref[:, cols]
        acc = e
        span = 1
        while span < w:
            acc = acc + pltpu.roll(acc, span, axis=0)
            span *= 2
        cnt = jnp.minimum(w, pos + 1).astype(F32)
        d_groups.append(acc[POOL_HALO:] / cnt - e[POOL_HALO:])
    _pool_project(d_groups, wp_ref, scale_ref, o_ref)


def _pool_sample_kernel(ext_ref, wp_ref, scale_ref, o_ref, *, past_len, dec_seq):
    db = ext_ref.shape[1]
    for q in range(dec_seq):
        d_groups = []
        for gi, w in enumerate(POOL_WINDOWS):
            cols = slice(gi * POOL_GROUP_WIDTH, (gi + 1) * POOL_GROUP_WIDTH)
            row = POOL_HALO + q
            acc = ext_ref[row, :, cols]
            for i in range(1, w):
                acc = acc + ext_ref[row - i, :, cols]
            cnt = float(min(w, past_len + q + 1))
            d_groups.append(acc / cnt - ext_ref[row, :, cols])
        _pool_project(d_groups, wp_ref, scale_ref, o_ref.at[pl.ds(q * db, db)])


def _pool_sample(ext, wp, scale, past_len, dec_seq):
    rows, db, pw = ext.shape
    return pl.pallas_call(
        functools.partial(_pool_sample_kernel, past_len=past_len, dec_seq=dec_seq),
        grid=(1,),
        in_specs=[_const_spec(ext.shape), _const_spec(wp.shape), _const_spec(scale.shape)],
        out_specs=_const_spec((dec_seq * db, pw)),
        out_shape=jax.ShapeDtypeStruct((dec_seq * db, pw), BF16),
        compiler_params=_params("arbitrary"),
        name="pool_sample",
    )(ext, wp, scale)


def _cmp_weights(w1, w2, pe):
    n_slots = CMP_BLOCK // CMP_STRIDE
    w1s = w1.reshape(n_slots, CMP_FLAT, CMP_HIDDEN)
    w1cat = jnp.concatenate([w1s[h] for h in range(n_slots)], axis=1).astype(BF16)
    pe8 = jnp.pad(pe.reshape(n_slots, CMP_FLAT), ((0, SUBLANES - n_slots), (0, 0)))
    return w1cat, w2.astype(BF16), pe8


def _pad_axis(x, axis, size, front=False):
    pad = [(0, 0)] * x.ndim
    extra = size - x.shape[axis]
    pad[axis] = (extra, 0) if front else (0, extra)
    return jnp.pad(x, pad)


def _rows_to_state(x_fm):
    b, _, _, t = x_fm.shape
    return x_fm.reshape(b, 2, N_KV_HEADS, HEAD_DIM, t).transpose(0, 4, 1, 2, 3)


def _state_to_fm(x):
    b, r = x.shape[:2]
    return x.transpose(0, 2, 3, 4, 1).reshape(b, N_KV, HEAD_DIM, r)


def kernel(x_prompt, x_sample, cache_kv_cmp, cache_kv_slc, page_table, state_kv_win, state_pool, n_ffn1, w_ffn1_gate, w_ffn1_up, w_ffn1_down, n_mix, w_in, w_cmp_k1, w_cmp_k2, pe_cmp_k, w_cmp_v1, w_cmp_v2, pe_cmp_v, w_pool, pool_scale, w_out, n_ffn2, w_ffn2_gate, w_ffn2_up, w_ffn2_down, n_final):
    b, t, d = x_prompt.shape
    db, ds, _ = x_sample.shape
    depth = w_in.shape[0]
    n_pages = page_table.shape[1]
    past = n_pages * PAGE_SIZE
    wb = state_kv_win.shape[2]
    assert t % ATTN_TQ == 0 and t % INPROJ_TM == 0 and t % CMP_ATTN_TQ == 0 and WINDOW % ATTN_TQ == 0 and t >= WINDOW
    assert ds <= Q_PAD and ds <= CMP_STRIDE and wb == WINDOW and past % SLC_BLOCK == 0
    assert (db * ds) % SUBLANES == 0 and n_pages % CMP_PAGE_GROUP == 0 and (t // PAGE_SIZE) % CMP_PAGE_GROUP == 0

    xp = x_prompt.reshape(b * t, d)
    xs = x_sample.reshape(db * ds, d)
    pos_p = jnp.arange(t, dtype=jnp.int32)
    pos_s = past + jnp.arange(ds, dtype=jnp.int32)
    tab_p = _rope_tables(pos_p)
    tab_s = tuple(jnp.tile(a, (db, 1)) for a in _rope_tables(pos_s)[:3]) + tuple(
        jnp.tile(a, (1, db)) for a in _rope_tables(pos_s)[3:])

    n_chunk_p = t // CMP_STRIDE
    n_slc_p = -(-t // SLC_BLOCK)
    agg_p = _agg_matrix(n_chunk_p, n_chunk_p - 1, n_slc_p, LANES * (-(-n_slc_p // LANES)))
    expand_p = _expand_matrix(t, agg_p.shape[1], ATTN_TQ)
    n_chunk_s = past // CMP_STRIDE
    n_slc_s = -(-(past + ds) // SLC_BLOCK)
    agg_s = _agg_matrix(n_chunk_s, n_chunk_s, n_slc_s, LANES * (-(-n_slc_s // LANES)))
    expand_s = _expand_matrix(past, LANES * (-(-(past // SLC_BLOCK) // LANES)), past)[0]

    st_p = ([], [], [], [])
    st_s = ([], [], [], [])
    for l in range(depth):
        last = l == depth - 1
        ffn1 = (n_ffn1[l], w_ffn1_gate[l].astype(BF16), w_ffn1_up[l].astype(BF16), w_ffn1_down[l].astype(BF16))
        ffn2 = (n_ffn2[l], w_ffn2_gate[l].astype(BF16), w_ffn2_up[l].astype(BF16), w_ffn2_down[l].astype(BF16))
        w = w_in[l]
        o_kv, o_gate, o_pool = NSA_WIDTH, NSA_WIDTH + 3 * KV_WIDTH, NSA_WIDTH + 3 * KV_WIDTH + N_GATES
        wq = w[:, :o_kv].astype(BF16)
        wkv_t = w[:, o_kv:o_gate].T.astype(BF16)
        wgt = _pad_axis(w[:, o_gate:o_pool], 1, LANES).astype(BF16)
        wu = w[:, o_pool:].astype(BF16)
        proj = (n_mix[l], wq, wkv_t, wgt, wu)
        wk1, wk2, pek = _cmp_weights(w_cmp_k1[l], w_cmp_k2[l], pe_cmp_k[l])
        wv1, wv2, pev = _cmp_weights(w_cmp_v1[l], w_cmp_v2[l], pe_cmp_v[l])
        pe = jnp.stack([pek, pev])
        wp = w_pool[l].astype(BF16)
        scale = pool_scale[l].reshape(1, POOL_WIDTH)
        wo_nsa = w_out[l][:NSA_WIDTH].astype(BF16)
        wo_pool = w_out[l][NSA_WIDTH:].astype(BF16)

        xp = _ffn(xp, *ffn1)
        q_hm, q2_hm, kvc, kvs, kvw, kvs_b, kvw_b, gates, pool_out, u_tail = _inproj(
            xp, *proj, tab_p, BF16, pool=(wp, scale))
        kc, vc = _compress_prompt(kvc, pe, wk1, wv1, wk2, wv2)
        gates3 = gates.reshape(b, t, LANES)
        oc, sel = _cmp_attn(q_hm, kc, vc, gates3, agg_p, tq=CMP_ATTN_TQ, pos0=0, tiled=True,
                            n_slc=n_slc_p, sel_dtype=BF16)
        o_mix = _prompt_attn(q2_hm, kvs_b, kvw_b, sel, expand_p, oc, gates3)
        mix = (o_mix.reshape(b * t, NSA_WIDTH), pool_out, wo_nsa, wo_pool)
        xp = _ffn(xp, *ffn2, g_final=n_final if last else None, mix=mix)
        st_p[0].append(_rows_to_state(kvc))
        st_p[1].append(_rows_to_state(kvs))
        st_p[2].append(_rows_to_state(kvw[..., t - min(WINDOW, t):]))
        st_p[3].append(u_tail[:, POOL_HALO - POOL_STATE:])

        xs = _ffn(xs, *ffn1)
        q, _, kvc, kvs, kvw, _, _, gates, u = _inproj(xs, *proj, tab_s, F32)
        per_seq = lambda a: a.reshape(N_KV, HEAD_DIM, db, ds).transpose(2, 0, 1, 3)
        kvc_n, kvs_n, kvw_n = per_seq(kvc), per_seq(kvs), per_seq(kvw)
        ynew = _pad_axis(kvc_n.transpose(0, 1, 3, 2), 2, CMP_STRIDE).reshape(db, N_KV, CMP_FLAT)
        cache_c = _state_to_fm(cache_kv_cmp[l])
        kc, vc = _compress_sample(page_table, cache_c, _pad_axis(ynew, 1, SUBLANES), pe, wk1, wv1, wk2, wv2)
        q_s = _pad_axis(q.reshape(N_HEADS, db, ds, HEAD_DIM).transpose(1, 0, 2, 3), 2, Q_PAD)
        gates_s = _pad_axis(gates.reshape(db, ds, LANES), 1, Q_PAD)
        oc, p_slc = _cmp_attn(q_s, kc, vc, gates_s, agg_s, tq=Q_PAD, pos0=past, tiled=False,
                              n_slc=n_slc_s, sel_dtype=F32)
        sel = _select_all(p_slc, pos0=past, n_slc=n_slc_s)
        o_cs =_sample_slc(page_table, _state_to_fm(cache_kv_slc[l]), q_s, sel,
                           _pad_axis(kvs_n, 3, LANES, front=True), expand_s, oc, gates_s, ds)
        o_mix, st_win_new = _sample_win(q_s, _state_to_fm(state_kv_win[l]),
                                        _pad_axis(kvw_n, 3, LANES, front=True), o_cs, gates_s, ds)
        u3 = u.reshape(db, ds, POOL_WIDTH)
        ext = jnp.concatenate([jnp.zeros((db, POOL_HALO - POOL_STATE, POOL_WIDTH), F32), state_pool[l], u3], axis=1)
        pool_out = _pool_sample(ext.transpose(1, 0, 2), wp, scale, past, ds)
        pool_out = pool_out.reshape(ds, db, POOL_WIDTH).transpose(1, 0, 2).reshape(db * ds, POOL_WIDTH)
        mix = (o_mix[:, :ds].reshape(db * ds, NSA_WIDTH), pool_out, wo_nsa, wo_pool)
        xs = _ffn(xs, *ffn2, g_final=n_final if last else None, mix=mix)
        st_s[0].append(_rows_to_state(kvc_n))
        st_s[1].append(_rows_to_state(kvs_n))
        st_s[2].append(_rows_to_state(st_win_new))
        st_s[3].append(jnp.concatenate([state_pool[l], u3], axis=1)[:, ds:])

    return (xp.reshape(b, t, d), xs.reshape(db, ds, d),
            jnp.stack(st_p[0]), jnp.stack(st_p[1]), jnp.stack(st_p[2]), jnp.stack(st_p[3]),
            jnp.stack(st_s[0]), jnp.stack(st_s[1]), jnp.stack(st_s[2]), jnp.stack(st_s[3]))
```

```python
import functools

import numpy as np
import jax
import jax.numpy as jnp
from jax import lax
from jax.experimental import pallas as pl
from jax.experimental.pallas import tpu as pltpu

F32 = jnp.float32
BF16 = jnp.bfloat16

HEAD_DIM = 64
N_HEADS = 8
N_KV_HEADS = 2
GROUP = N_HEADS // N_KV_HEADS
NSA_WIDTH = N_HEADS * HEAD_DIM
N_KV = 2 * N_KV_HEADS
KV_WIDTH = N_KV * HEAD_DIM
ROPE_DIM = HEAD_DIM // 4
ROPE_HALF = ROPE_DIM // 2
ROPE_THETA = 500000.0
CMP_BLOCK = 32
CMP_STRIDE = 16
CMP_HIDDEN = 256
CMP_FLAT = CMP_STRIDE * HEAD_DIM
SLC_BLOCK = 64
TOPK_BLOCKS = 16
WINDOW = 512
PAGE_SIZE = 128
POOL_WINDOWS = (2, 4, 8, 16)
POOL_GROUP_WIDTH = 128
POOL_WIDTH = POOL_GROUP_WIDTH * len(POOL_WINDOWS)
POOL_STATE = max(POOL_WINDOWS) - 1
POOL_HALO = 16
RMS_EPS = 1e-6
FORCE_SCORE = 1e4
NEG_INF = -1e30
ATTN_SCALE = HEAD_DIM ** -0.5
LOG2_E = float(np.log2(np.e))
N_GATES = 3 * N_HEADS

LANES = 128
SUBLANES = 8
VMEM_LIMIT_BYTES = 56 * 1024 * 1024

_NT = (((1,), (1,)), ((), ()))


def _params(*sem):
    return pltpu.CompilerParams(dimension_semantics=sem, vmem_limit_bytes=VMEM_LIMIT_BYTES)


def _rmsnorm(x, g):
    return x * lax.rsqrt(jnp.mean(x * x, axis=-1, keepdims=True) + RMS_EPS) * g


def _const_spec(shape):
    n = len(shape)
    return pl.BlockSpec(shape, lambda *_: (0,) * n)


FFN_TM = 512
FFN_CHUNK = 256


def _ffn_kernel(x_ref, g_ref, wg_ref, wu_ref, wd_ref, gf_ref, *rest, final_norm, mix):
    if mix:
        mo_ref, mp_ref, wo_ref, wp_ref, o_ref, act_ref = rest
        x = (x_ref[...] + jnp.dot(mo_ref[...].astype(BF16), wo_ref[...], preferred_element_type=F32)
             + jnp.dot(mp_ref[...], wp_ref[...], preferred_element_type=F32))
    else:
        o_ref, act_ref = rest
        x = x_ref[...]
    h = _rmsnorm(x, g_ref[...]).astype(BF16)
    d_ff = wg_ref.shape[1]
    for c in range(d_ff // FFN_CHUNK):
        sl = slice(c * FFN_CHUNK, (c + 1) * FFN_CHUNK)
        a = jnp.dot(h, wg_ref[:, sl], preferred_element_type=F32)
        u = jnp.dot(h, wu_ref[:, sl], preferred_element_type=F32)
        act_ref[:, sl] = (a * jax.nn.sigmoid(a) * u).astype(BF16)
    y = x + 0.5 * jnp.dot(act_ref[...], wd_ref[...], preferred_element_type=F32)
    if final_norm:
        y = _rmsnorm(y, gf_ref[...])
    o_ref[...] = y


def _ffn(x, g, wg, wu, wd, g_final=None, mix=None):
    m, d = x.shape
    d_ff = wg.shape[1]
    tm = min(FFN_TM, m)
    final_norm = g_final is not None
    gf = g_final if final_norm else g
    row = lambda i: (i, 0)
    once = lambda shape: pl.BlockSpec(shape, lambda i: (0, 0), pipeline_mode=pl.Buffered(1))
    in_specs = [pl.BlockSpec((tm, d), row), _const_spec((1, d)),
                once((d, d_ff)), once((d, d_ff)), once((d_ff, d)), _const_spec((1, d))]
    args = [x, g.reshape(1, d), wg, wu, wd, gf.reshape(1, d)]
    if mix is not None:
        mo, mp, wo, wp = mix
        in_specs += [pl.BlockSpec((tm, mo.shape[1]), row), pl.BlockSpec((tm, mp.shape[1]), row),
                     once(wo.shape), once(wp.shape)]
        args += [mo, mp, wo, wp]
    return pl.pallas_call(
        functools.partial(_ffn_kernel, final_norm=final_norm, mix=mix is not None),
        grid=(m // tm,),
        in_specs=in_specs,
        out_specs=pl.BlockSpec((tm, d), row),
        out_shape=jax.ShapeDtypeStruct((m, d), F32),
        scratch_shapes=[pltpu.VMEM((tm, d_ff), BF16)],
        compiler_params=_params("arbitrary"),
        name="ffn",
    )(*args)


INPROJ_TM = 1024
ATTN_TQ = 256


def _rope_tables(pos):
    inv = jnp.power(ROPE_THETA, -jnp.arange(ROPE_HALF, dtype=F32) / ROPE_HALF)
    ang = pos.astype(F32)[:, None] * inv[None, :]
    cos, sin = jnp.cos(ang), jnp.sin(ang)
    t = pos.shape[0]
    rest = HEAD_DIM - ROPE_DIM
    z_half = jnp.zeros((t, ROPE_HALF), F32)
    z_rest = jnp.zeros((t, rest), F32)
    cos_t = jnp.concatenate([cos, cos, jnp.ones((t, rest), F32)], axis=1)
    sin_a = jnp.concatenate([-sin, z_half, z_rest], axis=1)
    sin_b = jnp.concatenate([z_half, sin, z_rest], axis=1)
    rep = LANES // HEAD_DIM
    return tuple(jnp.tile(a, (1, rep)) for a in (cos_t, sin_a, sin_b)) + (cos.T, sin.T)


def _rope(z, cos_t, sin_a, sin_b):
    outs = []
    for c in range(z.shape[1] // LANES):
        zc = z[:, c * LANES:(c + 1) * LANES]
        outs.append(zc * cos_t + pltpu.roll(zc, LANES - ROPE_HALF, axis=1) * sin_a
                    + pltpu.roll(zc, ROPE_HALF, axis=1) * sin_b)
    return outs[0] if len(outs) == 1 else jnp.concatenate(outs, axis=1)


def _rope_fm(kt, cos, sin):
    x1, x2 = kt[0:ROPE_HALF], kt[ROPE_HALF:ROPE_DIM]
    return jnp.concatenate([x1 * cos - x2 * sin, x2 * cos + x1 * sin, kt[ROPE_DIM:]], axis=0)


def _inproj_kernel(x_ref, g_ref, wq_ref, wkv_t_ref, wgt_ref, wu_ref,
                   cos_ref, sa_ref, sb_ref, cos_fm_ref, sin_fm_ref, *rest, n_tab, pool):
    if pool:
        (wp_ref, scale_ref, q_ref, q2_ref, kvc_ref, kvs_ref, kvw_ref, kvs_b_ref, kvw_b_ref, gate_ref,
         pool_ref, u_tail_ref, carry_ref, ext_ref) = rest
    else:
        q_ref, q2_ref, kvc_ref, kvs_ref, kvw_ref, kvs_b_ref, kvw_b_ref, gate_ref, u_ref = rest
    h = _rmsnorm(x_ref[...], g_ref[...]).astype(BF16)
    tm = h.shape[0]
    tk = kvs_b_ref.shape[-1]
    cos_t, sin_a, sin_b = cos_ref[...], sa_ref[...], sb_ref[...]
    cos_fm, sin_fm = cos_fm_ref[...], sin_fm_ref[...]
    q = _rope(jnp.dot(h, wq_ref[...], preferred_element_type=F32), cos_t, sin_a, sin_b) * ATTN_SCALE
    q2 = q * LOG2_E
    for hd in range(N_HEADS):
        q_ref[hd] = q[:, hd * HEAD_DIM:(hd + 1) * HEAD_DIM].astype(q_ref.dtype)
        q2_ref[hd] = q2[:, hd * HEAD_DIM:(hd + 1) * HEAD_DIM].astype(BF16)
    kv_t = lax.dot_general(wkv_t_ref[...], h, _NT, preferred_element_type=F32)
    for i, (f_ref, b_ref) in enumerate(((kvc_ref, None), (kvs_ref, kvs_b_ref), (kvw_ref, kvw_b_ref))):
        for j in range(N_KV):
            r0 = i * KV_WIDTH + j * HEAD_DIM
            blk = kv_t[r0:r0 + HEAD_DIM]
            if j < N_KV_HEADS:
                blk = _rope_fm(blk, cos_fm, sin_fm)
            f_ref[j] = blk
            if b_ref is not None:
                for c in range(tm // tk):
                    b_ref[j, c] = blk[:, c * tk:(c + 1) * tk].astype(BF16)
    gate_ref[...] = jax.nn.sigmoid(jnp.dot(h, wgt_ref[...], preferred_element_type=F32))
    u = jnp.dot(h, wu_ref[...], preferred_element_type=F32)
    if pool:
        ti = lax.rem(pl.program_id(0), n_tab)
        ext_ref[0:POOL_HALO, :] = jnp.where(ti > 0, carry_ref[...], 0.0)
        ext_ref[POOL_HALO:, :] = u
        carry_ref[...] = u[tm - POOL_HALO:]
        u_tail_ref[...] = u[tm - POOL_HALO:]
        _pool_tile(ext_ref, ti * tm, wp_ref, scale_ref, pool_ref)
    else:
        u_ref[...] = u


def _inproj(x, g, wq, wkv_t, wgt, wu, tables, q_dtype, pool=None):
    m, d = x.shape
    seq = tables[0].shape[0]
    nseq = m // seq
    tm = min(INPROJ_TM, seq)
    tk = min(ATTN_TQ, tm)
    n_tab = seq // tm
    row = lambda i: (i, 0)
    tab = lambda i: (i % n_tab, 0)
    tab_fm = lambda i: (0, i % n_tab)
    hm = lambda i: (i // n_tab, 0, i % n_tab, 0)
    fm = lambda i: (i // n_tab, 0, 0, i % n_tab)
    fmb = lambda i: (i // n_tab, 0, i % n_tab, 0, 0)
    fm_shape = jax.ShapeDtypeStruct((nseq, N_KV, HEAD_DIM, seq), F32)
    fmb_shape = jax.ShapeDtypeStruct((nseq, N_KV, seq // tk, HEAD_DIM, tk), BF16)
    fm_spec = pl.BlockSpec((None, N_KV, HEAD_DIM, tm), fm)
    fmb_spec = pl.BlockSpec((None, N_KV, tm // tk, HEAD_DIM, tk), fmb)
    in_specs = [
        pl.BlockSpec((tm, d), row), _const_spec((1, d)),
        _const_spec(wq.shape), _const_spec(wkv_t.shape), _const_spec(wgt.shape), _const_spec(wu.shape),
        pl.BlockSpec((tm, LANES), tab), pl.BlockSpec((tm, LANES), tab), pl.BlockSpec((tm, LANES), tab),
        pl.BlockSpec((ROPE_HALF, tm), tab_fm), pl.BlockSpec((ROPE_HALF, tm), tab_fm),
    ]
    out_specs = [
        pl.BlockSpec((None, N_HEADS, tm, HEAD_DIM), hm), pl.BlockSpec((None, N_HEADS, tm, HEAD_DIM), hm),
        fm_spec, fm_spec, fm_spec, fmb_spec, fmb_spec,
        pl.BlockSpec((tm, LANES), row),
    ]
    out_shape = [
        jax.ShapeDtypeStruct((nseq, N_HEADS, seq, HEAD_DIM), q_dtype),
        jax.ShapeDtypeStruct((nseq, N_HEADS, seq, HEAD_DIM), BF16),
        fm_shape, fm_shape, fm_shape, fmb_shape, fmb_shape,
        jax.ShapeDtypeStruct((m, LANES), F32),
    ]
    args = [x, g.reshape(1, d), wq, wkv_t, wgt, wu, *tables]
    scratch = []
    if pool is not None:
        in_specs += [_const_spec(pool[0].shape), _const_spec(pool[1].shape)]
        args += list(pool)
        out_specs += [pl.BlockSpec((tm, POOL_WIDTH), row),
                      pl.BlockSpec((None, POOL_HALO, POOL_WIDTH), lambda i: (i // n_tab, 0, 0))]
        out_shape += [jax.ShapeDtypeStruct((m, POOL_WIDTH), BF16),
                      jax.ShapeDtypeStruct((nseq, POOL_HALO, POOL_WIDTH), F32)]
        scratch = [pltpu.VMEM((POOL_HALO, POOL_WIDTH), F32), pltpu.VMEM((POOL_HALO + tm, POOL_WIDTH), F32)]
    else:
        out_specs += [pl.BlockSpec((tm, POOL_WIDTH), row)]
        out_shape += [jax.ShapeDtypeStruct((m, POOL_WIDTH), F32)]
    return pl.pallas_call(
        functools.partial(_inproj_kernel, n_tab=n_tab, pool=pool is not None),
        grid=(m // tm,),
        in_specs=in_specs,
        out_specs=out_specs,
        out_shape=out_shape,
        scratch_shapes=scratch,
        compiler_params=_params("arbitrary"),
        name="inproj",
    )(*args)


def _gelu_tanh(x):
    return 0.5 * x * (1.0 + jnp.tanh(np.sqrt(2.0 / np.pi).astype(np.float32) * (x + 0.044715 * (x * x * x))))


def _pick_row(blk, r):
    row8 = lax.broadcasted_iota(jnp.int32, (SUBLANES, 1), 0)
    return jnp.sum(jnp.where(row8 == r, blk, 0.0), axis=0, keepdims=True)


def _compress_finish(kv, n, new8, pe_ref, w1_ref, w2_ref, out_ref, y_ref, has_new):
    y_ref[pl.ds(2 * n, 2 * SUBLANES), :] = jnp.concatenate([new8, pe_ref[kv]], axis=0).astype(BF16)
    p = jnp.dot(y_ref[...], w1_ref[...], preferred_element_type=F32)
    p_new, p_pe = p[2 * n:2 * n + SUBLANES], p[2 * n + SUBLANES:2 * n + 2 * SUBLANES]
    bias = _pick_row(p_pe[:, :CMP_HIDDEN], 0) + _pick_row(p_pe[:, CMP_HIDDEN:], 1)
    last = lax.broadcasted_iota(jnp.int32, (n, CMP_HIDDEN), 0) == n - 1
    for g in range(N_KV_HEADS):
        slot0 = p[g * n:(g + 1) * n, :CMP_HIDDEN]
        slot1 = pltpu.roll(p[g * n:(g + 1) * n, CMP_HIDDEN:], n - 1, axis=0)
        if has_new:
            slot1 = jnp.where(last, _pick_row(p_new[:, CMP_HIDDEN:], g), slot1)
        hid = _gelu_tanh(slot0 + slot1 + bias).astype(BF16)
        out_ref[g] = jnp.dot(hid, w2_ref[...], preferred_element_type=F32).astype(BF16)


CMP_PAGE_GROUP = 8


def _dechunk_perm():
    perm = np.zeros((N_KV_HEADS * PAGE_SIZE, 2 * LANES), np.float32)
    cpp = PAGE_SIZE // CMP_STRIDE
    for g in range(N_KV_HEADS):
        for c in range(cpp):
            for j in range(CMP_STRIDE // 2):
                for par in range(2):
                    perm[g * PAGE_SIZE + CMP_STRIDE * c + 2 * j + par, par * LANES + g * HEAD_DIM + j * cpp + c] = 1.0
    return jnp.asarray(perm, BF16)


def _regroup_pages(load_tile, n_pages, n, perm_ref, y_ref):
    pg = CMP_PAGE_GROUP
    cpp = PAGE_SIZE // CMP_STRIDE
    for grp in range(n_pages // pg):
        lhs = jnp.concatenate(
            [jnp.concatenate([load_tile(grp * pg + q, g) for g in range(N_KV_HEADS)], axis=1)
             for q in range(pg)], axis=0).astype(BF16)
        out = jnp.dot(lhs, perm_ref[...], preferred_element_type=F32)
        x = jnp.concatenate(
            [jnp.concatenate([out[q * HEAD_DIM:(q + 1) * HEAD_DIM, :LANES],
                              out[q * HEAD_DIM:(q + 1) * HEAD_DIM, LANES:]], axis=0) for q in range(pg)], axis=1)
        xt = x.T
        r0 = grp * (pg * cpp)
        for g in range(N_KV_HEADS):
            for j in range(CMP_STRIDE // 2):
                piece = jnp.concatenate(
                    [xt[q * LANES + g * HEAD_DIM + j * cpp:q * LANES + g * HEAD_DIM + (j + 1) * cpp]
                     for q in range(pg)], axis=0)
                y_ref[g * n + r0:g * n + r0 + pg * cpp, j * LANES:(j + 1) * LANES] = piece.astype(BF16)


def _compress_prompt_kernel(x_ref, pe_ref, perm_ref, wk1_ref, wv1_ref, wk2_ref, wv2_ref, kc_ref, vc_ref, y_ref):
    t = x_ref.shape[-1]
    n = t // CMP_STRIDE
    for kv, (w1_ref, w2_ref, out_ref) in enumerate(((wk1_ref, wk2_ref, kc_ref), (wv1_ref, wv2_ref, vc_ref))):
        _regroup_pages(lambda page, g: x_ref[N_KV_HEADS * kv + g, :, page * PAGE_SIZE:(page + 1) * PAGE_SIZE],
                       t // PAGE_SIZE, n, perm_ref, y_ref.at[kv])
        _compress_finish(kv, n, jnp.zeros((SUBLANES, CMP_FLAT), F32), pe_ref, w1_ref, w2_ref, out_ref,
                         y_ref.at[kv], False)


def _compress_prompt(x, pe, wk1, wv1, wk2, wv2):
    b, _, _, t = x.shape
    n = t // CMP_STRIDE
    perm = _dechunk_perm()
    out = jax.ShapeDtypeStruct((b, N_KV_HEADS, n, HEAD_DIM), BF16)
    out_spec = pl.BlockSpec((None, N_KV_HEADS, n, HEAD_DIM), lambda i: (i, 0, 0, 0))
    return pl.pallas_call(
        _compress_prompt_kernel,
        grid=(b,),
        in_specs=[pl.BlockSpec((None, N_KV, HEAD_DIM, t), lambda i: (i, 0, 0, 0)), _const_spec(pe.shape),
                  _const_spec(perm.shape),
                  _const_spec(wk1.shape), _const_spec(wv1.shape), _const_spec(wk2.shape), _const_spec(wv2.shape)],
        out_specs=[out_spec, out_spec],
        out_shape=[out, out],
        scratch_shapes=[pltpu.VMEM((2, 2 * n + 2 * SUBLANES, CMP_FLAT), BF16)],
        compiler_params=_params("arbitrary"),
        name="compress_prompt",
    )(x, pe, perm, wk1, wv1, wk2, wv2)


def _paged_prologue(start_fetch, wait_fetch):
    b = pl.program_id(0)
    slot = lax.rem(b, 2)

    @pl.when(b == 0)
    def _():
        start_fetch(0, 0)

    @pl.when(b + 1 < pl.num_programs(0))
    def _():
        start_fetch(b + 1, 1 - slot)

    wait_fetch(slot)
    return slot


def _compress_sample_kernel(pt_ref, cache_hbm, ynew_ref, pe_ref, perm_ref, wk1_ref, wv1_ref, wk2_ref, wv2_ref,
                            kc_ref, vc_ref, buf, sem, y_ref):
    n_pages = pt_ref.shape[1]
    n = n_pages * (PAGE_SIZE // CMP_STRIDE)

    def copy(page, slot, p):
        return pltpu.make_async_copy(cache_hbm.at[page], buf.at[slot, p], sem.at[slot])

    def start_fetch(seq, slot):
        for p in range(n_pages):
            copy(pt_ref[seq, p], slot, p).start()

    def wait_fetch(slot):
        for p in range(n_pages):
            copy(0, slot, p).wait()

    slot = _paged_prologue(start_fetch, wait_fetch)
    ynew = ynew_ref[...]
    row8 = lax.broadcasted_iota(jnp.int32, (SUBLANES, 1), 0)
    for kv in range(2):
        _regroup_pages(lambda page, g, kv=kv: buf[slot, page, N_KV_HEADS * kv + g], n_pages, n, perm_ref,
                       y_ref.at[kv])
    for kv, (w1_ref, w2_ref, out_ref) in enumerate(((wk1_ref, wk2_ref, kc_ref), (wv1_ref, wv2_ref, vc_ref))):
        new8 = jnp.where(row8 == 0, _pick_row(ynew, N_KV_HEADS * kv),
                         jnp.where(row8 == 1, _pick_row(ynew, N_KV_HEADS * kv + 1), 0.0))
        _compress_finish(kv, n, new8, pe_ref, w1_ref, w2_ref, out_ref, y_ref.at[kv], True)


def _compress_sample(page_table, cache, ynew, pe, wk1, wv1, wk2, wv2):
    db, n_pages = page_table.shape
    n = n_pages * (PAGE_SIZE // CMP_STRIDE)
    out = jax.ShapeDtypeStruct((db, N_KV_HEADS, n, HEAD_DIM), BF16)
    out_spec = pl.BlockSpec((None, N_KV_HEADS, n, HEAD_DIM), lambda i, pt: (i, 0, 0, 0))
    const = lambda s: pl.BlockSpec(s, lambda i, pt: (0,) * len(s))
    perm = _dechunk_perm()
    return pl.pallas_call(
        _compress_sample_kernel,
        grid_spec=pltpu.PrefetchScalarGridSpec(
            num_scalar_prefetch=1,
            grid=(db,),
            in_specs=[pl.BlockSpec(memory_space=pl.ANY),
                      pl.BlockSpec((None, SUBLANES, CMP_FLAT), lambda i, pt: (i, 0, 0)),
                      const(pe.shape), const(perm.shape),
                      const(wk1.shape), const(wv1.shape), const(wk2.shape), const(wv2.shape)],
            out_specs=[out_spec, out_spec],
            scratch_shapes=[pltpu.VMEM((2, n_pages, N_KV, HEAD_DIM, PAGE_SIZE), F32),
                            pltpu.SemaphoreType.DMA((2,)),
                            pltpu.VMEM((2, 2 * n + 2 * SUBLANES, CMP_FLAT), BF16)],
        ),
        out_shape=[out, out],
        compiler_params=_params("arbitrary"),
        name="compress_sample",
    )(page_table, cache, ynew, pe, perm, wk1, wv1, wk2, wv2)


def _select_mask(p_slc, t_pos, j, n_slc):
    cur = t_pos // SLC_BLOCK
    forced = (j == 0) | (j == cur) | (j == cur - 1)
    real = j < n_slc
    score = jnp.where(forced, FORCE_SCORE, jnp.where(j <= cur, p_slc, NEG_INF))
    score = jnp.where(real, score, -jnp.inf)
    rank = jnp.zeros(score.shape, jnp.int32)
    for jp in range(n_slc):
        c = score[jp:jp + 1, :]
        rank = rank + ((c > score) | ((c == score) & (j > jp))).astype(jnp.int32)
    return jnp.where((rank < TOPK_BLOCKS) & real, 0.0, NEG_INF)


def _select_all_kernel(p_ref, o_ref, *, pos0, q_pad, n_slc):
    nsp, lanes = p_ref.shape
    n_rows = SUBLANES * (-(-n_slc // SUBLANES))
    t_pos = pos0 + lax.rem(lax.broadcasted_iota(jnp.int32, (n_rows, lanes), 1), q_pad)
    j = lax.broadcasted_iota(jnp.int32, (n_rows, lanes), 0)
    o_ref[0:n_rows, :] = _select_mask(p_ref[0:n_rows, :], t_pos, j, n_slc)
    if n_rows < nsp:
        o_ref[n_rows:, :] = jnp.zeros((nsp - n_rows, lanes), F32)


def _select_all(p_slc, *, pos0, n_slc):
    db, q_pad, width = p_slc.shape
    nsp = width // N_KV_HEADS
    p_t = p_slc.reshape(db, q_pad, N_KV_HEADS, nsp).transpose(3, 0, 2, 1).reshape(nsp, db * N_KV_HEADS * q_pad)
    mask_t = pl.pallas_call(
        functools.partial(_select_all_kernel, pos0=pos0, q_pad=q_pad, n_slc=n_slc),
        grid=(1,),
        in_specs=[_const_spec(p_t.shape)],
        out_specs=_const_spec(p_t.shape),
        out_shape=jax.ShapeDtypeStruct(p_t.shape, F32),
        compiler_params=_params("arbitrary"),
        name="select_all",
    )(p_t)
    return mask_t.reshape(nsp, db, N_KV_HEADS, q_pad).transpose(1, 3, 2, 0).reshape(db, q_pad, width)


CMP_ATTN_TQ = 1024
CMP_ATTN_SEQS = 8


def _cmp_attn_kernel(q_ref, kc_ref, vc_ref, gate_ref, agg_ref, oc_ref, sel_ref, **static):
    results = [_cmp_attn_one(q_ref.at[i], kc_ref.at[i], vc_ref.at[i], gate_ref.at[i], agg_ref, **static)
               for i in range(q_ref.shape[0])]
    for i, (heads, masks) in enumerate(results):
        nsp = masks[0].shape[1]
        for g, mask in enumerate(masks):
            sel_ref[i, :, g * nsp:(g + 1) * nsp] = mask.astype(sel_ref.dtype)
        for hd, o in enumerate(heads):
            oc_ref[i, :, hd * HEAD_DIM:(hd + 1) * HEAD_DIM] = o


def _cmp_attn_one(q_ref, kc_ref, vc_ref, gate_ref, agg_ref, *, pos0, tiled, n_slc, blocks_major):
    tq = q_ref.shape[1]
    n = kc_ref.shape[1]
    nsp = agg_ref.shape[0] if blocks_major else agg_ref.shape[1]
    t0 = pos0 + (pl.program_id(1) * tq if tiled else 0)
    t_n = t0 + lax.broadcasted_iota(jnp.int32, (tq, n), 0)
    blk_end = lax.broadcasted_iota(jnp.int32, (tq, n), 1) * CMP_STRIDE + (CMP_BLOCK - 1)
    valid = (blk_end <= t_n)[None]
    if blocks_major:
        n_rows = SUBLANES * (-(-n_slc // SUBLANES))
        t_s = t0 + lax.broadcasted_iota(jnp.int32, (n_rows, tq), 1)
        j = lax.broadcasted_iota(jnp.int32, (n_rows, tq), 0)
        blk_end_t = lax.broadcasted_iota(jnp.int32, (n, GROUP * tq), 0) * CMP_STRIDE + (CMP_BLOCK - 1)
        valid_t = blk_end_t <= t0 + lax.rem(lax.broadcasted_iota(jnp.int32, (n, GROUP * tq), 1), tq)
    gates = gate_ref[...]
    heads, masks = [], []
    for g in range(N_KV_HEADS):
        qg = jnp.concatenate([q_ref[GROUP * g + r].astype(F32) for r in range(GROUP)], axis=0).astype(BF16)
        if blocks_major:
            s = lax.dot_general(kc_ref[g], qg, _NT, preferred_element_type=F32)
            s = jnp.where(valid_t, s, NEG_INF)
            e = jnp.where(valid_t, jnp.exp(s - jnp.max(s, axis=0, keepdims=True)), 0.0)
            p_t = e / jnp.maximum(jnp.sum(e, axis=0, keepdims=True), 1e-30)
            o = jnp.dot(p_t.T.astype(BF16), vc_ref[g], preferred_element_type=F32)
            p_grp = p_t[:, :tq]
            for r in range(1, GROUP):
                p_grp = p_grp + p_t[:, r * tq:(r + 1) * tq]
        else:
            s = lax.dot_general(qg, kc_ref[g], _NT, preferred_element_type=F32).reshape(GROUP, tq, n)
            s = jnp.where(valid, s, NEG_INF)
            e = jnp.where(valid, jnp.exp(s - jnp.max(s, axis=-1, keepdims=True)), 0.0)
            p = e / jnp.maximum(jnp.sum(e, axis=-1, keepdims=True), 1e-30)
            o = jnp.dot(p.reshape(GROUP * tq, n).astype(BF16), vc_ref[g], preferred_element_type=F32)
            p_grp = jnp.sum(p, axis=0)
        p_hi = p_grp.astype(BF16)
        p_lo = (p_grp - p_hi.astype(F32)).astype(BF16)
        if blocks_major:
            p_slc = (jnp.dot(agg_ref[...], p_hi, preferred_element_type=F32)
                     + jnp.dot(agg_ref[...], p_lo, preferred_element_type=F32))
            mask = _select_mask(p_slc[:n_rows], t_s, j, n_slc)
            if n_rows < nsp:
                mask = jnp.concatenate([mask, jnp.zeros((nsp - n_rows, tq), F32)], axis=0)
            mask = mask.T
        else:
            mask = (jnp.dot(p_hi, agg_ref[...], preferred_element_type=F32)
                    + jnp.dot(p_lo, agg_ref[...], preferred_element_type=F32))
        masks.append(mask)
        for r in range(GROUP):
            hd = GROUP * g + r
            heads.append(gates[:, 3 * hd:3 * hd + 1] * o[r * tq:(r + 1) * tq])
    return heads, masks


def _agg_matrix(n_cmp_pad, n_cmp, n_slc, n_slc_pad):
    c0 = np.arange(n_cmp)[:, None] * CMP_STRIDE
    s0 = np.arange(n_slc)[None, :] * SLC_BLOCK
    overlap = np.clip(np.minimum(c0 + CMP_BLOCK, s0 + SLC_BLOCK) - np.maximum(c0, s0), 0, None)
    agg = np.zeros((n_cmp_pad, n_slc_pad), np.float32)
    agg[:n_cmp, :n_slc] = overlap / CMP_BLOCK
    return jnp.asarray(agg, BF16)


def _cmp_attn(q_hm, kc, vc, gates, agg, *, tq, pos0, tiled, n_slc, sel_dtype):
    b, _, t, _ = q_hm.shape
    n = kc.shape[2]
    nsp = agg.shape[1]
    blocks_major = tq % LANES == 0
    if blocks_major:
        agg = agg.T
    ns = CMP_ATTN_SEQS if (t == tq and b % CMP_ATTN_SEQS == 0) else 1
    return pl.pallas_call(
        functools.partial(_cmp_attn_kernel, pos0=pos0, tiled=tiled, n_slc=n_slc, blocks_major=blocks_major),
        grid=(b // ns, t // tq),
        in_specs=[
            pl.BlockSpec((ns, N_HEADS, tq, HEAD_DIM), lambda i, k: (i, 0, k, 0)),
            pl.BlockSpec((ns, N_KV_HEADS, n, HEAD_DIM), lambda i, k: (i, 0, 0, 0)),
            pl.BlockSpec((ns, N_KV_HEADS, n, HEAD_DIM), lambda i, k: (i, 0, 0, 0)),
            pl.BlockSpec((ns, tq, LANES), lambda i, k: (i, k, 0)),
            _const_spec(agg.shape),
        ],
        out_specs=[pl.BlockSpec((ns, tq, NSA_WIDTH), lambda i, k: (i, k, 0)),
                   pl.BlockSpec((ns, tq, N_KV_HEADS * nsp), lambda i, k: (i, k, 0))],
        out_shape=[jax.ShapeDtypeStruct((b, t, NSA_WIDTH), F32),
                   jax.ShapeDtypeStruct((b, t, N_KV_HEADS * nsp), sel_dtype)],
        compiler_params=_params("arbitrary", "arbitrary"),
        name="cmp_attn",
    )(q_hm, kc, vc, gates, agg)


def _expand_matrix(n_keys, n_blk_pad, tk):
    e = np.zeros((n_keys // tk, n_blk_pad, tk), np.float32)
    key = np.arange(n_keys)
    e[key // tk, key // SLC_BLOCK, key % tk] = 1.0
    return jnp.asarray(e, BF16)


def _col_blocks(x, op):
    out = x[:, :LANES]
    for c in range(1, x.shape[1] // LANES):
        out = op(out, x[:, c * LANES:(c + 1) * LANES])
    return out


def _prompt_attn_kernel(q_ref, kvs_ref, kvw_ref, sel_ref, e_ref, oc_ref, gate_ref, o_ref,
                        qx_ref, qg_ref, s_ref, m_ref, l_ref, acc_ref):
    tq = q_ref.shape[1]
    tk = kvs_ref.shape[-1]
    nsp = sel_ref.shape[1] // N_KV_HEADS
    rows = GROUP * tq
    spare = s_ref.shape[1] - 1
    qt = pl.program_id(1)
    n_win = WINDOW // tk
    row = lax.broadcasted_iota(jnp.int32, (tq, tk), 0)
    col = lax.broadcasted_iota(jnp.int32, (tq, tk), 1)
    causal = jnp.where(col <= row, 0.0, NEG_INF)
    win_lo = jnp.where(col >= row, 0.0, NEG_INF)
    gates = gate_ref[...]
    oc = oc_ref[...]
    k_pad = jnp.zeros((qx_ref.shape[2] - nsp - HEAD_DIM, tk), BF16)
    heads = range(N_KV_HEADS)

    def scores_slc(g, kt):
        rhs = jnp.concatenate([e_ref[kt], kvs_ref[g, kt], k_pad], axis=0)
        return jnp.dot(qx_ref[g], rhs, preferred_element_type=F32)

    def scores_win(g, kt):
        return jnp.dot(qg_ref[g], kvw_ref[g, kt], preferred_element_type=F32)

    def find_max(scores, g, kt, slot, bias):
        s = scores(g, kt)
        if bias is not None:
            s = (s.reshape(GROUP, tq, tk) + bias[None]).reshape(rows, tk)
        s_ref[g, slot] = s
        m_ref[g] = jnp.maximum(m_ref[g], _col_blocks(s, jnp.maximum))

    def accumulate(v_ref, g, kt, slot):
        p = jnp.exp2(s_ref[g, slot] - jnp.tile(m_ref[g], (1, tk // LANES)))
        l_ref[g] += _col_blocks(p, jnp.add)
        acc_ref[g] += lax.dot_general(p.astype(BF16), v_ref[N_KV_HEADS + g, kt], _NT, preferred_element_type=F32)

    def maybe(kt, bias):
        exists = kt >= 0
        gate = jnp.where(exists, 0.0, NEG_INF)
        return jnp.maximum(kt, 0), jnp.where(exists, kt, spare), (gate if bias is None else bias + gate)

    def softmax_v(scores, v_ref, loop_pairs, tail):
        def both_passes(fn_pair, fn_tail):
            def body(i, carry):
                for g in heads:
                    fn_pair(g, 2 * i)
                    fn_pair(g, 2 * i + 1)
                return carry
            lax.fori_loop(0, loop_pairs, body, 0)
            for g in heads:
                for kt, slot, bias in tail:
                    fn_tail(g, kt, slot, bias)

        m_ref[...] = jnp.full(m_ref.shape, NEG_INF, F32)
        both_passes(lambda g, kt: find_max(scores, g, kt, kt, None),
                    lambda g, kt, slot, bias: find_max(scores, g, kt, slot, bias))
        for g in heads:
            m_ref[g] = jnp.broadcast_to(jnp.max(m_ref[g], axis=-1, keepdims=True), (rows, LANES))
        l_ref[...] = jnp.zeros(l_ref.shape, F32)
        acc_ref[...] = jnp.zeros(acc_ref.shape, F32)
        both_passes(lambda g, kt: accumulate(v_ref, g, kt, kt),
                    lambda g, kt, slot, bias: accumulate(v_ref, g, kt, slot))
        return [acc_ref[g] / jnp.sum(l_ref[g], axis=-1, keepdims=True) for g in heads]

    for g in heads:
        sel_g = sel_ref[:, g * nsp:(g + 1) * nsp]
        qx_ref[g, :, nsp + HEAD_DIM:] = jnp.zeros((rows, qx_ref.shape[2] - nsp - HEAD_DIM), BF16)
        for r in range(GROUP):
            qx_ref[g, r * tq:(r + 1) * tq, :nsp] = sel_g
            qx_ref[g, r * tq:(r + 1) * tq, nsp:nsp + HEAD_DIM] = q_ref[GROUP * g + r]
            qg_ref[g, r * tq:(r + 1) * tq, :] = q_ref[GROUP * g + r]
    odd_kt = jnp.where(qt % 2 == 1, qt - 1, -1)
    o_s = softmax_v(scores_slc, kvs_ref, qt // 2, [maybe(odd_kt, None), (qt, qt, causal)])
    tail = [maybe(qt - n_win, win_lo)] + [maybe(qt - d, None) for d in range(n_win - 1, 0, -1)]
    o_w = softmax_v(scores_win, kvw_ref, 0, tail + [(qt, qt, causal)])
    for g in heads:
        for r in range(GROUP):
            hd = GROUP * g + r
            cols = slice(hd * HEAD_DIM, (hd + 1) * HEAD_DIM)
            head = slice(r * tq, (r + 1) * tq)
            o_ref[:, cols] = (oc[:, cols] + gates[:, 3 * hd + 1:3 * hd + 2] * o_s[g][head]
                              + gates[:, 3 * hd + 2:3 * hd + 3] * o_w[g][head]).astype(o_ref.dtype)


def _prompt_attn(q_hm, kvs_b, kvw_b, sel, expand, oc, gates):
    b, _, t, _ = q_hm.shape
    tq = kvs_b.shape[-1]
    rows = GROUP * tq
    nsp = sel.shape[2] // N_KV_HEADS
    k_ext = LANES * (-(-(nsp + HEAD_DIM) // LANES))
    tile3 = lambda i, k: (i, k, 0)
    kv_spec = pl.BlockSpec((None,) + kvs_b.shape[1:], lambda i, k: (i, 0, 0, 0, 0))
    return pl.pallas_call(
        _prompt_attn_kernel,
        grid=(b, t // tq),
        in_specs=[
            pl.BlockSpec((None, N_HEADS, tq, HEAD_DIM), lambda i, k: (i, 0, k, 0)),
            kv_spec, kv_spec,
            pl.BlockSpec((None, tq, sel.shape[2]), tile3),
            _const_spec(expand.shape),
            pl.BlockSpec((None, tq, NSA_WIDTH), tile3),
            pl.BlockSpec((None, tq, LANES), tile3),
        ],
        out_specs=pl.BlockSpec((None, tq, NSA_WIDTH), tile3),
        out_shape=jax.ShapeDtypeStruct((b, t, NSA_WIDTH), BF16),
        scratch_shapes=[pltpu.VMEM((N_KV_HEADS, rows, k_ext), BF16), pltpu.VMEM((N_KV_HEADS, rows, HEAD_DIM), BF16),
                        pltpu.VMEM((N_KV_HEADS, t // tq + 1, rows, tq), F32),
                        pltpu.VMEM((N_KV_HEADS, rows, LANES), F32), pltpu.VMEM((N_KV_HEADS, rows, LANES), F32),
                        pltpu.VMEM((N_KV_HEADS, rows, HEAD_DIM), F32)],
        compiler_params=_params("arbitrary", "arbitrary"),
        name="prompt_attn",
    )(q_hm, kvs_b, kvw_b, sel, expand, oc, gates)


Q_PAD = 8


def _group_q(q_ref, g):
    return jnp.concatenate([q_ref[GROUP * g + r] for r in range(GROUP)], axis=0).astype(BF16)


def _two_piece_attention(qg, k_old, v_old, bias_old, k_new, v_new, bias_new):
    def scores(k_t, bias):
        s = jnp.dot(qg, k_t.astype(BF16), preferred_element_type=F32)
        nk = s.shape[1]
        return (s.reshape(GROUP, Q_PAD, nk) + bias[None]).reshape(GROUP * Q_PAD, nk)
    s_old = scores(k_old, bias_old)
    s_new = scores(k_new, bias_new)
    m = jnp.maximum(jnp.max(s_old, axis=-1, keepdims=True), jnp.max(s_new, axis=-1, keepdims=True))
    p_old = jnp.exp(s_old - m)
    p_new = jnp.exp(s_new - m)
    den = jnp.sum(p_old, axis=-1, keepdims=True) + jnp.sum(p_new, axis=-1, keepdims=True)
    pv = (lax.dot_general(p_old.astype(BF16), v_old.astype(BF16), _NT, preferred_element_type=F32)
          + lax.dot_general(p_new.astype(BF16), v_new.astype(BF16), _NT, preferred_element_type=F32))
    return pv / den


def _add_gated(prev_ref, gate_ref, o_ref, per_g, branch):
    prev, gates = prev_ref[...], gate_ref[...]
    for g in range(N_KV_HEADS):
        for r in range(GROUP):
            hd = GROUP * g + r
            cols = slice(hd * HEAD_DIM, (hd + 1) * HEAD_DIM)
            o_ref[:, cols] = (prev[:, cols] + gates[:, 3 * hd + branch:3 * hd + branch + 1]
                              * per_g[g][r * Q_PAD:(r + 1) * Q_PAD])


def _new_rows_bias(dec_seq):
    q = lax.broadcasted_iota(jnp.int32, (Q_PAD, LANES), 0)
    i = lax.broadcasted_iota(jnp.int32, (Q_PAD, LANES), 1) - (LANES - dec_seq)
    return jnp.where((i >= 0) & (i <= q), 0.0, NEG_INF)


def _sample_slc_kernel(pt_ref, cache_hbm, q_ref, sel_ref, new_ref, e_ref, prev_ref, gate_ref, o_ref,
                       buf, sem, *, dec_seq):
    n_pages = pt_ref.shape[1]

    def copy(page, slot, p):
        return pltpu.make_async_copy(cache_hbm.at[page],
                                     buf.at[slot, :, :, pl.ds(p * PAGE_SIZE, PAGE_SIZE)], sem.at[slot])

    def start_fetch(seq, slot):
        for p in range(n_pages):
            copy(pt_ref[seq, p], slot, p).start()

    def wait_fetch(slot):
        for p in range(n_pages):
            copy(0, slot, p).wait()

    slot = _paged_prologue(start_fetch, wait_fetch)
    nsp = sel_ref.shape[1] // N_KV_HEADS
    n_past_blk = n_pages * (PAGE_SIZE // SLC_BLOCK)
    new_bias = _new_rows_bias(dec_seq)
    n_exp = e_ref.shape[0]
    sel = [sel_ref[:, g * nsp:(g + 1) * nsp] for g in range(N_KV_HEADS)]
    sel_past = jnp.concatenate([s[:, :n_exp] for s in sel], axis=0).astype(BF16)
    bias_past = jnp.dot(sel_past, e_ref[...], preferred_element_type=F32)
    per_g = []
    for g in range(N_KV_HEADS):
        bias_old = bias_past[g * Q_PAD:(g + 1) * Q_PAD]
        bias_new = new_bias + sel[g][:, n_past_blk:n_past_blk + 1]
        per_g.append(_two_piece_attention(_group_q(q_ref, g), buf[slot, g], buf[slot, N_KV_HEADS + g], bias_old,
                                          new_ref[g], new_ref[N_KV_HEADS + g], bias_new))
    _add_gated(prev_ref, gate_ref, o_ref, per_g, 1)


def _sample_slc(page_table, cache, q_s, sel, new_t, expand, prev, gates, dec_seq):
    db, n_pages = page_table.shape
    past = n_pages * PAGE_SIZE
    seq3 = lambda i, pt: (i, 0, 0)
    seq4 = lambda i, pt: (i, 0, 0, 0)
    return pl.pallas_call(
        functools.partial(_sample_slc_kernel, dec_seq=dec_seq),
        grid_spec=pltpu.PrefetchScalarGridSpec(
            num_scalar_prefetch=1,
            grid=(db,),
            in_specs=[pl.BlockSpec(memory_space=pl.ANY),
                      pl.BlockSpec((None, N_HEADS, Q_PAD, HEAD_DIM), seq4),
                      pl.BlockSpec((None, Q_PAD, sel.shape[2]), seq3),
                      pl.BlockSpec((None, N_KV, HEAD_DIM, LANES), seq4),
                      pl.BlockSpec(expand.shape, lambda i, pt: (0, 0)),
                      pl.BlockSpec((None, Q_PAD, NSA_WIDTH), seq3),
                      pl.BlockSpec((None, Q_PAD, LANES), seq3)],
            out_specs=pl.BlockSpec((None, Q_PAD, NSA_WIDTH), seq3),
            scratch_shapes=[pltpu.VMEM((2, N_KV, HEAD_DIM, past), F32), pltpu.SemaphoreType.DMA((2,))],
        ),
        out_shape=jax.ShapeDtypeStruct((db, Q_PAD, NSA_WIDTH), F32),
        compiler_params=_params("arbitrary"),
        name="sample_slc",
    )(page_table, cache, q_s, sel, new_t, expand, prev, gates)


WIN_SEQS = 4


def _sample_win_kernel(q_ref, st_ref, new_ref, prev_ref, gate_ref, o_ref, st_out_ref, *, dec_seq):
    wb = st_ref.shape[-1]
    q = lax.broadcasted_iota(jnp.int32, (Q_PAD, wb), 0)
    i = lax.broadcasted_iota(jnp.int32, (Q_PAD, wb), 1)
    bias_old = jnp.where(wb + q - i <= WINDOW, 0.0, NEG_INF)
    new_bias = _new_rows_bias(dec_seq)
    lane = lax.broadcasted_iota(jnp.int32, (HEAD_DIM, wb), 1)
    per_seq = [[_two_piece_attention(_group_q(q_ref.at[b], g), st_ref[b, g], st_ref[b, N_KV_HEADS + g], bias_old,
                                     new_ref[b, g], new_ref[b, N_KV_HEADS + g], new_bias)
                for g in range(N_KV_HEADS)] for b in range(q_ref.shape[0])]
    for b, per_g in enumerate(per_seq):
        _add_gated(prev_ref.at[b], gate_ref.at[b], o_ref.at[b], per_g, 2)
        for j in range(N_KV):
            shifted = pltpu.roll(st_ref[b, j], wb - dec_seq, axis=1)
            st_out_ref[b, j] = jnp.where(lane >= wb - dec_seq, jnp.tile(new_ref[b, j], (1, wb // LANES)), shifted)


def _sample_win(q_s, st_win, new_t, prev, gates, dec_seq):
    db, _, _, wb = st_win.shape
    ns = WIN_SEQS if db % WIN_SEQS == 0 else 1
    seq3 = lambda i: (i, 0, 0)
    seq4 = lambda i: (i, 0, 0, 0)
    return pl.pallas_call(
        functools.partial(_sample_win_kernel, dec_seq=dec_seq),
        grid=(db // ns,),
        in_specs=[pl.BlockSpec((ns, N_HEADS, Q_PAD, HEAD_DIM), seq4),
                  pl.BlockSpec((ns, N_KV, HEAD_DIM, wb), seq4),
                  pl.BlockSpec((ns, N_KV, HEAD_DIM, LANES), seq4),
                  pl.BlockSpec((ns, Q_PAD, NSA_WIDTH), seq3),
                  pl.BlockSpec((ns, Q_PAD, LANES), seq3)],
        out_specs=[pl.BlockSpec((ns, Q_PAD, NSA_WIDTH), seq3), pl.BlockSpec((ns, N_KV, HEAD_DIM, wb), seq4)],
        out_shape=[jax.ShapeDtypeStruct((db, Q_PAD, NSA_WIDTH), F32),
                   jax.ShapeDtypeStruct((db, N_KV, HEAD_DIM, wb), F32)],
        compiler_params=_params("arbitrary"),
        name="sample_win",
    )(q_s, st_win, new_t, prev, gates)


def _pool_project(d_groups, wp_ref, scale_ref, o_ref):
    for gi, d in enumerate(d_groups):
        cols = slice(gi * POOL_GROUP_WIDTH, (gi + 1) * POOL_GROUP_WIDTH)
        y = jnp.dot(d.astype(BF16), wp_ref[gi], preferred_element_type=F32)
        o_ref[:, cols] = (y * scale_ref[:, cols]).astype(o_ref.dtype)


def _pool_tile(ext_ref, t0, wp_ref, scale_ref, o_ref):
    tm = o_ref.shape[0]
    pos = t0 + lax.broadcasted_iota(jnp.int32, (tm, POOL_GROUP_WIDTH), 0)
    d_groups = []
    for gi, w in enumerate(POOL_WINDOWS):
        cols = slice(gi * POOL_GROUP_WIDTH, (gi + 1) * POOL_GROUP_WIDTH)
        e = ext_ref[:, cols]
        acc = e
        span = 1
        while span < w:
            acc = acc + pltpu.roll(acc, span, axis=0)
            span *= 2
        cnt = jnp.minimum(w, pos + 1).astype(F32)
        d_groups.append(acc[POOL_HALO:] / cnt - e[POOL_HALO:])
    _pool_project(d_groups, wp_ref, scale_ref, o_ref)


def _pool_sample_kernel(ext_ref, wp_ref, scale_ref, o_ref, *, past_len, dec_seq):
    db = ext_ref.shape[1]
    for q in range(dec_seq):
        d_groups = []
        for gi, w in enumerate(POOL_WINDOWS):
            cols = slice(gi * POOL_GROUP_WIDTH, (gi + 1) * POOL_GROUP_WIDTH)
            row = POOL_HALO + q
            acc = ext_ref[row, :, cols]
            for i in range(1, w):
                acc = acc + ext_ref[row - i, :, cols]
            cnt = float(min(w, past_len + q + 1))
            d_groups.append(acc / cnt - ext_ref[row, :, cols])
        _pool_project(d_groups, wp_ref, scale_ref, o_ref.at[pl.ds(q * db, db)])


def _pool_sample(ext, wp, scale, past_len, dec_seq):
    rows, db, pw = ext.shape
    return pl.pallas_call(
        functools.partial(_pool_sample_kernel, past_len=past_len, dec_seq=dec_seq),
        grid=(1,),
        in_specs=[_const_spec(ext.shape), _const_spec(wp.shape), _const_spec(scale.shape)],
        out_specs=_const_spec((dec_seq * db, pw)),
        out_shape=jax.ShapeDtypeStruct((dec_seq * db, pw), BF16),
        compiler_params=_params("arbitrary"),
        name="pool_sample",
    )(ext, wp, scale)


def _cmp_weights(w1, w2, pe):
    n_slots = CMP_BLOCK // CMP_STRIDE
    w1s = w1.reshape(n_slots, CMP_FLAT, CMP_HIDDEN)
    w1cat = jnp.concatenate([w1s[h] for h in range(n_slots)], axis=1).astype(BF16)
    pe8 = jnp.pad(pe.reshape(n_slots, CMP_FLAT), ((0, SUBLANES - n_slots), (0, 0)))
    return w1cat, w2.astype(BF16), pe8


def _pad_axis(x, axis, size, front=False):
    pad = [(0, 0)] * x.ndim
    extra = size - x.shape[axis]
    pad[axis] = (extra, 0) if front else (0, extra)
    return jnp.pad(x, pad)


def _rows_to_state(x_fm):
    b, _, _, t = x_fm.shape
    return x_fm.reshape(b, 2, N_KV_HEADS, HEAD_DIM, t).transpose(0, 4, 1, 2, 3)


def _state_to_fm(x):
    b, r = x.shape[:2]
    return x.transpose(0, 2, 3, 4, 1).reshape(b, N_KV, HEAD_DIM, r)


def kernel(x_prompt, x_sample, cache_kv_cmp, cache_kv_slc, page_table, state_kv_win, state_pool, n_ffn1, w_ffn1_gate, w_ffn1_up, w_ffn1_down, n_mix, w_in, w_cmp_k1, w_cmp_k2, pe_cmp_k, w_cmp_v1, w_cmp_v2, pe_cmp_v, w_pool, pool_scale, w_out, n_ffn2, w_ffn2_gate, w_ffn2_up, w_ffn2_down, n_final):
    b, t, d = x_prompt.shape
    db, ds, _ = x_sample.shape
    depth = w_in.shape[0]
    n_pages = page_table.shape[1]
    past = n_pages * PAGE_SIZE
    wb = state_kv_win.shape[2]
    assert t % ATTN_TQ == 0 and t % INPROJ_TM == 0 and t % CMP_ATTN_TQ == 0 and WINDOW % ATTN_TQ == 0 and t >= WINDOW
    assert ds <= Q_PAD and ds <= CMP_STRIDE and wb == WINDOW and past % SLC_BLOCK == 0
    assert (db * ds) % SUBLANES == 0 and n_pages % CMP_PAGE_GROUP == 0 and (t // PAGE_SIZE) % CMP_PAGE_GROUP == 0

    xp = x_prompt.reshape(b * t, d)
    xs = x_sample.reshape(db * ds, d)
    pos_p = jnp.arange(t, dtype=jnp.int32)
    pos_s = past + jnp.arange(ds, dtype=jnp.int32)
    tab_p = _rope_tables(pos_p)
    tab_s = tuple(jnp.tile(a, (db, 1)) for a in _rope_tables(pos_s)[:3]) + tuple(
        jnp.tile(a, (1, db)) for a in _rope_tables(pos_s)[3:])

    n_chunk_p = t // CMP_STRIDE
    n_slc_p = -(-t // SLC_BLOCK)
    agg_p = _agg_matrix(n_chunk_p, n_chunk_p - 1, n_slc_p, LANES * (-(-n_slc_p // LANES)))
    expand_p = _expand_matrix(t, agg_p.shape[1], ATTN_TQ)
    n_chunk_s = past // CMP_STRIDE
    n_slc_s = -(-(past + ds) // SLC_BLOCK)
    agg_s = _agg_matrix(n_chunk_s, n_chunk_s, n_slc_s, LANES * (-(-n_slc_s // LANES)))
    expand_s = _expand_matrix(past, LANES * (-(-(past // SLC_BLOCK) // LANES)), past)[0]

    st_p = ([], [], [], [])
    st_s = ([], [], [], [])
    for l in range(depth):
        last = l == depth - 1
        ffn1 = (n_ffn1[l], w_ffn1_gate[l].astype(BF16), w_ffn1_up[l].astype(BF16), w_ffn1_down[l].astype(BF16))
        ffn2 = (n_ffn2[l], w_ffn2_gate[l].astype(BF16), w_ffn2_up[l].astype(BF16), w_ffn2_down[l].astype(BF16))
        w = w_in[l]
        o_kv, o_gate, o_pool = NSA_WIDTH, NSA_WIDTH + 3 * KV_WIDTH, NSA_WIDTH + 3 * KV_WIDTH + N_GATES
        wq = w[:, :o_kv].astype(BF16)
        wkv_t = w[:, o_kv:o_gate].T.astype(BF16)
        wgt = _pad_axis(w[:, o_gate:o_pool], 1, LANES).astype(BF16)
        wu = w[:, o_pool:].astype(BF16)
        proj = (n_mix[l], wq, wkv_t, wgt, wu)
        wk1, wk2, pek = _cmp_weights(w_cmp_k1[l], w_cmp_k2[l], pe_cmp_k[l])
        wv1, wv2, pev = _cmp_weights(w_cmp_v1[l], w_cmp_v2[l], pe_cmp_v[l])
        pe = jnp.stack([pek, pev])
        wp = w_pool[l].astype(BF16)
        scale = pool_scale[l].reshape(1, POOL_WIDTH)
        wo_nsa = w_out[l][:NSA_WIDTH].astype(BF16)
        wo_pool = w_out[l][NSA_WIDTH:].astype(BF16)

        xp = _ffn(xp, *ffn1)
        q_hm, q2_hm, kvc, kvs, kvw, kvs_b, kvw_b, gates, pool_out, u_tail = _inproj(
            xp, *proj, tab_p, BF16, pool=(wp, scale))
        kc, vc = _compress_prompt(kvc, pe, wk1, wv1, wk2, wv2)
        gates3 = gates.reshape(b, t, LANES)
        oc, sel = _cmp_attn(q_hm, kc, vc, gates3, agg_p, tq=CMP_ATTN_TQ, pos0=0, tiled=True,
                            n_slc=n_slc_p, sel_dtype=BF16)
        o_mix = _prompt_attn(q2_hm, kvs_b, kvw_b, sel, expand_p, oc, gates3)
        mix = (o_mix.reshape(b * t, NSA_WIDTH), pool_out, wo_nsa, wo_pool)
        xp = _ffn(xp, *ffn2, g_final=n_final if last else None, mix=mix)
        st_p[0].append(_rows_to_state(kvc))
        st_p[1].append(_rows_to_state(kvs))
        st_p[2].append(_rows_to_state(kvw[..., t - min(WINDOW, t):]))
        st_p[3].append(u_tail[:, POOL_HALO - POOL_STATE:])

        xs = _ffn(xs, *ffn1)
        q, _, kvc, kvs, kvw, _, _, gates, u = _inproj(xs, *proj, tab_s, F32)
        per_seq = lambda a: a.reshape(N_KV, HEAD_DIM, db, ds).transpose(2, 0, 1, 3)
        kvc_n, kvs_n, kvw_n = per_seq(kvc), per_seq(kvs), per_seq(kvw)
        ynew = _pad_axis(kvc_n.transpose(0, 1, 3, 2), 2, CMP_STRIDE).reshape(db, N_KV, CMP_FLAT)
        cache_c = _state_to_fm(cache_kv_cmp[l])
        kc, vc = _compress_sample(page_table, cache_c, _pad_axis(ynew, 1, SUBLANES), pe, wk1, wv1, wk2, wv2)
        q_s = _pad_axis(q.reshape(N_HEADS, db, ds, HEAD_DIM).transpose(1, 0, 2, 3), 2, Q_PAD)
        gates_s = _pad_axis(gates.reshape(db, ds, LANES), 1, Q_PAD)
        oc, p_slc = _cmp_attn(q_s, kc, vc, gates_s, agg_s, tq=Q_PAD, pos0=past, tiled=False,
                              n_slc=n_slc_s, sel_dtype=F32)
        sel = _select_all(p_slc, pos0=past, n_slc=n_slc_s)
        o_cs =_sample_slc(page_table, _state_to_fm(cache_kv_slc[l]), q_s, sel,
                           _pad_axis(kvs_n, 3, LANES, front=True), expand_s, oc, gates_s, ds)
        o_mix, st_win_new = _sample_win(q_s, _state_to_fm(state_kv_win[l]),
                                        _pad_axis(kvw_n, 3, LANES, front=True), o_cs, gates_s, ds)
        u3 = u.reshape(db, ds, POOL_WIDTH)
        ext = jnp.concatenate([jnp.zeros((db, POOL_HALO - POOL_STATE, POOL_WIDTH), F32), state_pool[l], u3], axis=1)
        pool_out = _pool_sample(ext.transpose(1, 0, 2), wp, scale, past, ds)
        pool_out = pool_out.reshape(ds, db, POOL_WIDTH).transpose(1, 0, 2).reshape(db * ds, POOL_WIDTH)
        mix = (o_mix[:, :ds].reshape(db * ds, NSA_WIDTH), pool_out, wo_nsa, wo_pool)
        xs = _ffn(xs, *ffn2, g_final=n_final if last else None, mix=mix)
        st_s[0].append(_rows_to_state(kvc_n))
        st_s[1].append(_rows_to_state(kvs_n))
        st_s[2].append(_rows_to_state(st_win_new))
        st_s[3].append(jnp.concatenate([state_pool[l], u3], axis=1)[:, ds:])

    return (xp.reshape(b, t, d), xs.reshape(db, ds, d),
            jnp.stack(st_p[0]), jnp.stack(st_p[1]), jnp.stack(st_p[2]), jnp.stack(st_p[3]),
            jnp.stack(st_s[0]), jnp.stack(st_s[1]), jnp.stack(st_s[2]), jnp.stack(st_s[3]))
```
